```python
import math
import jax
import jax.numpy as jnp
from jax import lax
import numpy as np


D_MODEL = 1024
BATCH = 2
SEQ = 16384
DEPTH = 2

HEAD_DIM = 64
BLOCK_Q = 128
ROPE_THETA = 10000.0
LN_EPS = 1e-5
N_SB_HEADS = D_MODEL // (2 * HEAD_DIM)
N_DSA_HEADS = D_MODEL // (2 * HEAD_DIM)
N_IDX_HEADS = 4
IDX_DIM = 64
DSA_TOPK_MAX = 256
N_DIFF_HEADS = D_MODEL // (2 * HEAD_DIM)
DIFF_DIM = HEAD_DIM
D_FF = ((-(-8 * D_MODEL // 3)) + 255) // 256 * 256
SB_W = N_SB_HEADS * HEAD_DIM
DSA_W = N_DSA_HEADS * HEAD_DIM
EVEN_SPLITS = (SB_W, SB_W, SB_W, DSA_W, HEAD_DIM, HEAD_DIM, N_IDX_HEADS * IDX_DIM, IDX_DIM, N_IDX_HEADS)
EVEN_IN = sum(EVEN_SPLITS)
EVEN_OUT = SB_W + DSA_W
DIFF_W = N_DIFF_HEADS * 2 * DIFF_DIM
N_EVEN = (DEPTH + 1) // 2
N_ODD = DEPTH // 2
ALPHA = (2 * DEPTH) ** 0.25
BETA = (8 * DEPTH) ** -0.25

kernel_name = 'hybrid_sb_dsa_diff_block'

F32 = jnp.float32


def layer_norm(x, g, b):
    x32 = x.astype(F32)
    mu = jnp.mean(x32, axis=-1, keepdims=True)
    var = jnp.mean(jnp.square(x32 - mu), axis=-1, keepdims=True)
    return ((x32 - mu) * lax.rsqrt(var + LN_EPS) * g.astype(F32) + b.astype(F32)).astype(x.dtype)


def rms_norm(x, g):
    x32 = x.astype(F32)
    ms = jnp.mean(jnp.square(x32), axis=-1, keepdims=True)
    return (x32 * lax.rsqrt(ms + LN_EPS) * g.astype(F32)).astype(x.dtype)


def rope_tables(positions, dim):
    inv = ROPE_THETA ** (-jnp.arange(0, dim, 2, dtype=F32) / dim)
    ang = positions.astype(F32)[..., None] * inv
    return jnp.cos(ang)[:, :, None, :], jnp.sin(ang)[:, :, None, :]


def apply_rope(t, cos, sin):
    t32 = t.astype(F32)
    half = t.shape[-1] // 2
    t1, t2 = t32[..., :half], t32[..., half:]
    return jnp.concatenate([t1 * cos - t2 * sin, t2 * cos + t1 * sin], axis=-1).astype(t.dtype)


def to_blocks(t):
    b, s = t.shape[:2]
    return t.reshape((b, s // BLOCK_Q, BLOCK_Q) + t.shape[2:]).swapaxes(0, 1)


def from_blocks(o):
    nb, b, bq = o.shape[:3]
    return o.swapaxes(0, 1).reshape((b, nb * bq) + o.shape[3:])


def stick_breaking_attention(q, k, v):
    s_len, d = q.shape[1], q.shape[-1]
    scale = d ** -0.5
    key_pos = jnp.arange(s_len)

    def block(args):
        qb, start = args
        z = jnp.einsum('bqhd,bshd->bhqs', qb, k, preferred_element_type=F32) * scale
        qpos = start + jnp.arange(BLOCK_Q)
        past = key_pos[None, :] < qpos[:, None]
        log_keep = jnp.where(past, jax.nn.log_sigmoid(-z), 0.0)
        suffix = lax.cumsum(log_keep, axis=3, reverse=True) - log_keep
        w = jnp.where(past, jnp.exp(jax.nn.log_sigmoid(z) + suffix), 0.0)
        return jnp.einsum('bhqs,bshd->bqhd', w.astype(v.dtype), v)

    starts = jnp.arange(s_len // BLOCK_Q) * BLOCK_Q
    return from_blocks(lax.map(block, (to_blocks(q), starts)))


def dsa_attention(q, k, v, q_idx, k_idx, w_idx):
    s_len, d = q.shape[1], q.shape[-1]
    topk = min(DSA_TOPK_MAX, s_len // 4)
    key_pos = jnp.arange(s_len)
    gather = jax.vmap(lambda t, i: t[i])

    def block(args):
        qb, qib, wb, start = args
        qpos = start + jnp.arange(BLOCK_Q)
        causal = key_pos[None, :] <= qpos[:, None]
        logits = jnp.einsum('bqhe,bse->bhqs', qib, k_idx, preferred_element_type=F32) * IDX_DIM ** -0.5
        score = jnp.einsum('bhqs,bqh->bqs', jax.nn.relu(logits), wb.astype(F32))
        score = jnp.where(causal[None], score, -jnp.inf)
        _, sel = lax.top_k(score, topk)
        valid = sel <= qpos[None, :, None]
        kg = gather(k, sel)
        vg = gather(v, sel)
        s = jnp.einsum('bqhd,bqkd->bhqk', qb, kg, preferred_element_type=F32) * d ** -0.5
        s = jnp.where(valid[:, None], s, -jnp.inf)
        p = jax.nn.softmax(s, axis=-1)
        return jnp.einsum('bhqk,bqkd->bqhd', p.astype(vg.dtype), vg)

    starts = jnp.arange(s_len // BLOCK_Q) * BLOCK_Q
    return from_blocks(lax.map(block, (to_blocks(q), to_blocks(q_idx), to_blocks(w_idx), starts)))


def diff_attention(q, k, v, lam):
    s_len, d = q.shape[1], q.shape[-1]
    key_pos = jnp.arange(s_len)

    def block(args):
        qb, start = args
        qpos = start + jnp.arange(BLOCK_Q)
        causal = key_pos[None, :] <= qpos[:, None]
        s = jnp.einsum('bqhcd,bshcd->bhcqs', qb, k, preferred_element_type=F32) * d ** -0.5
        p = jax.nn.softmax(jnp.where(causal, s, -jnp.inf), axis=-1)
        a = p[:, :, 0] - lam * p[:, :, 1]
        return jnp.einsum('bhqs,bshe->bqhe', a.astype(v.dtype), v)

    starts = jnp.arange(s_len // BLOCK_Q) * BLOCK_Q
    return from_blocks(lax.map(block, (to_blocks(q), starts)))


def even_mixer(h, cos, sin, w_in, w_out):
    b, s, _ = h.shape
    idx, acc = [], 0
    for width in EVEN_SPLITS[:-1]:
        acc += width
        idx.append(acc)
    q_sb, k_sb, v_sb, q_dsa, k_dsa, v_dsa, q_ix, k_ix, w_ix = jnp.split(h @ w_in, idx, axis=-1)
    o_sb = stick_breaking_attention(q_sb.reshape(b, s, N_SB_HEADS, HEAD_DIM),
                                    k_sb.reshape(b, s, N_SB_HEADS, HEAD_DIM),
                                    v_sb.reshape(b, s, N_SB_HEADS, HEAD_DIM))
    o_dsa = dsa_attention(apply_rope(q_dsa.reshape(b, s, N_DSA_HEADS, HEAD_DIM), cos, sin),
                          apply_rope(k_dsa[:, :, None], cos, sin)[:, :, 0],
                          v_dsa,
                          apply_rope(q_ix.reshape(b, s, N_IDX_HEADS, IDX_DIM), cos, sin),
                          apply_rope(k_ix[:, :, None], cos, sin)[:, :, 0],
                          w_ix * N_IDX_HEADS ** -0.5)
    o = jnp.concatenate([o_sb.reshape(b, s, SB_W), o_dsa.reshape(b, s, DSA_W)], axis=-1)
    return o @ w_out


def odd_mixer(h, cos, sin, w_in, lam_q1, lam_k1, lam_q2, lam_k2, subln_g, w_out, lambda_init):
    b, s, _ = h.shape
    q, k, v = jnp.split(h @ w_in, 3, axis=-1)
    q = apply_rope(q.reshape(b, s, 2 * N_DIFF_HEADS, DIFF_DIM), cos, sin).reshape(b, s, N_DIFF_HEADS, 2, DIFF_DIM)
    k = apply_rope(k.reshape(b, s, 2 * N_DIFF_HEADS, DIFF_DIM), cos, sin).reshape(b, s, N_DIFF_HEADS, 2, DIFF_DIM)
    v = v.reshape(b, s, N_DIFF_HEADS, 2 * DIFF_DIM)
    lam = (jnp.exp(jnp.sum(lam_q1.astype(F32) * lam_k1.astype(F32)))
           - jnp.exp(jnp.sum(lam_q2.astype(F32) * lam_k2.astype(F32))) + lambda_init)
    o = diff_attention(q, k, v, lam)
    o = rms_norm(o, subln_g) * (1.0 - lambda_init)
    return o.reshape(b, s, DIFF_W) @ w_out


def swiglu(h, w_gate, w_up, w_down):
    return (jax.nn.silu(h @ w_gate) * (h @ w_up)) @ w_down


def setup_inputs(seed: int = 0) -> dict:
    key = jax.random.key(seed)
    ks = jax.random.split(key, 24)

    def nrm(i, shape, scale):
        return jax.random.normal(ks[i], shape, F32) * scale

    return {
        'x': nrm(0, (BATCH, SEQ, D_MODEL), 1.0),
        'c': nrm(1, (BATCH, D_MODEL), 1.0),
        'positions': jnp.broadcast_to(jnp.arange(SEQ, dtype=jnp.int32), (BATCH, SEQ)),
        'w_mod': nrm(2, (DEPTH, D_MODEL, 6 * D_MODEL), 0.1 * D_MODEL ** -0.5),
        'b_mod': nrm(3, (DEPTH, 6 * D_MODEL), 0.01),
        'w_in_even': nrm(4, (N_EVEN, D_MODEL, EVEN_IN), D_MODEL ** -0.5),
        'w_out_even': nrm(5, (N_EVEN, EVEN_OUT, D_MODEL), BETA * EVEN_OUT ** -0.5),
        'w_in_odd': nrm(6, (N_ODD, D_MODEL, 3 * DIFF_W), D_MODEL ** -0.5),
        'lam_q1': nrm(7, (N_ODD, DIFF_DIM), 0.1),
        'lam_k1': nrm(8, (N_ODD, DIFF_DIM), 0.1),
        'lam_q2': nrm(9, (N_ODD, DIFF_DIM), 0.1),
        'lam_k2': nrm(10, (N_ODD, DIFF_DIM), 0.1),
        'subln_g': 1.0 + nrm(11, (N_ODD, 2 * DIFF_DIM), 0.02),
        'w_out_odd': nrm(12, (N_ODD, DIFF_W, D_MODEL), BETA * DIFF_W ** -0.5),
        'ln_mix_g': 1.0 + nrm(13, (DEPTH, D_MODEL), 0.02),
        'ln_mix_b': nrm(14, (DEPTH, D_MODEL), 0.02),
        'w_gate': nrm(15, (DEPTH, D_MODEL, D_FF), D_MODEL ** -0.5),
        'w_up': nrm(16, (DEPTH, D_MODEL, D_FF), D_MODEL ** -0.5),
        'w_down': nrm(17, (DEPTH, D_FF, D_MODEL), BETA * D_FF ** -0.5),
        'ln_ffn_g': 1.0 + nrm(18, (DEPTH, D_MODEL), 0.02),
        'ln_ffn_b': nrm(19, (DEPTH, D_MODEL), 0.02),
    }


def reference(x, c, positions, w_mod, b_mod, w_in_even, w_out_even, w_in_odd, lam_q1, lam_k1,
              lam_q2, lam_k2, subln_g, w_out_odd, ln_mix_g, ln_mix_b, w_gate, w_up, w_down,
              ln_ffn_g, ln_ffn_b):
    cos, sin = rope_tables(positions, HEAD_DIM)
    mod = jnp.einsum('bd,ldm->lbm', jax.nn.silu(c), w_mod) + b_mod[:, None, :]
    for i in range(DEPTH):
        sh_m, sc_m, g_m, sh_f, sc_f, g_f = [t[:, None, :] for t in jnp.split(mod[i], 6, axis=-1)]
        h = x * (1.0 + sc_m) + sh_m
        if i % 2 == 0:
            y = even_mixer(h, cos, sin, w_in_even[i // 2], w_out_even[i // 2])
        else:
            j = i // 2
            lambda_init = 0.8 - 0.6 * math.exp(-0.3 * i)
            y = odd_mixer(h, cos, sin, w_in_odd[j], lam_q1[j], lam_k1[j], lam_q2[j], lam_k2[j],
                          subln_g[j], w_out_odd[j], lambda_init)
        x = layer_norm(ALPHA * x + (1.0 + g_m) * y, ln_mix_g[i], ln_mix_b[i])
        h = x * (1.0 + sc_f) + sh_f
        y = swiglu(h, w_gate[i], w_up[i], w_down[i])
        x = layer_norm(ALPHA * x + (1.0 + g_f) * y, ln_ffn_g[i], ln_ffn_b[i])
    return x
```

```python
import functools
import math

import jax
import jax.numpy as jnp
from jax import lax
from jax.experimental import pallas as pl
from jax.experimental.pallas import tpu as pltpu

HEAD_DIM = 64
N_IDX_HEADS = 4
DSA_TOPK_MAX = 256
ROPE_THETA = 10000.0
LN_EPS = 1e-5
LANES = 128
NEG_BIG = -1e30
INT_MIN = -2 ** 31

F32 = jnp.float32
BF16 = jnp.bfloat16
I32 = jnp.int32

VMEM_LIMIT = 56 * 1024 * 1024


def _cparams(sem):
    return pltpu.CompilerParams(dimension_semantics=sem, vmem_limit_bytes=VMEM_LIMIT)


def _nt(a, b):
    return lax.dot_general(a, b, (((1,), (1,)), ((), ())), preferred_element_type=F32)


def _nn(a, b):
    return jnp.dot(a, b, preferred_element_type=F32)


def _layer_norm(v, g, b):
    mu = jnp.mean(v, axis=-1, keepdims=True)
    d = v - mu
    var = jnp.mean(d * d, axis=-1, keepdims=True)
    return d * lax.rsqrt(var + LN_EPS) * g + b


def _mod_kernel(c_ref, w_ref, b_ref, o_ref):
    c = c_ref[...]
    a = (c * jax.nn.sigmoid(c)).astype(BF16)
    o_ref[0] = _nn(a, w_ref[0]) + b_ref[0]


def _modulation(c, w_mod_bf, b_mod):
    depth, d, n = w_mod_bf.shape
    bsz = c.shape[0]
    tn = n // 4
    return pl.pallas_call(
        _mod_kernel,
        grid=(depth, n // tn),
        in_specs=[pl.BlockSpec((bsz, d), lambda l, j: (0, 0)),
                  pl.BlockSpec((1, d, tn), lambda l, j: (l, 0, j)),
                  pl.BlockSpec((1, 1, tn), lambda l, j: (l, 0, j))],
        out_specs=pl.BlockSpec((1, bsz, tn), lambda l, j: (l, 0, j)),
        out_shape=jax.ShapeDtypeStruct((depth, bsz, n), F32),
        name="modulation",
        compiler_params=_cparams(("arbitrary", "arbitrary")),
    )(c, w_mod_bf, b_mod.reshape(depth, 1, n))


def _inproj_kernel(x_ref, mod_ref, cos_ref, sg_ref, wp_ref, wr_ref, wrr_ref, *out_refs,
                   plain_outs, rope_outs):
    x = x_ref[...]
    sh = mod_ref[0, 0:1, :]
    sc = mod_ref[0, 1:2, :]
    h = (x * (1.0 + sc) + sh).astype(BF16)
    plain = _nn(h, wp_ref[...])
    rope = _nn(h, wr_ref[...])
    rot = _nn(h, wrr_ref[...])
    cos = cos_ref[...]
    sg = sg_ref[...]
    n = 0
    for (c0, width, scale) in plain_outs:
        o = out_refs[n]
        v = plain[:, c0:c0 + width]
        if scale != 1.0:
            v = v * scale
        if o.dtype == F32:
            v = v.astype(BF16).astype(F32)
        o[...] = v.astype(o.dtype)
        n += 1
    for (c0, width, scale) in rope_outs:
        o = out_refs[n]
        for j in range(width // LANES):
            sl = slice(c0 + j * LANES, c0 + (j + 1) * LANES)
            v = rope[:, sl] * cos + rot[:, sl] * sg
            if scale != 1.0:
                v = v * scale
            o[:, j * LANES:(j + 1) * LANES] = v.astype(o.dtype)
        n += 1


def _inproj(xf, mod_l, cos128, sg128, wp, wr, wrr, plain_outs, rope_outs, out_dtypes, rows_per_batch, tm):
    r, d = xf.shape
    tiles_per_batch = rows_per_batch // tm
    widths = [w for (_, w, _) in plain_outs] + [w for (_, w, _) in rope_outs]
    row = lambda i: (i, 0)
    const = lambda i: (0, 0)
    return pl.pallas_call(
        functools.partial(_inproj_kernel, plain_outs=plain_outs, rope_outs=rope_outs),
        grid=(r // tm,),
        in_specs=[pl.BlockSpec((tm, d), row),
                  pl.BlockSpec((1, 6, d), lambda i: (i // tiles_per_batch, 0, 0)),
                  pl.BlockSpec((tm, LANES), row),
                  pl.BlockSpec((tm, LANES), row),
                  pl.BlockSpec(wp.shape, const),
                  pl.BlockSpec(wr.shape, const),
                  pl.BlockSpec(wrr.shape, const)],
        out_specs=[pl.BlockSpec((tm, w), row) for w in widths],
        out_shape=[jax.ShapeDtypeStruct((r, w), dt) for w, dt in zip(widths, out_dtypes)],
        name="inproj",
        compiler_params=_cparams(("arbitrary",)),
    )(xf, mod_l, cos128, sg128, wp, wr, wrr)


def _rot_partner(w):
    d, n = w.shape
    return w.reshape(d, n // HEAD_DIM, 2, HEAD_DIM // 2)[:, :, ::-1, :].reshape(d, n)


def _outproj_kernel(*refs, n_in, alpha):
    o_refs = refs[:n_in]
    w_refs = refs[n_in:2 * n_in]
    x_ref, mod_ref, g_ref, b_ref, out_ref = refs[2 * n_in:]
    y = _nn(o_refs[0][...], w_refs[0][...])
    for a, w in zip(o_refs[1:], w_refs[1:]):
        y = y + _nn(a[...], w[...])
    gate = mod_ref[0, 2:3, :]
    v = alpha * x_ref[...] + (1.0 + gate) * y
    out_ref[...] = _layer_norm(v, g_ref[...], b_ref[...])


def _outproj_ln(o_list, w_list, xf, mod_l, g, b, alpha, rows_per_batch, tm):
    r, d = xf.shape
    tiles_per_batch = rows_per_batch // tm
    row = lambda i: (i, 0)
    const = lambda i: (0, 0)
    n_in = len(o_list)
    return pl.pallas_call(
        functools.partial(_outproj_kernel, n_in=n_in, alpha=alpha),
        grid=(r // tm,),
        in_specs=([pl.BlockSpec((tm, o.shape[1]), row) for o in o_list]
                  + [pl.BlockSpec(w.shape, const) for w in w_list]
                  + [pl.BlockSpec((tm, d), row),
                     pl.BlockSpec((1, 6, d), lambda i: (i // tiles_per_batch, 0, 0)),
                     pl.BlockSpec((1, d), const),
                     pl.BlockSpec((1, d), const)]),
        out_specs=pl.BlockSpec((tm, d), row),
        out_shape=jax.ShapeDtypeStruct((r, d), F32),
        name="outproj_ln",
        compiler_params=_cparams(("arbitrary",)),
    )(*o_list, *w_list, xf, mod_l, g.reshape(1, d), b.reshape(1, d))


def _ffn_kernel(x_ref, mod_ref, wg_ref, wu_ref, wd_ref, g_ref, b_ref, out_ref, h_ref, acc_ref, *, alpha):
    f = pl.program_id(1)

    @pl.when(f == 0)
    def _():
        sh = mod_ref[0, 3:4, :]
        sc = mod_ref[0, 4:5, :]
        h_ref[...] = (x_ref[...] * (1.0 + sc) + sh).astype(BF16)
        acc_ref[...] = jnp.zeros_like(acc_ref)

    h = h_ref[...]
    gate = _nn(h, wg_ref[...])
    up = _nn(h, wu_ref[...])
    a = (gate * jax.nn.sigmoid(gate) * up).astype(BF16)
    acc_ref[...] += _nn(a, wd_ref[...])

    @pl.when(f == pl.num_programs(1) - 1)
    def _():
        gf = mod_ref[0, 5:6, :]
        v = alpha * x_ref[...] + (1.0 + gf) * acc_ref[...]
        out_ref[...] = _layer_norm(v, g_ref[...], b_ref[...])


def _ffn_ln(xf, mod_l, wg, wu, wd, g, b, alpha, rows_per_batch, tm, tf):
    r, d = xf.shape
    dff = wg.shape[1]
    tiles_per_batch = rows_per_batch // tm
    return pl.pallas_call(
        functools.partial(_ffn_kernel, alpha=alpha),
        grid=(r // tm, dff // tf),
        in_specs=[pl.BlockSpec((tm, d), lambda i, f: (i, 0)),
                  pl.BlockSpec((1, 6, d), lambda i, f: (i // tiles_per_batch, 0, 0)),
                  pl.BlockSpec((d, tf), lambda i, f: (0, f)),
                  pl.BlockSpec((d, tf), lambda i, f: (0, f)),
                  pl.BlockSpec((tf, d), lambda i, f: (f, 0)),
                  pl.BlockSpec((1, d), lambda i, f: (0, 0)),
                  pl.BlockSpec((1, d), lambda i, f: (0, 0))],
        out_specs=pl.BlockSpec((tm, d), lambda i, f: (i, 0)),
        out_shape=jax.ShapeDtypeStruct((r, d), F32),
        scratch_shapes=[pltpu.VMEM((tm, d), BF16), pltpu.VMEM((tm, d), F32)],
        name="ffn_ln",
        compiler_params=_cparams(("arbitrary", "arbitrary")),
    )(xf, mod_l, wg, wu, wd, g.reshape(1, d), b.reshape(1, d))


def _lane_half_masks():
    lane = lax.broadcasted_iota(I32, (1, LANES), 1)
    return lane < HEAD_DIM, lane >= HEAD_DIM


def _sb_kernel(q_ref, k_ref, v_ref, o_ref, *, blk):
    qi = pl.program_id(2)
    q = q_ref[0]
    lo_half, hi_half = _lane_half_masks()
    zero = jnp.zeros_like(q)
    qs = (jnp.where(lo_half, q, zero), jnp.where(hi_half, q, zero))
    rr = lax.broadcasted_iota(I32, (blk, blk), 0)
    cc = lax.broadcasted_iota(I32, (blk, blk), 1)
    later = (rr > cc).astype(BF16)
    past = cc < rr

    def step(kb, carry, diag):
        off = pl.multiple_of(kb * blk, blk)
        k = k_ref[0, pl.ds(off, blk), :]
        v = v_ref[0, pl.ds(off, blk), :]
        out = []
        for hd in range(2):
            c, acc = carry[hd]
            z = _nt(qs[hd], k)
            sp = jnp.maximum(z, 0.0) + jnp.log(1.0 + jnp.exp(-jnp.abs(z)))
            if diag:
                sp = jnp.where(past, sp, 0.0)
            sp_hi = sp.astype(BF16)
            sp_lo = (sp - sp_hi.astype(F32)).astype(BF16)
            tail = _nn(sp_hi, later) + _nn(sp_lo, later)
            w = jnp.exp(z - sp - tail - c)
            if diag:
                w = jnp.where(past, w, 0.0)
            acc = acc + _nn(w.astype(BF16), v)
            c = c + tail[:, 0:1] + sp[:, 0:1]
            out.append((c, acc))
        return tuple(out)

    init = tuple((jnp.zeros((blk, 1), F32), jnp.zeros((blk, LANES), F32)) for _ in range(2))
    carry = step(qi, init, True)
    carry = lax.fori_loop(0, qi, lambda i, cr: step(qi - 1 - i, cr, False), carry)
    o_ref[0] = jnp.where(lo_half, carry[0][1], carry[1][1]).astype(o_ref.dtype)


def _sb_attention(q, k, v, blk):
    bsz, s, w = q.shape
    return pl.pallas_call(
        functools.partial(_sb_kernel, blk=blk),
        grid=(bsz, w // LANES, s // blk),
        in_specs=[pl.BlockSpec((1, blk, LANES), lambda b, h, i: (b, i, h)),
                  pl.BlockSpec((1, s, LANES), lambda b, h, i: (b, 0, h)),
                  pl.BlockSpec((1, s, LANES), lambda b, h, i: (b, 0, h))],
        out_specs=pl.BlockSpec((1, blk, LANES), lambda b, h, i: (b, i, h)),
        out_shape=jax.ShapeDtypeStruct((bsz, s, w), BF16),
        name="sb_attention",
        compiler_params=_cparams(("arbitrary", "arbitrary", "arbitrary")),
    )(q, k, v)


def _flash_update(s, m, l, acc, v):
    m_new = jnp.maximum(m, jnp.max(s, axis=1, keepdims=True))
    p = jnp.exp(s - m_new)
    a = jnp.exp(m - m_new)
    l = a * l + jnp.sum(p, axis=1, keepdims=True)
    acc = a * acc + _nn(p.astype(BF16), v)
    return m_new, l, acc


def _dsa_kernel(qd_ref, qx_ref, w_ref, kd_ref, kx_ref, vd_ref, o_ref, key_ref, *, blk, topk):
    qb = pl.program_id(1)
    nkb = qb + 1
    lo_half, hi_half = _lane_half_masks()
    halves = (lo_half, hi_half)
    n_heads = qd_ref.shape[2] // HEAD_DIM
    rr = lax.broadcasted_iota(I32, (blk, blk), 0)
    cc = lax.broadcasted_iota(I32, (blk, blk), 1)

    qx = qx_ref[0]
    wx = w_ref[0]
    qx_heads = []
    for hx in range(N_IDX_HEADS):
        chunk = qx[:, (hx // 2) * LANES:(hx // 2 + 1) * LANES]
        qx_heads.append(jnp.where(halves[hx % 2], chunk, jnp.zeros_like(chunk)))

    def score_block(kb, _):
        off = pl.multiple_of(kb * blk, blk)
        kx = kx_ref[0, pl.ds(off, blk), :]
        score = jnp.zeros((blk, blk), F32)
        for hx in range(N_IDX_HEADS):
            logit = _nt(qx_heads[hx], kx)
            r = jnp.maximum(logit, 0.0).astype(BF16).astype(F32)
            score = score + r * wx[:, hx:hx + 1]
        bits = lax.bitcast_convert_type(score, I32)
        key = bits ^ (lax.shift_right_arithmetic(bits, 31) & 0x7FFFFFFF)
        key = jnp.where(key == -1, 0, key)
        causal = (cc + (kb - qb) * blk) <= rr
        key_ref[kb] = jnp.where(causal, key, INT_MIN)
        return 0

    lax.fori_loop(0, nkb, score_block, 0)

    row_t = qb * blk + lax.broadcasted_iota(I32, (blk, 1), 0)
    k_eff = jnp.minimum(topk, row_t + 1).astype(F32)

    def count_ge(thr):
        def body(kb, acc):
            hit = jnp.where(key_ref[kb] >= thr, 1.0, 0.0)
            part = hit[:, 0:LANES]
            for j in range(1, blk // LANES):
                part = part + hit[:, j * LANES:(j + 1) * LANES]
            return acc + part
        acc = lax.fori_loop(0, nkb, body, jnp.zeros((blk, LANES), F32))
        return jnp.sum(acc, axis=1, keepdims=True)

    def bit_step(i, prefix):
        cand = prefix | lax.shift_left(jnp.int32(1), 31 - i)
        cnt = count_ge(cand ^ INT_MIN)
        return jnp.where(cnt >= k_eff, cand, prefix)

    prefix = lax.fori_loop(0, 32, bit_step, jnp.zeros((blk, 1), I32))
    thr = prefix ^ INT_MIN
    need = k_eff - count_ge(thr + 1)

    earlier = (rr < cc).astype(BF16)

    def mask_block(kb, seen):
        key = key_ref[kb]
        eq = jnp.where(key == thr, 1.0, 0.0)
        rank = _nn(eq.astype(BF16), earlier) + seen
        take = jnp.where(rank < need, 1, 0)
        sel = (key + take) > thr
        key_ref[kb] = lax.bitcast_convert_type(jnp.where(sel, 0.0, NEG_BIG), I32)
        return seen + jnp.sum(eq, axis=1, keepdims=True)

    lax.fori_loop(0, nkb, mask_block, jnp.zeros((blk, 1), F32))

    qd = qd_ref[0]
    for hd in range(n_heads):
        chunk = qd[:, (hd // 2) * LANES:(hd // 2 + 1) * LANES]
        qh = jnp.where(halves[hd % 2], chunk, jnp.zeros_like(chunk))

        def attn_block(kb, carry, qh=qh):
            m, l, acc = carry
            off = pl.multiple_of(kb * blk, blk)
            kd = kd_ref[0, pl.ds(off, blk), :]
            vd = vd_ref[0, pl.ds(off, blk), :]
            s = _nt(qh, kd) + lax.bitcast_convert_type(key_ref[kb], F32)
            return _flash_update(s, m, l, acc, vd)

        init = (jnp.full((blk, 1), NEG_BIG, F32), jnp.zeros((blk, 1), F32), jnp.zeros((blk, LANES), F32))
        m, l, acc = lax.fori_loop(0, nkb, attn_block, init)
        o = acc / l
        src = (hd % 2) * HEAD_DIM
        o_ref[0, :, hd * HEAD_DIM:(hd + 1) * HEAD_DIM] = o[:, src:src + HEAD_DIM].astype(o_ref.dtype)


def _dsa_attention(qd, qx, wx, kd2, kx2, vd2, blk, topk):
    bsz, s, wq = qd.shape
    kspec = pl.BlockSpec((1, s, LANES), lambda b, i: (b, 0, 0))
    return pl.pallas_call(
        functools.partial(_dsa_kernel, blk=blk, topk=topk),
        grid=(bsz, s // blk),
        in_specs=[pl.BlockSpec((1, blk, wq), lambda b, i: (b, i, 0)),
                  pl.BlockSpec((1, blk, qx.shape[2]), lambda b, i: (b, i, 0)),
                  pl.BlockSpec((1, blk, LANES), lambda b, i: (b, i, 0)),
                  kspec, kspec, kspec],
        out_specs=pl.BlockSpec((1, blk, wq), lambda b, i: (b, i, 0)),
        out_shape=jax.ShapeDtypeStruct((bsz, s, wq), BF16),
        scratch_shapes=[pltpu.VMEM((s // blk, blk, blk), I32)],
        name="dsa_attention",
        compiler_params=_cparams(("arbitrary", "arbitrary")),
    )(qd, qx, wx, kd2, kx2, vd2)


def _diff_kernel(q_ref, k_ref, v_ref, lam_ref, g_ref, o_ref, *, blk, lambda_init):
    qi = pl.program_id(2)
    q = q_ref[0]
    lo_half, hi_half = _lane_half_masks()
    zero = jnp.zeros_like(q)
    qs = (jnp.where(lo_half, q, zero), jnp.where(hi_half, q, zero))
    rr = lax.broadcasted_iota(I32, (blk, blk), 0)
    cc = lax.broadcasted_iota(I32, (blk, blk), 1)
    causal = cc <= rr

    def step(kb, carry, diag):
        off = pl.multiple_of(kb * blk, blk)
        k = k_ref[0, pl.ds(off, blk), :]
        v = v_ref[0, pl.ds(off, blk), :]
        out = []
        for c in range(2):
            s = _nt(qs[c], k)
            if diag:
                s = jnp.where(causal, s, NEG_BIG)
            out.append(_flash_update(s, *carry[c], v))
        return tuple(out)

    init = tuple((jnp.full((blk, 1), NEG_BIG, F32), jnp.zeros((blk, 1), F32), jnp.zeros((blk, LANES), F32))
                 for _ in range(2))
    carry = step(qi, init, True)
    carry = lax.fori_loop(0, qi, lambda i, cr: step(i, cr, False), carry)

    lp = lam_ref[...]
    lam = (jnp.exp(jnp.sum(lp[0:1] * lp[1:2], axis=1, keepdims=True))
           - jnp.exp(jnp.sum(lp[2:3] * lp[3:4], axis=1, keepdims=True)) + lambda_init)
    (_, l0, a0), (_, l1, a1) = carry
    o = a0 / l0 - lam * (a1 / l1)
    ms = jnp.mean(o * o, axis=-1, keepdims=True)
    o_ref[0] = (o * lax.rsqrt(ms + LN_EPS) * g_ref[...] * (1.0 - lambda_init)).astype(o_ref.dtype)


def _diff_attention(q, k, v, lam_params, subln_g, blk, lambda_init):
    bsz, s, w = q.shape
    return pl.pallas_call(
        functools.partial(_diff_kernel, blk=blk, lambda_init=lambda_init),
        grid=(bsz, w // LANES, s // blk),
        in_specs=[pl.BlockSpec((1, blk, LANES), lambda b, h, i: (b, i, h)),
                  pl.BlockSpec((1, s, LANES), lambda b, h, i: (b, 0, h)),
                  pl.BlockSpec((1, s, LANES), lambda b, h, i: (b, 0, h)),
                  pl.BlockSpec(lam_params.shape, lambda b, h, i: (0, 0)),
                  pl.BlockSpec((1, LANES), lambda b, h, i: (0, 0))],
        out_specs=pl.BlockSpec((1, blk, LANES), lambda b, h, i: (b, i, h)),
        out_shape=jax.ShapeDtypeStruct((bsz, s, w), BF16),
        name="diff_attention",
        compiler_params=_cparams(("arbitrary", "arbitrary", "arbitrary")),
    )(q, k, v, lam_params, subln_g.reshape(1, LANES))


def _pick(n, prefs):
    for p in prefs:
        if n % p == 0:
            return p
    return n


def kernel(x, c, positions, w_mod, b_mod, w_in_even, w_out_even, w_in_odd, lam_q1, lam_k1, lam_q2, lam_k2,
           subln_g, w_out_odd, ln_mix_g, ln_mix_b, w_gate, w_up, w_down, ln_ffn_g, ln_ffn_b):
    bsz, s, d = x.shape
    depth = w_mod.shape[0]
    alpha = (2 * depth) ** 0.25
    dff = w_gate.shape[2]
    rows = bsz * s
    tm = _pick(s, (512, 256, 128))
    tf = _pick(dff, (1408, 1024, 512, 256, 128))
    blk = _pick(s, (256, 128))
    scale = HEAD_DIM ** -0.5

    inv = ROPE_THETA ** (-jnp.arange(0, HEAD_DIM, 2, dtype=F32) / HEAD_DIM)
    ang = positions.astype(F32)[..., None] * inv
    cos, sin = jnp.cos(ang).reshape(rows, -1), jnp.sin(ang).reshape(rows, -1)
    cos128 = jnp.concatenate([cos] * 4, axis=1)
    sg128 = jnp.concatenate([-sin, sin, -sin, sin], axis=1)

    mod = _modulation(c, w_mod.astype(BF16), b_mod)
    xf = x.reshape(rows, d)

    for i in range(depth):
        mod_l = mod[i].reshape(bsz, 6, d)
        if i % 2 == 0:
            w = w_in_even[i // 2]
            n_sb = n_dsa = d // (2 * HEAD_DIM)
            sbw, dsw, ixw = n_sb * HEAD_DIM, n_dsa * HEAD_DIM, N_IDX_HEADS * HEAD_DIM
            offs = [0]
            for width in (sbw, sbw, sbw, dsw, HEAD_DIM, HEAD_DIM, ixw, HEAD_DIM, N_IDX_HEADS):
                offs.append(offs[-1] + width)
            col = lambda j: w[:, offs[j]:offs[j + 1]]
            pad = jnp.zeros((d, LANES - N_IDX_HEADS), w.dtype)
            wp = jnp.concatenate([col(0), col(1), col(2), col(5), col(5), col(8), pad], axis=1)
            wr = jnp.concatenate([col(3), col(6), col(4), col(4), col(7), col(7)], axis=1)
            plain_outs = ((0, sbw, scale), (sbw, sbw, 1.0), (2 * sbw, sbw, 1.0),
                          (3 * sbw, LANES, 1.0), (3 * sbw + LANES, LANES, N_IDX_HEADS ** -0.5))
            rope_outs = ((0, dsw, scale), (dsw, ixw, scale), (dsw + ixw, LANES, 1.0),
                         (dsw + ixw + LANES, LANES, 1.0))
            dts = (BF16, BF16, BF16, BF16, F32, BF16, BF16, BF16, BF16)
            q_sb, k_sb, v_sb, vd2, wx, qd, qx, kd2, kx2 = _inproj(
                xf, mod_l, cos128, sg128, wp.astype(BF16), wr.astype(BF16), _rot_partner(wr).astype(BF16),
                plain_outs, rope_outs, dts, s, tm)
            r3 = lambda t: t.reshape(bsz, s, t.shape[1])
            o_sb = _sb_attention(r3(q_sb), r3(k_sb), r3(v_sb), blk)
            o_dsa = _dsa_attention(r3(qd), r3(qx), r3(wx), r3(kd2), r3(kx2), r3(vd2), blk,
                                   min(DSA_TOPK_MAX, s // 4))
            w_out = w_out_even[i // 2].astype(BF16)
            o_list = [o_sb.reshape(rows, sbw), o_dsa.reshape(rows, dsw)]
            w_list = [w_out[:sbw], w_out[sbw:]]
        else:
            j = i // 2
            w = w_in_odd[j]
            dw = w.shape[1] // 3
            lambda_init = 0.8 - 0.6 * math.exp(-0.3 * i)
            wp = w[:, 2 * dw:]
            wr = w[:, :2 * dw]
            plain_outs = ((0, dw, 1.0),)
            rope_outs = ((0, dw, scale), (dw, dw, 1.0))
            v_df, q_df, k_df = _inproj(
                xf, mod_l, cos128, sg128, wp.astype(BF16), wr.astype(BF16), _rot_partner(wr).astype(BF16),
                plain_outs, rope_outs, (BF16, BF16, BF16), s, tm)
            r3 = lambda t: t.reshape(bsz, s, t.shape[1])
            lam_params = jnp.stack([lam_q1[j], lam_k1[j], lam_q2[j], lam_k2[j]]).astype(F32)
            o_df = _diff_attention(r3(q_df), r3(k_df), r3(v_df), lam_params, subln_g[j].astype(F32), blk,
                                   lambda_init)
            o_list = [o_df.reshape(rows, dw)]
            w_list = [w_out_odd[j].astype(BF16)]
        xf = _outproj_ln(o_list, w_list, xf, mod_l, ln_mix_g[i], ln_mix_b[i], alpha, s, tm)
        xf = _ffn_ln(xf, mod_l, w_gate[i].astype(BF16), w_up[i].astype(BF16), w_down[i].astype(BF16),
                     ln_ffn_g[i], ln_ffn_b[i], alpha, s, tm, tf)
    return xf.reshape(bsz, s, d)
```

```python
import functools
import math

import jax
import jax.numpy as jnp
from jax import lax
from jax.experimental import pallas as pl
from jax.experimental.pallas import tpu as pltpu

HEAD_DIM = 64
N_IDX_HEADS = 4
DSA_TOPK_MAX = 256
ROPE_THETA = 10000.0
LN_EPS = 1e-5
LANES = 128
NEG_BIG = -1e30
INT_MIN = -2 ** 31

F32 = jnp.float32
BF16 = jnp.bfloat16
I32 = jnp.int32

VMEM_LIMIT = 56 * 1024 * 1024


def _cparams(sem):
    return pltpu.CompilerParams(dimension_semantics=sem, vmem_limit_bytes=VMEM_LIMIT)


def _nt(a, b):
    return lax.dot_general(a, b, (((1,), (1,)), ((), ())), preferred_element_type=F32)


def _nn(a, b):
    return jnp.dot(a, b, preferred_element_type=F32)


def _layer_norm(v, g, b):
    mu = jnp.mean(v, axis=-1, keepdims=True)
    d = v - mu
    var = jnp.mean(d * d, axis=-1, keepdims=True)
    return d * lax.rsqrt(var + LN_EPS) * g + b


def _mod_kernel(c_ref, w_ref, b_ref, o_ref):
    c = c_ref[...]
    a = (c * jax.nn.sigmoid(c)).astype(BF16)
    o_ref[0] = _nn(a, w_ref[0]) + b_ref[0]


def _modulation(c, w_mod_bf, b_mod):
    depth, d, n = w_mod_bf.shape
    bsz = c.shape[0]
    tn = n // 4
    return pl.pallas_call(
        _mod_kernel,
        grid=(depth, n // tn),
        in_specs=[pl.BlockSpec((bsz, d), lambda l, j: (0, 0)),
                  pl.BlockSpec((1, d, tn), lambda l, j: (l, 0, j)),
                  pl.BlockSpec((1, 1, tn), lambda l, j: (l, 0, j))],
        out_specs=pl.BlockSpec((1, bsz, tn), lambda l, j: (l, 0, j)),
        out_shape=jax.ShapeDtypeStruct((depth, bsz, n), F32),
        name="modulation",
        compiler_params=_cparams(("arbitrary", "arbitrary")),
    )(c, w_mod_bf, b_mod.reshape(depth, 1, n))


def _inproj_kernel(x_ref, mod_ref, cos_ref, sg_ref, wp_ref, wr_ref, wrr_ref, *out_refs,
                   plain_outs, rope_outs):
    x = x_ref[...]
    sh = mod_ref[0, 0:1, :]
    sc = mod_ref[0, 1:2, :]
    h = (x * (1.0 + sc) + sh).astype(BF16)
    plain = _nn(h, wp_ref[...])
    rope = _nn(h, wr_ref[...])
    rot = _nn(h, wrr_ref[...])
    cos = cos_ref[...]
    sg = sg_ref[...]
    n = 0
    for (c0, width, scale) in plain_outs:
        o = out_refs[n]
        v = plain[:, c0:c0 + width]
        if scale != 1.0:
            v = v * scale
        if o.dtype == F32:
            v = v.astype(BF16).astype(F32)
        o[...] = v.astype(o.dtype)
        n += 1
    for (c0, width, scale) in rope_outs:
        o = out_refs[n]
        for j in range(width // LANES):
            sl = slice(c0 + j * LANES, c0 + (j + 1) * LANES)
            v = rope[:, sl] * cos + rot[:, sl] * sg
            if scale != 1.0:
                v = v * scale
            o[:, j * LANES:(j + 1) * LANES] = v.astype(o.dtype)
        n += 1


def _inproj(xf, mod_l, cos128, sg128, wp, wr, wrr, plain_outs, rope_outs, out_dtypes, rows_per_batch, tm):
    r, d = xf.shape
    tiles_per_batch = rows_per_batch // tm
    widths = [w for (_, w, _) in plain_outs] + [w for (_, w, _) in rope_outs]
    row = lambda i: (i, 0)
    const = lambda i: (0, 0)
    return pl.pallas_call(
        functools.partial(_inproj_kernel, plain_outs=plain_outs, rope_outs=rope_outs),
        grid=(r // tm,),
        in_specs=[pl.BlockSpec((tm, d), row),
                  pl.BlockSpec((1, 6, d), lambda i: (i // tiles_per_batch, 0, 0)),
                  pl.BlockSpec((tm, LANES), row),
                  pl.BlockSpec((tm, LANES), row),
                  pl.BlockSpec(wp.shape, const),
                  pl.BlockSpec(wr.shape, const),
                  pl.BlockSpec(wrr.shape, const)],
        out_specs=[pl.BlockSpec((tm, w), row) for w in widths],
        out_shape=[jax.ShapeDtypeStruct((r, w), dt) for w, dt in zip(widths, out_dtypes)],
        name="inproj",
        compiler_params=_cparams(("arbitrary",)),
    )(xf, mod_l, cos128, sg128, wp, wr, wrr)


def _rot_partner(w):
    d, n = w.shape
    return w.reshape(d, n // HEAD_DIM, 2, HEAD_DIM // 2)[:, :, ::-1, :].reshape(d, n)


def _outproj_kernel(*refs, n_in, alpha):
    o_refs = refs[:n_in]
    w_refs = refs[n_in:2 * n_in]
    x_ref, mod_ref, g_ref, b_ref, out_ref = refs[2 * n_in:]
    y = _nn(o_refs[0][...], w_refs[0][...])
    for a, w in zip(o_refs[1:], w_refs[1:]):
        y = y + _nn(a[...], w[...])
    gate = mod_ref[0, 2:3, :]
    v = alpha * x_ref[...] + (1.0 + gate) * y
    out_ref[...] = _layer_norm(v, g_ref[...], b_ref[...])


def _outproj_ln(o_list, w_list, xf, mod_l, g, b, alpha, rows_per_batch, tm):
    r, d = xf.shape
    tiles_per_batch = rows_per_batch // tm
    row = lambda i: (i, 0)
    const = lambda i: (0, 0)
    n_in = len(o_list)
    return pl.pallas_call(
        functools.partial(_outproj_kernel, n_in=n_in, alpha=alpha),
        grid=(r // tm,),
        in_specs=([pl.BlockSpec((tm, o.shape[1]), row) for o in o_list]
                  + [pl.BlockSpec(w.shape, const) for w in w_list]
                  + [pl.BlockSpec((tm, d), row),
                     pl.BlockSpec((1, 6, d), lambda i: (i // tiles_per_batch, 0, 0)),
                     pl.BlockSpec((1, d), const),
                     pl.BlockSpec((1, d), const)]),
        out_specs=pl.BlockSpec((tm, d), row),
        out_shape=jax.ShapeDtypeStruct((r, d), F32),
        name="outproj_ln",
        compiler_params=_cparams(("arbitrary",)),
    )(*o_list, *w_list, xf, mod_l, g.reshape(1, d), b.reshape(1, d))


def _ffn_kernel(x_ref, mod_ref, wg_ref, wu_ref, wd_ref, g_ref, b_ref, out_ref, h_ref, acc_ref, *, alpha):
    f = pl.program_id(1)

    @pl.when(f == 0)
    def _():
        sh = mod_ref[0, 3:4, :]
        sc = mod_ref[0, 4:5, :]
        h_ref[...] = (x_ref[...] * (1.0 + sc) + sh).astype(BF16)
        acc_ref[...] = jnp.zeros_like(acc_ref)

    h = h_ref[...]
    gate = _nn(h, wg_ref[...])
    up = _nn(h, wu_ref[...])
    a = (gate * jax.nn.sigmoid(gate) * up).astype(BF16)
    acc_ref[...] += _nn(a, wd_ref[...])

    @pl.when(f == pl.num_programs(1) - 1)
    def _():
        gf = mod_ref[0, 5:6, :]
        v = alpha * x_ref[...] + (1.0 + gf) * acc_ref[...]
        out_ref[...] = _layer_norm(v, g_ref[...], b_ref[...])


def _ffn_ln(xf, mod_l, wg, wu, wd, g, b, alpha, rows_per_batch, tm, tf):
    r, d = xf.shape
    dff = wg.shape[1]
    tiles_per_batch = rows_per_batch // tm
    return pl.pallas_call(
        functools.partial(_ffn_kernel, alpha=alpha),
        grid=(r // tm, dff // tf),
        in_specs=[pl.BlockSpec((tm, d), lambda i, f: (i, 0)),
                  pl.BlockSpec((1, 6, d), lambda i, f: (i // tiles_per_batch, 0, 0)),
                  pl.BlockSpec((d, tf), lambda i, f: (0, f)),
                  pl.BlockSpec((d, tf), lambda i, f: (0, f)),
                  pl.BlockSpec((tf, d), lambda i, f: (f, 0)),
                  pl.BlockSpec((1, d), lambda i, f: (0, 0)),
                  pl.BlockSpec((1, d), lambda i, f: (0, 0))],
        out_specs=pl.BlockSpec((tm, d), lambda i, f: (i, 0)),
        out_shape=jax.ShapeDtypeStruct((r, d), F32),
        scratch_shapes=[pltpu.VMEM((tm, d), BF16), pltpu.VMEM((tm, d), F32)],
        name="ffn_ln",
        compiler_params=_cparams(("arbitrary", "arbitrary")),
    )(xf, mod_l, wg, wu, wd, g.reshape(1, d), b.reshape(1, d))


def _lane_half_masks():
    lane = lax.broadcasted_iota(I32, (1, LANES), 1)
    return lane < HEAD_DIM, lane >= HEAD_DIM


def _sb_kernel(q_ref, k_ref, v_ref, o_ref, *, blk, bk, sub):
    qi = pl.program_id(2)
    q = q_ref[0]
    lo_half, hi_half = _lane_half_masks()
    zero = jnp.zeros_like(q)
    qs = (jnp.where(lo_half, q, zero), jnp.where(hi_half, q, zero))
    rs = lax.broadcasted_iota(I32, (sub, sub), 0)
    cs = lax.broadcasted_iota(I32, (sub, sub), 1)
    later = (rs > cs).astype(BF16)
    later2 = jnp.concatenate([later, later], axis=0)
    n_full = (qi * blk) // bk

    def step(kb, carry, diag):
        off = pl.multiple_of(kb * bk, bk)
        k = k_ref[0, pl.ds(off, bk), :]
        v = v_ref[0, pl.ds(off, bk), :]
        if diag:
            rr = lax.broadcasted_iota(I32, (blk, bk), 0) + (qi * blk - kb * bk)
            cc = lax.broadcasted_iota(I32, (blk, bk), 1)
            past = cc < rr
        out = []
        for hd in range(2):
            c, acc = carry[hd]
            z = _nt(qs[hd], k)
            sp = jnp.maximum(z, 0.0) + jnp.log(1.0 + jnp.exp(-jnp.abs(z)))
            if diag:
                sp = jnp.where(past, sp, 0.0)
            sp_hi = sp.astype(BF16)
            sp_lo = (sp - sp_hi.astype(F32)).astype(BF16)
            ws = [None] * (bk // sub)
            for j in reversed(range(bk // sub)):
                sl = slice(j * sub, (j + 1) * sub)
                tail = _nn(jnp.concatenate([sp_hi[:, sl], sp_lo[:, sl]], axis=1), later2)
                ws[j] = jnp.exp(z[:, sl] - sp[:, sl] - tail - c)
                c = c + tail[:, 0:1] + sp[:, j * sub:j * sub + 1]
            w = jnp.concatenate(ws, axis=1)
            if diag:
                w = jnp.where(past, w, 0.0)
            acc = acc + _nn(w.astype(BF16), v)
            out.append((c, acc))
        return tuple(out)

    init = tuple((jnp.zeros((blk, 1), F32), jnp.zeros((blk, LANES), F32)) for _ in range(2))
    carry = step(n_full, init, True)
    carry = lax.fori_loop(0, n_full, lambda i, cr: step(n_full - 1 - i, cr, False), carry)
    o_ref[0] = jnp.where(lo_half, carry[0][1], carry[1][1]).astype(o_ref.dtype)


def _sb_attention(q, k, v, blk, bk):
    bsz, s, w = q.shape
    return pl.pallas_call(
        functools.partial(_sb_kernel, blk=blk, bk=bk, sub=min(bk, 256)),
        grid=(bsz, w // LANES, s // blk),
        in_specs=[pl.BlockSpec((1, blk, LANES), lambda b, h, i: (b, i, h)),
                  pl.BlockSpec((1, s, LANES), lambda b, h, i: (b, 0, h)),
                  pl.BlockSpec((1, s, LANES), lambda b, h, i: (b, 0, h))],
        out_specs=pl.BlockSpec((1, blk, LANES), lambda b, h, i: (b, i, h)),
        out_shape=jax.ShapeDtypeStruct((bsz, s, w), BF16),
        name="sb_attention",
        compiler_params=_cparams(("arbitrary", "arbitrary", "arbitrary")),
    )(q, k, v)


def _flash_update(s, m, l, acc, v):
    m_new = jnp.maximum(m, jnp.max(s, axis=1, keepdims=True))
    p = jnp.exp(s - m_new)
    a = jnp.exp(m - m_new)
    l = a * l + jnp.sum(p, axis=1, keepdims=True)
    acc = a * acc + _nn(p.astype(BF16), v)
    return m_new, l, acc


def _dsa_kernel(qd_ref, qx_ref, w_ref, kd_ref, kx_ref, vd_ref, o_ref, key_ref, *, bq, bk, sub, rows, topk):
    qb = pl.program_id(1)
    nkb = (qb * bq) // bk + 1
    lo_half, hi_half = _lane_half_masks()
    halves = (lo_half, hi_half)
    n_heads = qd_ref.shape[2] // HEAD_DIM
    n_chunks = bq // rows

    qx = qx_ref[0]
    wx = w_ref[0]
    qx_heads = []
    for hx in range(N_IDX_HEADS):
        chunk = qx[:, (hx // 2) * LANES:(hx // 2 + 1) * LANES]
        qx_heads.append(jnp.where(halves[hx % 2], chunk, jnp.zeros_like(chunk)))

    def score_block(kb, _):
        off = pl.multiple_of(kb * bk, bk)
        kx = kx_ref[0, pl.ds(off, bk), :]
        score = jnp.zeros((bq, bk), F32)
        for hx in range(N_IDX_HEADS):
            logit = _nt(qx_heads[hx], kx)
            r = jnp.maximum(logit, 0.0).astype(BF16).astype(F32)
            score = score + r * wx[:, hx:hx + 1]
        bits = lax.bitcast_convert_type(score, I32)
        key = bits ^ (lax.shift_right_arithmetic(bits, 31) & 0x7FFFFFFF)
        key = jnp.where(key == -1, 0, key)
        rr = lax.broadcasted_iota(I32, (bq, bk), 0) + (qb * bq - kb * bk)
        cc = lax.broadcasted_iota(I32, (bq, bk), 1)
        key_ref[kb] = jnp.where(cc <= rr, key, INT_MIN)
        return 0

    lax.fori_loop(0, nkb, score_block, 0)

    def count_ge(ch, thr):
        def body(kb, acc):
            hit = jnp.where(key_ref[kb, ch * rows:(ch + 1) * rows, :] >= thr, 1.0, 0.0)
            part = hit[:, 0:LANES]
            for j in range(1, bk // LANES):
                part = part + hit[:, j * LANES:(j + 1) * LANES]
            return acc + part
        acc = lax.fori_loop(0, nkb, body, jnp.zeros((rows, LANES), F32))
        return jnp.sum(acc, axis=1, keepdims=True)

    k_eff = []
    for ch in range(n_chunks):
        row_t = qb * bq + ch * rows + lax.broadcasted_iota(I32, (rows, 1), 0)
        k_eff.append(jnp.minimum(topk, row_t + 1).astype(F32))

    def bit_step(i, prefixes):
        bit = lax.shift_left(jnp.int32(1), 31 - i)
        out = []
        for ch in range(n_chunks):
            cand = prefixes[ch] | bit
            cnt = count_ge(ch, cand ^ INT_MIN)
            out.append(jnp.where(cnt >= k_eff[ch], cand, prefixes[ch]))
        return tuple(out)

    prefixes = lax.fori_loop(0, 32, bit_step, tuple(jnp.zeros((rows, 1), I32) for _ in range(n_chunks)))
    thr_c = [p ^ INT_MIN for p in prefixes]
    need_c = [k_eff[ch] - count_ge(ch, thr_c[ch] + 1) for ch in range(n_chunks)]
    thr = jnp.concatenate(thr_c, axis=0)
    need = jnp.concatenate(need_c, axis=0)

    rs = lax.broadcasted_iota(I32, (sub, sub), 0)
    cs = lax.broadcasted_iota(I32, (sub, sub), 1)
    earlier = (rs < cs).astype(BF16)

    def mask_block(kb, seen):
        for j in range(bk // sub):
            key = key_ref[kb, :, j * sub:(j + 1) * sub]
            eq = jnp.where(key == thr, 1.0, 0.0)
            rank = _nn(eq.astype(BF16), earlier) + seen
            take = jnp.where(rank < need, 1, 0)
            sel = (key + take) > thr
            key_ref[kb, :, j * sub:(j + 1) * sub] = lax.bitcast_convert_type(jnp.where(sel, 0.0, NEG_BIG), I32)
            seen = seen + jnp.sum(eq, axis=1, keepdims=True)
        return seen

    lax.fori_loop(0, nkb, mask_block, jnp.zeros((bq, 1), F32))

    qd = qd_ref[0]
    for hd in range(n_heads):
        chunk = qd[:, (hd // 2) * LANES:(hd // 2 + 1) * LANES]
        qh = jnp.where(halves[hd % 2], chunk, jnp.zeros_like(chunk))

        def attn_block(kb, carry, qh=qh):
            m, l, acc = carry
            off = pl.multiple_of(kb * bk, bk)
            kd = kd_ref[0, pl.ds(off, bk), :]
            vd = vd_ref[0, pl.ds(off, bk), :]
            s = _nt(qh, kd) + lax.bitcast_convert_type(key_ref[kb], F32)
            return _flash_update(s, m, l, acc, vd)

        init = (jnp.full((bq, 1), NEG_BIG, F32), jnp.zeros((bq, 1), F32), jnp.zeros((bq, LANES), F32))
        m, l, acc = lax.fori_loop(0, nkb, attn_block, init)
        o = acc / l
        src = (hd % 2) * HEAD_DIM
        o_ref[0, :, hd * HEAD_DIM:(hd + 1) * HEAD_DIM] = o[:, src:src + HEAD_DIM].astype(o_ref.dtype)


def _dsa_attention(qd, qx, wx, kd2, kx2, vd2, blk, bk, topk):
    bsz, s, wq = qd.shape
    kspec = pl.BlockSpec((1, s, LANES), lambda b, i: (b, 0, 0))
    return pl.pallas_call(
        functools.partial(_dsa_kernel, bq=blk, bk=bk, sub=min(bk, 256), rows=min(blk, 64), topk=topk),
        grid=(bsz, s // blk),
        in_specs=[pl.BlockSpec((1, blk, wq), lambda b, i: (b, i, 0)),
                  pl.BlockSpec((1, blk, qx.shape[2]), lambda b, i: (b, i, 0)),
                  pl.BlockSpec((1, blk, LANES), lambda b, i: (b, i, 0)),
                  kspec, kspec, kspec],
        out_specs=pl.BlockSpec((1, blk, wq), lambda b, i: (b, i, 0)),
        out_shape=jax.ShapeDtypeStruct((bsz, s, wq), BF16),
        scratch_shapes=[pltpu.VMEM((s // bk, blk, bk), I32)],
        name="dsa_attention",
        compiler_params=_cparams(("arbitrary", "arbitrary")),
    )(qd, qx, wx, kd2, kx2, vd2)


def _diff_kernel(q_ref, k_ref, v_ref, lam_ref, g_ref, o_ref, *, bq, bk, lambda_init):
    qi = pl.program_id(2)
    q = q_ref[0]
    lo_half, hi_half = _lane_half_masks()
    zero = jnp.zeros_like(q)
    qs = (jnp.where(lo_half, q, zero), jnp.where(hi_half, q, zero))
    n_full = (qi * bq) // bk

    def step(kb, carry, diag):
        off = pl.multiple_of(kb * bk, bk)
        k = k_ref[0, pl.ds(off, bk), :]
        v = v_ref[0, pl.ds(off, bk), :]
        out = []
        for c in range(2):
            s = _nt(qs[c], k)
            if diag:
                rr = lax.broadcasted_iota(I32, (bq, bk), 0) + (qi * bq - kb * bk)
                cc = lax.broadcasted_iota(I32, (bq, bk), 1)
                s = jnp.where(cc <= rr, s, NEG_BIG)
            out.append(_flash_update(s, *carry[c], v))
        return tuple(out)

    init = tuple((jnp.full((bq, 1), NEG_BIG, F32), jnp.zeros((bq, 1), F32), jnp.zeros((bq, LANES), F32))
                 for _ in range(2))
    carry = lax.fori_loop(0, n_full, lambda i, cr: step(i, cr, False), init)
    carry = step(n_full, carry, True)

    lp = lam_ref[...]
    lam = (jnp.exp(jnp.sum(lp[0:1] * lp[1:2], axis=1, keepdims=True))
           - jnp.exp(jnp.sum(lp[2:3] * lp[3:4], axis=1, keepdims=True)) + lambda_init)
    (_, l0, a0), (_, l1, a1) = carry
    o = a0 / l0 - lam * (a1 / l1)
    ms = jnp.mean(o * o, axis=-1, keepdims=True)
    o_ref[0] = (o * lax.rsqrt(ms + LN_EPS) * g_ref[...] * (1.0 - lambda_init)).astype(o_ref.dtype)


def _diff_attention(q, k, v, lam_params, subln_g, blk, bk, lambda_init):
    bsz, s, w = q.shape
    return pl.pallas_call(
        functools.partial(_diff_kernel, bq=blk, bk=bk, lambda_init=lambda_init),
        grid=(bsz, w // LANES, s // blk),
        in_specs=[pl.BlockSpec((1, blk, LANES), lambda b, h, i: (b, i, h)),
                  pl.BlockSpec((1, s, LANES), lambda b, h, i: (b, 0, h)),
                  pl.BlockSpec((1, s, LANES), lambda b, h, i: (b, 0, h)),
                  pl.BlockSpec(lam_params.shape, lambda b, h, i: (0, 0)),
                  pl.BlockSpec((1, LANES), lambda b, h, i: (0, 0))],
        out_specs=pl.BlockSpec((1, blk, LANES), lambda b, h, i: (b, i, h)),
        out_shape=jax.ShapeDtypeStruct((bsz, s, w), BF16),
        name="diff_attention",
        compiler_params=_cparams(("arbitrary", "arbitrary", "arbitrary")),
    )(q, k, v, lam_params, subln_g.reshape(1, LANES))


def _pick(n, prefs):
    for p in prefs:
        if n % p == 0:
            return p
    return n


def kernel(x, c, positions, w_mod, b_mod, w_in_even, w_out_even, w_in_odd, lam_q1, lam_k1, lam_q2, lam_k2,
           subln_g, w_out_odd, ln_mix_g, ln_mix_b, w_gate, w_up, w_down, ln_ffn_g, ln_ffn_b):
    bsz, s, d = x.shape
    depth = w_mod.shape[0]
    alpha = (2 * depth) ** 0.25
    dff = w_gate.shape[2]
    rows = bsz * s
    tm = _pick(s, (512, 256, 128))
    tf = _pick(dff, (1408, 1024, 512, 256, 128))
    blk = _pick(s, (256, 128))
    bkw = _pick(s, (1024, 512, 256))
    scale = HEAD_DIM ** -0.5

    inv = ROPE_THETA ** (-jnp.arange(0, HEAD_DIM, 2, dtype=F32) / HEAD_DIM)
    ang = positions.astype(F32)[..., None] * inv
    cos, sin = jnp.cos(ang).reshape(rows, -1), jnp.sin(ang).reshape(rows, -1)
    cos128 = jnp.concatenate([cos] * 4, axis=1)
    sg128 = jnp.concatenate([-sin, sin, -sin, sin], axis=1)

    mod = _modulation(c, w_mod.astype(BF16), b_mod)
    xf = x.reshape(rows, d)

    for i in range(depth):
        mod_l = mod[i].reshape(bsz, 6, d)
        if i % 2 == 0:
            w = w_in_even[i // 2]
            n_sb = n_dsa = d // (2 * HEAD_DIM)
            sbw, dsw, ixw = n_sb * HEAD_DIM, n_dsa * HEAD_DIM, N_IDX_HEADS * HEAD_DIM
            offs = [0]
            for width in (sbw, sbw, sbw, dsw, HEAD_DIM, HEAD_DIM, ixw, HEAD_DIM, N_IDX_HEADS):
                offs.append(offs[-1] + width)
            col = lambda j: w[:, offs[j]:offs[j + 1]]
            pad = jnp.zeros((d, LANES - N_IDX_HEADS), w.dtype)
            wp = jnp.concatenate([col(0), col(1), col(2), col(5), col(5), col(8), pad], axis=1)
            wr = jnp.concatenate([col(3), col(6), col(4), col(4), col(7), col(7)], axis=1)
            plain_outs = ((0, sbw, scale), (sbw, sbw, 1.0), (2 * sbw, sbw, 1.0),
                          (3 * sbw, LANES, 1.0), (3 * sbw + LANES, LANES, N_IDX_HEADS ** -0.5))
            rope_outs = ((0, dsw, scale), (dsw, ixw, scale), (dsw + ixw, LANES, 1.0),
                         (dsw + ixw + LANES, LANES, 1.0))
            dts = (BF16, BF16, BF16, BF16, F32, BF16, BF16, BF16, BF16)
            q_sb, k_sb, v_sb, vd2, wx, qd, qx, kd2, kx2 = _inproj(
                xf, mod_l, cos128, sg128, wp.astype(BF16), wr.astype(BF16), _rot_partner(wr).astype(BF16),
                plain_outs, rope_outs, dts, s, tm)
            r3 = lambda t: t.reshape(bsz, s, t.shape[1])
            o_sb = _sb_attention(r3(q_sb), r3(k_sb), r3(v_sb), blk, bkw)
            o_dsa = _dsa_attention(r3(qd), r3(qx), r3(wx), r3(kd2), r3(kx2), r3(vd2), blk, bkw,
                                   min(DSA_TOPK_MAX, s // 4))
            w_out = w_out_even[i // 2].astype(BF16)
            o_list = [o_sb.reshape(rows, sbw), o_dsa.reshape(rows, dsw)]
            w_list = [w_out[:sbw], w_out[sbw:]]
        else:
            j = i // 2
            w = w_in_odd[j]
            dw = w.shape[1] // 3
            lambda_init = 0.8 - 0.6 * math.exp(-0.3 * i)
            wp = w[:, 2 * dw:]
            wr = w[:, :2 * dw]
            plain_outs = ((0, dw, 1.0),)
            rope_outs = ((0, dw, scale), (dw, dw, 1.0))
            v_df, q_df, k_df = _inproj(
                xf, mod_l, cos128, sg128, wp.astype(BF16), wr.astype(BF16), _rot_partner(wr).astype(BF16),
                plain_outs, rope_outs, (BF16, BF16, BF16), s, tm)
            r3 = lambda t: t.reshape(bsz, s, t.shape[1])
            lam_params = jnp.stack([lam_q1[j], lam_k1[j], lam_q2[j], lam_k2[j]]).astype(F32)
            o_df = _diff_attention(r3(q_df), r3(k_df), r3(v_df), lam_params, subln_g[j].astype(F32), blk, bkw,
                                   lambda_init)
            o_list = [o_df.reshape(rows, dw)]
            w_list = [w_out_odd[j].astype(BF16)]
        xf = _outproj_ln(o_list, w_list, xf, mod_l, ln_mix_g[i], ln_mix_b[i], alpha, s, tm)
        xf = _ffn_ln(xf, mod_l, w_gate[i].astype(BF16), w_up[i].astype(BF16), w_down[i].astype(BF16),
                     ln_ffn_g[i], ln_ffn_b[i], alpha, s, tm, tf)
    return xf.reshape(bsz, s, d)
```

```python
import functools
import math

import jax
import jax.numpy as jnp
from jax import lax
from jax.experimental import pallas as pl
from jax.experimental.pallas import tpu as pltpu

HEAD_DIM = 64
N_IDX_HEADS = 4
DSA_TOPK_MAX = 256
ROPE_THETA = 10000.0
LN_EPS = 1e-5
LANES = 128
NEG_BIG = -1e30
INT_MIN = -2 ** 31
ONES_ROWS = 16
COUNT_ROWS = 64

F32 = jnp.float32
BF16 = jnp.bfloat16
I32 = jnp.int32

VMEM_LIMIT = 56 * 1024 * 1024


def _cparams(sem):
    return pltpu.CompilerParams(dimension_semantics=sem, vmem_limit_bytes=VMEM_LIMIT)


def _nt(a, b):
    return lax.dot_general(a, b, (((1,), (1,)), ((), ())), preferred_element_type=F32)


def _nn(a, b):
    return jnp.dot(a, b, preferred_element_type=F32)


def _layer_norm(v, g, b):
    mu = jnp.mean(v, axis=-1, keepdims=True)
    d = v - mu
    var = jnp.mean(d * d, axis=-1, keepdims=True)
    return d * lax.rsqrt(var + LN_EPS) * g + b


def _mod_kernel(c_ref, w_ref, b_ref, o_ref):
    c = c_ref[...]
    a = (c * jax.nn.sigmoid(c)).astype(BF16)
    o_ref[0] = _nn(a, w_ref[0]) + b_ref[0]


def _modulation(c, w_mod_bf, b_mod):
    depth, d, n = w_mod_bf.shape
    bsz = c.shape[0]
    tn = n // 4
    return pl.pallas_call(
        _mod_kernel,
        grid=(depth, n // tn),
        in_specs=[pl.BlockSpec((bsz, d), lambda l, j: (0, 0)),
                  pl.BlockSpec((1, d, tn), lambda l, j: (l, 0, j)),
                  pl.BlockSpec((1, 1, tn), lambda l, j: (l, 0, j))],
        out_specs=pl.BlockSpec((1, bsz, tn), lambda l, j: (l, 0, j)),
        out_shape=jax.ShapeDtypeStruct((depth, bsz, n), F32),
        name="modulation",
        compiler_params=_cparams(("arbitrary", "arbitrary")),
    )(c, w_mod_bf, b_mod.reshape(depth, 1, n))


def _inproj_kernel(x_ref, mod_ref, cos_ref, sg_ref, wp_ref, wr_ref, wrr_ref, *out_refs,
                   plain_outs, rope_outs):
    x = x_ref[...]
    sh = mod_ref[0, 0:1, :]
    sc = mod_ref[0, 1:2, :]
    h = (x * (1.0 + sc) + sh).astype(BF16)
    plain = _nn(h, wp_ref[...])
    rope = _nn(h, wr_ref[...])
    rot = _nn(h, wrr_ref[...])
    cos = cos_ref[...]
    sg = sg_ref[...]
    n = 0
    for (c0, width, scale) in plain_outs:
        o = out_refs[n]
        v = plain[:, c0:c0 + width]
        if scale != 1.0:
            v = v * scale
        if o.dtype == F32:
            v = v.astype(BF16).astype(F32)
        o[...] = v.astype(o.dtype)
        n += 1
    for (c0, width, scale) in rope_outs:
        o = out_refs[n]
        for j in range(width // LANES):
            sl = slice(c0 + j * LANES, c0 + (j + 1) * LANES)
            v = rope[:, sl] * cos + rot[:, sl] * sg
            if scale != 1.0:
                v = v * scale
            o[:, j * LANES:(j + 1) * LANES] = v.astype(o.dtype)
        n += 1


def _inproj(xf, mod_l, cos128, sg128, wp, wr, wrr, plain_outs, rope_outs, out_dtypes, rows_per_batch, tm):
    r, d = xf.shape
    tiles_per_batch = rows_per_batch // tm
    widths = [w for (_, w, _) in plain_outs] + [w for (_, w, _) in rope_outs]
    row = lambda i: (i, 0)
    const = lambda i: (0, 0)
    return pl.pallas_call(
        functools.partial(_inproj_kernel, plain_outs=plain_outs, rope_outs=rope_outs),
        grid=(r // tm,),
        in_specs=[pl.BlockSpec((tm, d), row),
                  pl.BlockSpec((1, 6, d), lambda i: (i // tiles_per_batch, 0, 0)),
                  pl.BlockSpec((tm, LANES), row),
                  pl.BlockSpec((tm, LANES), row),
                  pl.BlockSpec(wp.shape, const),
                  pl.BlockSpec(wr.shape, const),
                  pl.BlockSpec(wrr.shape, const)],
        out_specs=[pl.BlockSpec((tm, w), row) for w in widths],
        out_shape=[jax.ShapeDtypeStruct((r, w), dt) for w, dt in zip(widths, out_dtypes)],
        name="inproj",
        compiler_params=_cparams(("arbitrary",)),
    )(xf, mod_l, cos128, sg128, wp, wr, wrr)


def _rot_partner(w):
    d, n = w.shape
    return w.reshape(d, n // HEAD_DIM, 2, HEAD_DIM // 2)[:, :, ::-1, :].reshape(d, n)


def _outproj_kernel(*refs, n_in, alpha):
    o_refs = refs[:n_in]
    w_refs = refs[n_in:2 * n_in]
    x_ref, mod_ref, g_ref, b_ref, out_ref = refs[2 * n_in:]
    y = _nn(o_refs[0][...], w_refs[0][...])
    for a, w in zip(o_refs[1:], w_refs[1:]):
        y = y + _nn(a[...], w[...])
    gate = mod_ref[0, 2:3, :]
    v = alpha * x_ref[...] + (1.0 + gate) * y
    out_ref[...] = _layer_norm(v, g_ref[...], b_ref[...])


def _outproj_ln(o_list, w_list, xf, mod_l, g, b, alpha, rows_per_batch, tm):
    r, d = xf.shape
    tiles_per_batch = rows_per_batch // tm
    row = lambda i: (i, 0)
    const = lambda i: (0, 0)
    n_in = len(o_list)
    return pl.pallas_call(
        functools.partial(_outproj_kernel, n_in=n_in, alpha=alpha),
        grid=(r // tm,),
        in_specs=([pl.BlockSpec((tm, o.shape[1]), row) for o in o_list]
                  + [pl.BlockSpec(w.shape, const) for w in w_list]
                  + [pl.BlockSpec((tm, d), row),
                     pl.BlockSpec((1, 6, d), lambda i: (i // tiles_per_batch, 0, 0)),
                     pl.BlockSpec((1, d), const),
                     pl.BlockSpec((1, d), const)]),
        out_specs=pl.BlockSpec((tm, d), row),
        out_shape=jax.ShapeDtypeStruct((r, d), F32),
        name="outproj_ln",
        compiler_params=_cparams(("arbitrary",)),
    )(*o_list, *w_list, xf, mod_l, g.reshape(1, d), b.reshape(1, d))


def _ffn_kernel(x_ref, mod_ref, wg_ref, wu_ref, wd_ref, g_ref, b_ref, out_ref, h_ref, acc_ref, *, alpha):
    f = pl.program_id(1)

    @pl.when(f == 0)
    def _():
        sh = mod_ref[0, 3:4, :]
        sc = mod_ref[0, 4:5, :]
        h_ref[...] = (x_ref[...] * (1.0 + sc) + sh).astype(BF16)
        acc_ref[...] = jnp.zeros_like(acc_ref)

    h = h_ref[...]
    gate = _nn(h, wg_ref[...])
    up = _nn(h, wu_ref[...])
    a = (gate * jax.nn.sigmoid(gate) * up).astype(BF16)
    acc_ref[...] += _nn(a, wd_ref[...])

    @pl.when(f == pl.num_programs(1) - 1)
    def _():
        gf = mod_ref[0, 5:6, :]
        v = alpha * x_ref[...] + (1.0 + gf) * acc_ref[...]
        out_ref[...] = _layer_norm(v, g_ref[...], b_ref[...])


def _ffn_ln(xf, mod_l, wg, wu, wd, g, b, alpha, rows_per_batch, tm, tf):
    r, d = xf.shape
    dff = wg.shape[1]
    tiles_per_batch = rows_per_batch // tm
    return pl.pallas_call(
        functools.partial(_ffn_kernel, alpha=alpha),
        grid=(r // tm, dff // tf),
        in_specs=[pl.BlockSpec((tm, d), lambda i, f: (i, 0)),
                  pl.BlockSpec((1, 6, d), lambda i, f: (i // tiles_per_batch, 0, 0)),
                  pl.BlockSpec((d, tf), lambda i, f: (0, f)),
                  pl.BlockSpec((d, tf), lambda i, f: (0, f)),
                  pl.BlockSpec((tf, d), lambda i, f: (f, 0)),
                  pl.BlockSpec((1, d), lambda i, f: (0, 0)),
                  pl.BlockSpec((1, d), lambda i, f: (0, 0))],
        out_specs=pl.BlockSpec((tm, d), lambda i, f: (i, 0)),
        out_shape=jax.ShapeDtypeStruct((r, d), F32),
        scratch_shapes=[pltpu.VMEM((tm, d), BF16), pltpu.VMEM((tm, d), F32)],
        name="ffn_ln",
        compiler_params=_cparams(("arbitrary", "arbitrary")),
    )(xf, mod_l, wg, wu, wd, g.reshape(1, d), b.reshape(1, d))


def _lane_half_masks():
    lane = lax.broadcasted_iota(I32, (1, LANES), 1)
    return lane < HEAD_DIM, lane >= HEAD_DIM


def _sb_kernel(q_ref, k_ref, v_ref, o_ref, *, blk, bk, sub):
    qi = pl.program_id(2)
    q = q_ref[0]
    lo_half, hi_half = _lane_half_masks()
    zero = jnp.zeros_like(q)
    qs = (jnp.where(lo_half, q, zero), jnp.where(hi_half, q, zero))
    rs = lax.broadcasted_iota(I32, (sub, sub), 0)
    cs = lax.broadcasted_iota(I32, (sub, sub), 1)
    later = (rs > cs).astype(BF16)
    later2 = jnp.concatenate([later, later], axis=0)
    n_full = (qi * blk) // bk

    def step(kb, carry, diag):
        off = pl.multiple_of(kb * bk, bk)
        k = k_ref[0, pl.ds(off, bk), :]
        v = v_ref[0, pl.ds(off, bk), :]
        if diag:
            rr = lax.broadcasted_iota(I32, (blk, bk), 0) + (qi * blk - kb * bk)
            cc = lax.broadcasted_iota(I32, (blk, bk), 1)
            past = cc < rr
        out = []
        for hd in range(2):
            c, acc = carry[hd]
            z = _nt(qs[hd], k)
            sp = jnp.maximum(z, 0.0) + jnp.log(1.0 + jnp.exp(-jnp.abs(z)))
            if diag:
                sp = jnp.where(past, sp, 0.0)
            sp_hi = sp.astype(BF16)
            sp_lo = (sp - sp_hi.astype(F32)).astype(BF16)
            ws = [None] * (bk // sub)
            for j in reversed(range(bk // sub)):
                sl = slice(j * sub, (j + 1) * sub)
                tail = _nn(jnp.concatenate([sp_hi[:, sl], sp_lo[:, sl]], axis=1), later2)
                ws[j] = jnp.exp(z[:, sl] - sp[:, sl] - tail - c)
                c = c + tail[:, 0:1] + sp[:, j * sub:j * sub + 1]
            w = jnp.concatenate(ws, axis=1)
            if diag:
                w = jnp.where(past, w, 0.0)
            acc = acc + _nn(w.astype(BF16), v)
            out.append((c, acc))
        return tuple(out)

    init = tuple((jnp.zeros((blk, 1), F32), jnp.zeros((blk, LANES), F32)) for _ in range(2))
    carry = step(n_full, init, True)
    carry = lax.fori_loop(0, n_full, lambda i, cr: step(n_full - 1 - i, cr, False), carry)
    o_ref[0] = jnp.where(lo_half, carry[0][1], carry[1][1]).astype(o_ref.dtype)


def _sb_attention(q, k, v, blk, bk):
    bsz, s, w = q.shape
    return pl.pallas_call(
        functools.partial(_sb_kernel, blk=blk, bk=bk, sub=min(bk, 256)),
        grid=(bsz, w // LANES, s // blk),
        in_specs=[pl.BlockSpec((1, blk, LANES), lambda b, h, i: (b, i, h)),
                  pl.BlockSpec((1, s, LANES), lambda b, h, i: (b, 0, h)),
                  pl.BlockSpec((1, s, LANES), lambda b, h, i: (b, 0, h))],
        out_specs=pl.BlockSpec((1, blk, LANES), lambda b, h, i: (b, i, h)),
        out_shape=jax.ShapeDtypeStruct((bsz, s, w), BF16),
        name="sb_attention",
        compiler_params=_cparams(("arbitrary", "arbitrary", "arbitrary")),
    )(q, k, v)


def _fold_rows(x, target):
    while x.shape[0] > target:
        h = x.shape[0] // 2
        x = x[:h] + x[h:]
    return x


def _flash_update_t(s, m, acc, vt):
    m_new = jnp.maximum(m, jnp.max(s, axis=0, keepdims=True))
    p = jnp.exp(s - m_new)
    acc = jnp.exp(m - m_new) * acc + _nn(vt, p.astype(BF16))
    return m_new, acc


def _flash_init(v_rows, r):
    return (jnp.full((1, r), NEG_BIG, F32), jnp.zeros((v_rows, r), F32))


def _flash_pipelined_t(qq, k_ref, vt_block, s_ref, n_blocks, bk, loop_bias, tail_bias):
    last = n_blocks - 1
    n_pairs = last // 2

    def qk(kb, slot):
        off = pl.multiple_of(kb * bk, bk)
        s_ref[slot] = _nt(k_ref[0, pl.ds(off, bk), :], qq)

    def consume(kb, slot, carry, bias_fn):
        s = s_ref[slot]
        if bias_fn is not None:
            s = s + bias_fn(kb)
        return _flash_update_t(s, *carry, vt_block(kb))

    def pair(p, carry):
        qk(2 * p + 1, 1)
        carry = consume(2 * p, 0, carry, loop_bias)
        qk(2 * p + 2, 0)
        return consume(2 * p + 1, 1, carry, loop_bias)

    qk(0, 0)
    carry = lax.fori_loop(0, n_pairs, pair, _flash_init(vt_block(0).shape[0], qq.shape[0]))
    kb = 2 * n_pairs
    qk(jnp.minimum(kb + 1, last), 1)
    carry = consume(kb, 0, carry, tail_bias)
    return lax.cond(kb < last, lambda cr: consume(kb + 1, 1, cr, tail_bias), lambda cr: cr, carry)


def _dsa_kernel(qd_ref, qx_ref, wt_ref, kd_ref, kx_ref, vt_ref, o_ref, key_ref, s_ref, *, bq, bk, sub, topk):
    qb = pl.program_id(1)
    nkb = (qb * bq) // bk + 1
    lo_half, hi_half = _lane_half_masks()
    halves = (lo_half, hi_half)
    n_heads = qd_ref.shape[2] // HEAD_DIM

    qx = qx_ref[0]
    wt = wt_ref[0]
    qx_heads = []
    for hx in range(N_IDX_HEADS):
        chunk = qx[:, (hx // 2) * LANES:(hx // 2 + 1) * LANES]
        qx_heads.append(jnp.where(halves[hx % 2], chunk, jnp.zeros_like(chunk)))

    def score_block(kb, _):
        off = pl.multiple_of(kb * bk, bk)
        kx = kx_ref[0, pl.ds(off, bk), :]
        score = jnp.zeros((bk, bq), F32)
        for hx in range(N_IDX_HEADS):
            logit = _nt(kx, qx_heads[hx])
            r = jnp.maximum(logit, 0.0).astype(BF16).astype(F32)
            score = score + r * wt[hx:hx + 1, :]
        bits = lax.bitcast_convert_type(score, I32)
        key = bits ^ (lax.shift_right_arithmetic(bits, 31) & 0x7FFFFFFF)
        key = jnp.where(key == -1, 0, key)
        kpos = lax.broadcasted_iota(I32, (bk, bq), 0) + (kb * bk - qb * bq)
        qpos = lax.broadcasted_iota(I32, (bk, bq), 1)
        key_ref[kb] = jnp.where(kpos <= qpos, key, INT_MIN)
        return 0

    lax.fori_loop(0, nkb, score_block, 0)

    q_t = qb * bq + lax.broadcasted_iota(I32, (1, bq), 1)
    k_eff = jnp.minimum(topk, q_t + 1).astype(F32)

    def count_ge(thr):
        def body(kb, acc):
            for c in range(bk // COUNT_ROWS):
                keys = key_ref[kb, c * COUNT_ROWS:(c + 1) * COUNT_ROWS, :]
                acc = acc + _fold_rows(jnp.where(keys >= thr, 1.0, 0.0), 8)
            return acc
        acc = lax.fori_loop(0, nkb, body, jnp.zeros((8, bq), F32))
        return jnp.sum(acc, axis=0, keepdims=True)

    def bit_step(i, prefix):
        cand = prefix | lax.shift_left(jnp.int32(1), 31 - i)
        cnt = count_ge(cand ^ INT_MIN)
        return jnp.where(cnt >= k_eff, cand, prefix)

    prefix = lax.fori_loop(0, 32, bit_step, jnp.zeros((1, bq), I32))
    thr = prefix ^ INT_MIN
    need = k_eff - count_ge(thr + 1)

    rs = lax.broadcasted_iota(I32, (sub, sub), 0)
    cs = lax.broadcasted_iota(I32, (sub, sub), 1)
    earlier = (cs < rs).astype(BF16)

    def mask_block(kb, seen):
        for j in range(bk // sub):
            key = key_ref[kb, j * sub:(j + 1) * sub, :]
            eq = jnp.where(key == thr, 1.0, 0.0)
            rank = _nn(earlier, eq.astype(BF16)) + seen
            take = jnp.where(rank < need, 1, 0)
            sel = (key + take) > thr
            key_ref[kb, j * sub:(j + 1) * sub, :] = lax.bitcast_convert_type(jnp.where(sel, 0.0, NEG_BIG), I32)
            seen = seen + jnp.sum(eq, axis=0, keepdims=True)
        return seen

    lax.fori_loop(0, nkb, mask_block, jnp.zeros((1, bq), F32))

    qd = qd_ref[0]

    def pair_bias(kb):
        bias = lax.bitcast_convert_type(key_ref[kb], F32)
        return jnp.concatenate([bias, bias], axis=1)

    for hp in range(n_heads // 2):
        chunk = qd[:, hp * LANES:(hp + 1) * LANES]
        zero = jnp.zeros_like(chunk)
        qq = jnp.concatenate([jnp.where(lo_half, chunk, zero), jnp.where(hi_half, chunk, zero)], axis=0)

        _, acc = _flash_pipelined_t(qq, kd_ref, lambda kb: vt_ref[0, kb], s_ref, nkb, bk, pair_bias, pair_bias)
        o = acc[:HEAD_DIM] / acc[HEAD_DIM:HEAD_DIM + 1]
        pair = jnp.concatenate([o[:, :bq], o[:, bq:]], axis=0)
        o_ref[0, :, hp * LANES:(hp + 1) * LANES] = pair.T.astype(o_ref.dtype)


def _dsa_attention(qd, qx, wt, kd2, kx2, vt2, blk, bk, topk):
    bsz, s, wq = qd.shape
    kspec = pl.BlockSpec((1, s, LANES), lambda b, i: (b, 0, 0))
    return pl.pallas_call(
        functools.partial(_dsa_kernel, bq=blk, bk=bk, sub=min(bk, 256), topk=topk),
        grid=(bsz, s // blk),
        in_specs=[pl.BlockSpec((1, blk, wq), lambda b, i: (b, i, 0)),
                  pl.BlockSpec((1, blk, qx.shape[2]), lambda b, i: (b, i, 0)),
                  pl.BlockSpec((1, wt.shape[1], blk), lambda b, i: (b, 0, i)),
                  kspec, kspec,
                  pl.BlockSpec((1, s // bk, LANES, bk), lambda b, i: (b, 0, 0, 0))],
        out_specs=pl.BlockSpec((1, blk, wq), lambda b, i: (b, i, 0)),
        out_shape=jax.ShapeDtypeStruct((bsz, s, wq), BF16),
        scratch_shapes=[pltpu.VMEM((s // bk, bk, blk), I32), pltpu.VMEM((2, bk, 2 * blk), F32)],
        name="dsa_attention",
        compiler_params=_cparams(("arbitrary", "arbitrary")),
    )(qd, qx, wt, kd2, kx2, vt2)


def _diff_kernel(q_ref, k_ref, vt_ref, lam_ref, g_ref, o_ref, s_ref, *, bq, bk, lambda_init):
    qi = pl.program_id(2)
    q = q_ref[0]
    lo_half, hi_half = _lane_half_masks()
    zero = jnp.zeros_like(q)
    qq = jnp.concatenate([jnp.where(lo_half, q, zero), jnp.where(hi_half, q, zero)], axis=0)
    n_blocks = (qi * bq) // bk + 1

    def causal_bias(kb):
        kpos = lax.broadcasted_iota(I32, (bk, bq), 0) + (kb * bk - qi * bq)
        qpos = lax.broadcasted_iota(I32, (bk, bq), 1)
        bias = jnp.where(kpos <= qpos, 0.0, NEG_BIG)
        return jnp.concatenate([bias, bias], axis=1)

    _, acc = _flash_pipelined_t(qq, k_ref, lambda kb: vt_ref[0, 0, kb], s_ref, n_blocks, bk, None, causal_bias)
    l = acc[LANES:LANES + 1]
    acc = acc[:LANES]

    lp = lam_ref[...]
    lam = (jnp.exp(jnp.sum(lp[0:1] * lp[1:2], axis=1, keepdims=True))
           - jnp.exp(jnp.sum(lp[2:3] * lp[3:4], axis=1, keepdims=True)) + lambda_init)
    o = acc[:, :bq] / l[:, :bq] - lam * (acc[:, bq:] / l[:, bq:])
    ms = jnp.mean(o * o, axis=0, keepdims=True)
    o = o * lax.rsqrt(ms + LN_EPS) * g_ref[...] * (1.0 - lambda_init)
    o_ref[0] = o.T.astype(o_ref.dtype)


def _diff_attention(q, k, vt, lam_params, subln_g, blk, bk, lambda_init):
    bsz, s, w = q.shape
    return pl.pallas_call(
        functools.partial(_diff_kernel, bq=blk, bk=bk, lambda_init=lambda_init),
        grid=(bsz, w // LANES, s // blk),
        in_specs=[pl.BlockSpec((1, blk, LANES), lambda b, h, i: (b, i, h)),
                  pl.BlockSpec((1, s, LANES), lambda b, h, i: (b, 0, h)),
                  pl.BlockSpec((1, 1) + vt.shape[2:], lambda b, h, i: (b, h, 0, 0, 0)),
                  pl.BlockSpec(lam_params.shape, lambda b, h, i: (0, 0)),
                  pl.BlockSpec((LANES, 1), lambda b, h, i: (0, 0))],
        out_specs=pl.BlockSpec((1, blk, LANES), lambda b, h, i: (b, i, h)),
        out_shape=jax.ShapeDtypeStruct((bsz, s, w), BF16),
        scratch_shapes=[pltpu.VMEM((2, bk, 2 * blk), F32)],
        name="diff_attention",
        compiler_params=_cparams(("arbitrary", "arbitrary", "arbitrary")),
    )(q, k, vt, lam_params, subln_g.reshape(LANES, 1))


def _pick(n, prefs):
    for p in prefs:
        if n % p == 0:
            return p
    return n


def kernel(x, c, positions, w_mod, b_mod, w_in_even, w_out_even, w_in_odd, lam_q1, lam_k1, lam_q2, lam_k2,
           subln_g, w_out_odd, ln_mix_g, ln_mix_b, w_gate, w_up, w_down, ln_ffn_g, ln_ffn_b):
    bsz, s, d = x.shape
    depth = w_mod.shape[0]
    alpha = (2 * depth) ** 0.25
    dff = w_gate.shape[2]
    rows = bsz * s
    tm = _pick(s, (512, 256, 128))
    tf = _pick(dff, (1408, 1024, 512, 256, 128))
    blk = _pick(s, (256, 128))
    bkw = _pick(s, (1024, 512, 256))
    scale = HEAD_DIM ** -0.5

    inv = ROPE_THETA ** (-jnp.arange(0, HEAD_DIM, 2, dtype=F32) / HEAD_DIM)
    ang = positions.astype(F32)[..., None] * inv
    cos, sin = jnp.cos(ang).reshape(rows, -1), jnp.sin(ang).reshape(rows, -1)
    cos128 = jnp.concatenate([cos] * 4, axis=1)
    sg128 = jnp.concatenate([-sin, sin, -sin, sin], axis=1)

    mod = _modulation(c, w_mod.astype(BF16), b_mod)
    xf = x.reshape(rows, d)

    for i in range(depth):
        mod_l = mod[i].reshape(bsz, 6, d)
        if i % 2 == 0:
            w = w_in_even[i // 2]
            n_sb = n_dsa = d // (2 * HEAD_DIM)
            sbw, dsw, ixw = n_sb * HEAD_DIM, n_dsa * HEAD_DIM, N_IDX_HEADS * HEAD_DIM
            offs = [0]
            for width in (sbw, sbw, sbw, dsw, HEAD_DIM, HEAD_DIM, ixw, HEAD_DIM, N_IDX_HEADS):
                offs.append(offs[-1] + width)
            col = lambda j: w[:, offs[j]:offs[j + 1]]
            pad = jnp.zeros((d, LANES - N_IDX_HEADS), w.dtype)
            wp = jnp.concatenate([col(0), col(1), col(2), col(5), col(5), col(8), pad], axis=1)
            wr = jnp.concatenate([col(3), col(6), col(4), col(4), col(7), col(7)], axis=1)
            plain_outs = ((0, sbw, scale), (sbw, sbw, 1.0), (2 * sbw, sbw, 1.0),
                          (3 * sbw, LANES, 1.0), (3 * sbw + LANES, LANES, N_IDX_HEADS ** -0.5))
            rope_outs = ((0, dsw, scale), (dsw, ixw, scale), (dsw + ixw, LANES, 1.0),
                         (dsw + ixw + LANES, LANES, 1.0))
            dts = (BF16, BF16, BF16, BF16, F32, BF16, BF16, BF16, BF16)
            q_sb, k_sb, v_sb, vd2, wx, qd, qx, kd2, kx2 = _inproj(
                xf, mod_l, cos128, sg128, wp.astype(BF16), wr.astype(BF16), _rot_partner(wr).astype(BF16),
                plain_outs, rope_outs, dts, s, tm)
            r3 = lambda t: t.reshape(bsz, s, t.shape[1])
            o_sb = _sb_attention(r3(q_sb), r3(k_sb), r3(v_sb), blk, bkw)
            wt = r3(wx)[:, :, :8].swapaxes(1, 2)
            vt2 = vd2.reshape(bsz, s // bkw, bkw, LANES).swapaxes(2, 3)
            vt2 = jnp.concatenate([vt2[:, :, :HEAD_DIM], jnp.ones_like(vt2[:, :, HEAD_DIM:])], axis=2)
            o_dsa = _dsa_attention(r3(qd), r3(qx), wt, r3(kd2), r3(kx2), vt2, blk, bkw,
                                   min(DSA_TOPK_MAX, s // 4))
            w_out = w_out_even[i // 2].astype(BF16)
            o_list = [o_sb.reshape(rows, sbw), o_dsa.reshape(rows, dsw)]
            w_list = [w_out[:sbw], w_out[sbw:]]
        else:
            j = i // 2
            w = w_in_odd[j]
            dw = w.shape[1] // 3
            lambda_init = 0.8 - 0.6 * math.exp(-0.3 * i)
            wp = w[:, 2 * dw:]
            wr = w[:, :2 * dw]
            plain_outs = ((0, dw, 1.0),)
            rope_outs = ((0, dw, scale), (dw, dw, 1.0))
            v_df, q_df, k_df = _inproj(
                xf, mod_l, cos128, sg128, wp.astype(BF16), wr.astype(BF16), _rot_partner(wr).astype(BF16),
                plain_outs, rope_outs, (BF16, BF16, BF16), s, tm)
            r3 = lambda t: t.reshape(bsz, s, t.shape[1])
            lam_params = jnp.stack([lam_q1[j], lam_k1[j], lam_q2[j], lam_k2[j]]).astype(F32)
            vt = v_df.reshape(bsz, s // bkw, bkw, dw // LANES, LANES).transpose(0, 3, 1, 4, 2)
            vt = jnp.concatenate([vt, jnp.ones_like(vt[:, :, :, :ONES_ROWS])], axis=3)
            o_df = _diff_attention(r3(q_df), r3(k_df), vt, lam_params, subln_g[j].astype(F32), blk, bkw,
                                   lambda_init)
            o_list = [o_df.reshape(rows, dw)]
            w_list = [w_out_odd[j].astype(BF16)]
        xf = _outproj_ln(o_list, w_list, xf, mod_l, ln_mix_g[i], ln_mix_b[i], alpha, s, tm)
        xf = _ffn_ln(xf, mod_l, w_gate[i].astype(BF16), w_up[i].astype(BF16), w_down[i].astype(BF16),
                     ln_ffn_g[i], ln_ffn_b[i], alpha, s, tm, tf)
    return xf.reshape(bsz, s, d)
```

```python
import functools
import math

import jax
import jax.numpy as jnp
from jax import lax
from jax.experimental import pallas as pl
from jax.experimental.pallas import tpu as pltpu

HEAD_DIM = 64
N_IDX_HEADS = 4
DSA_TOPK_MAX = 256
ROPE_THETA = 10000.0
LN_EPS = 1e-5
LANES = 128
NEG_BIG = -1e30
INT_MIN = -2 ** 31
LOG2E = 1.4426950408889634
ONES_ROWS = 16
COUNT_ROWS = 64

F32 = jnp.float32
BF16 = jnp.bfloat16
I32 = jnp.int32

VMEM_LIMIT = 56 * 1024 * 1024


def _cparams(sem):
    return pltpu.CompilerParams(dimension_semantics=sem, vmem_limit_bytes=VMEM_LIMIT)


def _nt(a, b):
    return lax.dot_general(a, b, (((1,), (1,)), ((), ())), preferred_element_type=F32)


def _nn(a, b):
    return jnp.dot(a, b, preferred_element_type=F32)


def _layer_norm(v, g, b):
    mu = jnp.mean(v, axis=-1, keepdims=True)
    d = v - mu
    var = jnp.mean(d * d, axis=-1, keepdims=True)
    return d * lax.rsqrt(var + LN_EPS) * g + b


def _mod_kernel(c_ref, w_ref, b_ref, o_ref):
    c = c_ref[...]
    a = (c * jax.nn.sigmoid(c)).astype(BF16)
    o_ref[0] = _nn(a, w_ref[0]) + b_ref[0]


def _modulation(c, w_mod_bf, b_mod):
    depth, d, n = w_mod_bf.shape
    bsz = c.shape[0]
    tn = n // 4
    return pl.pallas_call(
        _mod_kernel,
        grid=(depth, n // tn),
        in_specs=[pl.BlockSpec((bsz, d), lambda l, j: (0, 0)),
                  pl.BlockSpec((1, d, tn), lambda l, j: (l, 0, j)),
                  pl.BlockSpec((1, 1, tn), lambda l, j: (l, 0, j))],
        out_specs=pl.BlockSpec((1, bsz, tn), lambda l, j: (l, 0, j)),
        out_shape=jax.ShapeDtypeStruct((depth, bsz, n), F32),
        name="modulation",
        compiler_params=_cparams(("arbitrary", "arbitrary")),
    )(c, w_mod_bf, b_mod.reshape(depth, 1, n))


def _inproj_kernel(x_ref, mod_ref, cos_ref, sg_ref, wp_ref, wr_ref, wrr_ref, *out_refs,
                   plain_outs, rope_outs):
    x = x_ref[...]
    sh = mod_ref[0, 0:1, :]
    sc = mod_ref[0, 1:2, :]
    h = (x * (1.0 + sc) + sh).astype(BF16)
    plain = _nn(h, wp_ref[...])
    rope = _nn(h, wr_ref[...])
    rot = _nn(h, wrr_ref[...])
    cos = cos_ref[...]
    sg = sg_ref[...]
    n = 0
    for (c0, width, scale) in plain_outs:
        o = out_refs[n]
        v = plain[:, c0:c0 + width]
        if scale != 1.0:
            v = v * scale
        if o.dtype == F32:
            v = v.astype(BF16).astype(F32)
        o[...] = v.astype(o.dtype)
        n += 1
    for (c0, width, scale) in rope_outs:
        o = out_refs[n]
        for j in range(width // LANES):
            sl = slice(c0 + j * LANES, c0 + (j + 1) * LANES)
            v = rope[:, sl] * cos + rot[:, sl] * sg
            if scale != 1.0:
                v = v * scale
            o[:, j * LANES:(j + 1) * LANES] = v.astype(o.dtype)
        n += 1


def _inproj(xf, mod_l, cos128, sg128, wp, wr, wrr, plain_outs, rope_outs, out_dtypes, rows_per_batch, tm):
    r, d = xf.shape
    tiles_per_batch = rows_per_batch // tm
    widths = [w for (_, w, _) in plain_outs] + [w for (_, w, _) in rope_outs]
    row = lambda i: (i, 0)
    const = lambda i: (0, 0)
    return pl.pallas_call(
        functools.partial(_inproj_kernel, plain_outs=plain_outs, rope_outs=rope_outs),
        grid=(r // tm,),
        in_specs=[pl.BlockSpec((tm, d), row),
                  pl.BlockSpec((1, 6, d), lambda i: (i // tiles_per_batch, 0, 0)),
                  pl.BlockSpec((tm, LANES), row),
                  pl.BlockSpec((tm, LANES), row),
                  pl.BlockSpec(wp.shape, const),
                  pl.BlockSpec(wr.shape, const),
                  pl.BlockSpec(wrr.shape, const)],
        out_specs=[pl.BlockSpec((tm, w), row) for w in widths],
        out_shape=[jax.ShapeDtypeStruct((r, w), dt) for w, dt in zip(widths, out_dtypes)],
        name="inproj",
        compiler_params=_cparams(("arbitrary",)),
    )(xf, mod_l, cos128, sg128, wp, wr, wrr)


def _rot_partner(w):
    d, n = w.shape
    return w.reshape(d, n // HEAD_DIM, 2, HEAD_DIM // 2)[:, :, ::-1, :].reshape(d, n)


def _outproj_kernel(*refs, n_in, alpha):
    o_refs = refs[:n_in]
    w_refs = refs[n_in:2 * n_in]
    x_ref, mod_ref, g_ref, b_ref, out_ref = refs[2 * n_in:]
    y = _nn(o_refs[0][...], w_refs[0][...])
    for a, w in zip(o_refs[1:], w_refs[1:]):
        y = y + _nn(a[...], w[...])
    gate = mod_ref[0, 2:3, :]
    v = alpha * x_ref[...] + (1.0 + gate) * y
    out_ref[...] = _layer_norm(v, g_ref[...], b_ref[...])


def _outproj_ln(o_list, w_list, xf, mod_l, g, b, alpha, rows_per_batch, tm):
    r, d = xf.shape
    tiles_per_batch = rows_per_batch // tm
    row = lambda i: (i, 0)
    const = lambda i: (0, 0)
    n_in = len(o_list)
    return pl.pallas_call(
        functools.partial(_outproj_kernel, n_in=n_in, alpha=alpha),
        grid=(r // tm,),
        in_specs=([pl.BlockSpec((tm, o.shape[1]), row) for o in o_list]
                  + [pl.BlockSpec(w.shape, const) for w in w_list]
                  + [pl.BlockSpec((tm, d), row),
                     pl.BlockSpec((1, 6, d), lambda i: (i // tiles_per_batch, 0, 0)),
                     pl.BlockSpec((1, d), const),
                     pl.BlockSpec((1, d), const)]),
        out_specs=pl.BlockSpec((tm, d), row),
        out_shape=jax.ShapeDtypeStruct((r, d), F32),
        name="outproj_ln",
        compiler_params=_cparams(("arbitrary",)),
    )(*o_list, *w_list, xf, mod_l, g.reshape(1, d), b.reshape(1, d))


def _ffn_kernel(x_ref, mod_ref, wg_ref, wu_ref, wd_ref, g_ref, b_ref, out_ref, h_ref, acc_ref, *, alpha):
    f = pl.program_id(1)

    @pl.when(f == 0)
    def _():
        sh = mod_ref[0, 3:4, :]
        sc = mod_ref[0, 4:5, :]
        h_ref[...] = (x_ref[...] * (1.0 + sc) + sh).astype(BF16)
        acc_ref[...] = jnp.zeros_like(acc_ref)

    h = h_ref[...]
    gate = _nn(h, wg_ref[...])
    up = _nn(h, wu_ref[...])
    a = (gate * jax.nn.sigmoid(gate) * up).astype(BF16)
    acc_ref[...] += _nn(a, wd_ref[...])

    @pl.when(f == pl.num_programs(1) - 1)
    def _():
        gf = mod_ref[0, 5:6, :]
        v = alpha * x_ref[...] + (1.0 + gf) * acc_ref[...]
        out_ref[...] = _layer_norm(v, g_ref[...], b_ref[...])


def _ffn_ln(xf, mod_l, wg, wu, wd, g, b, alpha, rows_per_batch, tm, tf):
    r, d = xf.shape
    dff = wg.shape[1]
    tiles_per_batch = rows_per_batch // tm
    return pl.pallas_call(
        functools.partial(_ffn_kernel, alpha=alpha),
        grid=(r // tm, dff // tf),
        in_specs=[pl.BlockSpec((tm, d), lambda i, f: (i, 0)),
                  pl.BlockSpec((1, 6, d), lambda i, f: (i // tiles_per_batch, 0, 0)),
                  pl.BlockSpec((d, tf), lambda i, f: (0, f)),
                  pl.BlockSpec((d, tf), lambda i, f: (0, f)),
                  pl.BlockSpec((tf, d), lambda i, f: (f, 0)),
                  pl.BlockSpec((1, d), lambda i, f: (0, 0)),
                  pl.BlockSpec((1, d), lambda i, f: (0, 0))],
        out_specs=pl.BlockSpec((tm, d), lambda i, f: (i, 0)),
        out_shape=jax.ShapeDtypeStruct((r, d), F32),
        scratch_shapes=[pltpu.VMEM((tm, d), BF16), pltpu.VMEM((tm, d), F32)],
        name="ffn_ln",
        compiler_params=_cparams(("arbitrary", "arbitrary")),
    )(xf, mod_l, wg, wu, wd, g.reshape(1, d), b.reshape(1, d))


def _lane_half_masks():
    lane = lax.broadcasted_iota(I32, (1, LANES), 1)
    return lane < HEAD_DIM, lane >= HEAD_DIM


def _sb_kernel(q_ref, k_ref, vt_ref, o_ref, z_ref, *, bq, bk, sub):
    qi = pl.program_id(2)
    q = q_ref[0]
    lo_half, hi_half = _lane_half_masks()
    zero = jnp.zeros_like(q)
    qq = jnp.concatenate([jnp.where(lo_half, q, zero), jnp.where(hi_half, q, zero)], axis=0)
    rs = lax.broadcasted_iota(I32, (sub, sub), 0)
    cs = lax.broadcasted_iota(I32, (sub, sub), 1)
    later = (cs > rs).astype(BF16)
    n_full = (qi * bq) // bk

    def qk(kb, slot):
        off = pl.multiple_of(kb * bk, bk)
        z_ref[slot] = _nt(k_ref[0, pl.ds(off, bk), :], qq)

    def consume(kb, slot, carry, diag):
        c, acc = carry
        z = z_ref[slot]
        sp = jnp.maximum(z, 0.0) + jnp.log(1.0 + jnp.exp2(jnp.abs(z) * -LOG2E))
        if diag:
            kpos = lax.broadcasted_iota(I32, (bk, bq), 0) + (kb * bk - qi * bq)
            qpos = lax.broadcasted_iota(I32, (bk, bq), 1)
            past = kpos < qpos
            past = jnp.concatenate([past, past], axis=1)
            sp = jnp.where(past, sp, 0.0)
        u = z - sp
        sp_b = sp.astype(BF16)
        ws = [None] * (bk // sub)
        for j in reversed(range(bk // sub)):
            tail = _nn(later, sp_b[j * sub:(j + 1) * sub])
            ws[j] = jnp.exp(u[j * sub:(j + 1) * sub] - tail - c)
            c = c + tail[0:1] + sp[j * sub:j * sub + 1]
        w = jnp.concatenate(ws, axis=0)
        if diag:
            w = jnp.where(past, w, 0.0)
        return c, acc + _nn(vt_ref[0, 0, kb], w.astype(BF16))

    def pair(p, carry):
        kb = n_full - 1 - 2 * p
        qk(kb - 1, 0)
        carry = consume(kb, 1, carry, False)
        qk(jnp.maximum(kb - 2, 0), 1)
        return consume(kb - 1, 0, carry, False)

    carry = (jnp.zeros((1, 2 * bq), F32), jnp.zeros((LANES, 2 * bq), F32))
    qk(n_full, 0)
    qk(jnp.maximum(n_full - 1, 0), 1)
    carry = consume(n_full, 0, carry, True)
    carry = lax.fori_loop(0, n_full // 2, pair, carry)
    _, acc = lax.cond(n_full % 2 == 1, lambda cr: consume(0, 1, cr, False), lambda cr: cr, carry)
    out_t = jnp.concatenate([acc[:HEAD_DIM, :bq], acc[HEAD_DIM:, bq:]], axis=0)
    o_ref[0] = out_t.T.astype(o_ref.dtype)


def _sb_attention(q, k, vt, blk, bk):
    bsz, s, w = q.shape
    return pl.pallas_call(
        functools.partial(_sb_kernel, bq=blk, bk=bk, sub=min(bk, 256)),
        grid=(bsz, w // LANES, s // blk),
        in_specs=[pl.BlockSpec((1, blk, LANES), lambda b, h, i: (b, i, h)),
                  pl.BlockSpec((1, s, LANES), lambda b, h, i: (b, 0, h)),
                  pl.BlockSpec((1, 1) + vt.shape[2:], lambda b, h, i: (b, h, 0, 0, 0))],
        out_specs=pl.BlockSpec((1, blk, LANES), lambda b, h, i: (b, i, h)),
        out_shape=jax.ShapeDtypeStruct((bsz, s, w), BF16),
        scratch_shapes=[pltpu.VMEM((2, bk, 2 * blk), F32)],
        name="sb_attention",
        compiler_params=_cparams(("arbitrary", "arbitrary", "arbitrary")),
    )(q, k, vt)


def _fold_rows(x, target):
    while x.shape[0] > target:
        h = x.shape[0] // 2
        x = x[:h] + x[h:]
    return x


def _flash_update_t(s, m, acc, vt):
    m_new = jnp.maximum(m, jnp.max(s, axis=0, keepdims=True))
    p = jnp.exp(s - m_new)
    acc = jnp.exp(m - m_new) * acc + _nn(vt, p.astype(BF16))
    return m_new, acc


def _flash_init(v_rows, r):
    return (jnp.full((1, r), NEG_BIG, F32), jnp.zeros((v_rows, r), F32))


def _flash_pipelined_t(qq, k_ref, vt_block, s_ref, n_blocks, bk, loop_bias, tail_bias):
    last = n_blocks - 1
    n_pairs = last // 2

    def qk(kb, slot):
        off = pl.multiple_of(kb * bk, bk)
        s_ref[slot] = _nt(k_ref[0, pl.ds(off, bk), :], qq)

    def consume(kb, slot, carry, bias_fn):
        s = s_ref[slot]
        if bias_fn is not None:
            s = s + bias_fn(kb)
        return _flash_update_t(s, *carry, vt_block(kb))

    def pair(p, carry):
        qk(2 * p + 1, 1)
        carry = consume(2 * p, 0, carry, loop_bias)
        qk(2 * p + 2, 0)
        return consume(2 * p + 1, 1, carry, loop_bias)

    qk(0, 0)
    carry = lax.fori_loop(0, n_pairs, pair, _flash_init(vt_block(0).shape[0], qq.shape[0]))
    kb = 2 * n_pairs
    qk(jnp.minimum(kb + 1, last), 1)
    carry = consume(kb, 0, carry, tail_bias)
    return lax.cond(kb < last, lambda cr: consume(kb + 1, 1, cr, tail_bias), lambda cr: cr, carry)


def _dsa_kernel(qd_ref, qx_ref, wt_ref, kd_ref, kx_ref, vt_ref, o_ref, key_ref, s_ref, *, bq, bk, sub, topk):
    qb = pl.program_id(1)
    nkb = (qb * bq) // bk + 1
    lo_half, hi_half = _lane_half_masks()
    halves = (lo_half, hi_half)
    n_heads = qd_ref.shape[2] // HEAD_DIM

    qx = qx_ref[0]
    wt = wt_ref[0]
    qx_heads = []
    for hx in range(N_IDX_HEADS):
        chunk = qx[:, (hx // 2) * LANES:(hx // 2 + 1) * LANES]
        qx_heads.append(jnp.where(halves[hx % 2], chunk, jnp.zeros_like(chunk)))

    def score_block(kb, _):
        off = pl.multiple_of(kb * bk, bk)
        kx = kx_ref[0, pl.ds(off, bk), :]
        score = jnp.zeros((bk, bq), F32)
        for hx in range(N_IDX_HEADS):
            logit = _nt(kx, qx_heads[hx])
            r = jnp.maximum(logit, 0.0).astype(BF16).astype(F32)
            score = score + r * wt[hx:hx + 1, :]
        bits = lax.bitcast_convert_type(score, I32)
        key = bits ^ (lax.shift_right_arithmetic(bits, 31) & 0x7FFFFFFF)
        key = jnp.where(key == -1, 0, key)
        kpos = lax.broadcasted_iota(I32, (bk, bq), 0) + (kb * bk - qb * bq)
        qpos = lax.broadcasted_iota(I32, (bk, bq), 1)
        key_ref[kb] = jnp.where(kpos <= qpos, key, INT_MIN)
        return 0

    lax.fori_loop(0, nkb, score_block, 0)

    q_t = qb * bq + lax.broadcasted_iota(I32, (1, bq), 1)
    k_eff = jnp.minimum(topk, q_t + 1).astype(F32)

    def count_ge(thr):
        def body(kb, acc):
            for c in range(bk // COUNT_ROWS):
                keys = key_ref[kb, c * COUNT_ROWS:(c + 1) * COUNT_ROWS, :]
                acc = acc + _fold_rows(jnp.where(keys >= thr, 1.0, 0.0), 8)
            return acc
        acc = lax.fori_loop(0, nkb, body, jnp.zeros((8, bq), F32))
        return jnp.sum(acc, axis=0, keepdims=True)

    def bit_step(i, prefix):
        cand = prefix | lax.shift_left(jnp.int32(1), 31 - i)
        cnt = count_ge(cand ^ INT_MIN)
        return jnp.where(cnt >= k_eff, cand, prefix)

    prefix = lax.fori_loop(0, 32, bit_step, jnp.zeros((1, bq), I32))
    thr = prefix ^ INT_MIN
    need = k_eff - count_ge(thr + 1)

    rs = lax.broadcasted_iota(I32, (sub, sub), 0)
    cs = lax.broadcasted_iota(I32, (sub, sub), 1)
    earlier = (cs < rs).astype(BF16)

    def mask_block(kb, seen):
        for j in range(bk // sub):
            key = key_ref[kb, j * sub:(j + 1) * sub, :]
            eq = jnp.where(key == thr, 1.0, 0.0)
            rank = _nn(earlier, eq.astype(BF16)) + seen
            take = jnp.where(rank < need, 1, 0)
            sel = (key + take) > thr
            key_ref[kb, j * sub:(j + 1) * sub, :] = lax.bitcast_convert_type(jnp.where(sel, 0.0, NEG_BIG), I32)
            seen = seen + jnp.sum(eq, axis=0, keepdims=True)
        return seen

    lax.fori_loop(0, nkb, mask_block, jnp.zeros((1, bq), F32))

    qd = qd_ref[0]

    def pair_bias(kb):
        bias = lax.bitcast_convert_type(key_ref[kb], F32)
        return jnp.concatenate([bias, bias], axis=1)

    for hp in range(n_heads // 2):
        chunk = qd[:, hp * LANES:(hp + 1) * LANES]
        zero = jnp.zeros_like(chunk)
        qq = jnp.concatenate([jnp.where(lo_half, chunk, zero), jnp.where(hi_half, chunk, zero)], axis=0)

        _, acc = _flash_pipelined_t(qq, kd_ref, lambda kb: vt_ref[0, kb], s_ref, nkb, bk, pair_bias, pair_bias)
        o = acc[:HEAD_DIM] / acc[HEAD_DIM:HEAD_DIM + 1]
        pair = jnp.concatenate([o[:, :bq], o[:, bq:]], axis=0)
        o_ref[0, :, hp * LANES:(hp + 1) * LANES] = pair.T.astype(o_ref.dtype)


def _dsa_attention(qd, qx, wt, kd2, kx2, vt2, blk, bk, topk):
    bsz, s, wq = qd.shape
    kspec = pl.BlockSpec((1, s, LANES), lambda b, i: (b, 0, 0))
    return pl.pallas_call(
        functools.partial(_dsa_kernel, bq=blk, bk=bk, sub=min(bk, 256), topk=topk),
        grid=(bsz, s // blk),
        in_specs=[pl.BlockSpec((1, blk, wq), lambda b, i: (b, i, 0)),
                  pl.BlockSpec((1, blk, qx.shape[2]), lambda b, i: (b, i, 0)),
                  pl.BlockSpec((1, wt.shape[1], blk), lambda b, i: (b, 0, i)),
                  kspec, kspec,
                  pl.BlockSpec((1, s // bk, LANES, bk), lambda b, i: (b, 0, 0, 0))],
        out_specs=pl.BlockSpec((1, blk, wq), lambda b, i: (b, i, 0)),
        out_shape=jax.ShapeDtypeStruct((bsz, s, wq), BF16),
        scratch_shapes=[pltpu.VMEM((s // bk, bk, blk), I32), pltpu.VMEM((2, bk, 2 * blk), F32)],
        name="dsa_attention",
        compiler_params=_cparams(("arbitrary", "arbitrary")),
    )(qd, qx, wt, kd2, kx2, vt2)


def _diff_kernel(q_ref, k_ref, vt_ref, lam_ref, g_ref, o_ref, s_ref, *, bq, bk, lambda_init):
    qi = pl.program_id(2)
    q = q_ref[0]
    lo_half, hi_half = _lane_half_masks()
    zero = jnp.zeros_like(q)
    qq = jnp.concatenate([jnp.where(lo_half, q, zero), jnp.where(hi_half, q, zero)], axis=0)
    n_blocks = (qi * bq) // bk + 1

    def causal_bias(kb):
        kpos = lax.broadcasted_iota(I32, (bk, bq), 0) + (kb * bk - qi * bq)
        qpos = lax.broadcasted_iota(I32, (bk, bq), 1)
        bias = jnp.where(kpos <= qpos, 0.0, NEG_BIG)
        return jnp.concatenate([bias, bias], axis=1)

    _, acc = _flash_pipelined_t(qq, k_ref, lambda kb: vt_ref[0, 0, kb], s_ref, n_blocks, bk, None, causal_bias)
    l = acc[LANES:LANES + 1]
    acc = acc[:LANES]

    lp = lam_ref[...]
    lam = (jnp.exp(jnp.sum(lp[0:1] * lp[1:2], axis=1, keepdims=True))
           - jnp.exp(jnp.sum(lp[2:3] * lp[3:4], axis=1, keepdims=True)) + lambda_init)
    o = acc[:, :bq] / l[:, :bq] - lam * (acc[:, bq:] / l[:, bq:])
    ms = jnp.mean(o * o, axis=0, keepdims=True)
    o = o * lax.rsqrt(ms + LN_EPS) * g_ref[...] * (1.0 - lambda_init)
    o_ref[0] = o.T.astype(o_ref.dtype)


def _diff_attention(q, k, vt, lam_params, subln_g, blk, bk, lambda_init):
    bsz, s, w = q.shape
    return pl.pallas_call(
        functools.partial(_diff_kernel, bq=blk, bk=bk, lambda_init=lambda_init),
        grid=(bsz, w // LANES, s // blk),
        in_specs=[pl.BlockSpec((1, blk, LANES), lambda b, h, i: (b, i, h)),
                  pl.BlockSpec((1, s, LANES), lambda b, h, i: (b, 0, h)),
                  pl.BlockSpec((1, 1) + vt.shape[2:], lambda b, h, i: (b, h, 0, 0, 0)),
                  pl.BlockSpec(lam_params.shape, lambda b, h, i: (0, 0)),
                  pl.BlockSpec((LANES, 1), lambda b, h, i: (0, 0))],
        out_specs=pl.BlockSpec((1, blk, LANES), lambda b, h, i: (b, i, h)),
        out_shape=jax.ShapeDtypeStruct((bsz, s, w), BF16),
        scratch_shapes=[pltpu.VMEM((2, bk, 2 * blk), F32)],
        name="diff_attention",
        compiler_params=_cparams(("arbitrary", "arbitrary", "arbitrary")),
    )(q, k, vt, lam_params, subln_g.reshape(LANES, 1))


def _pick(n, prefs):
    for p in prefs:
        if n % p == 0:
            return p
    return n


def kernel(x, c, positions, w_mod, b_mod, w_in_even, w_out_even, w_in_odd, lam_q1, lam_k1, lam_q2, lam_k2,
           subln_g, w_out_odd, ln_mix_g, ln_mix_b, w_gate, w_up, w_down, ln_ffn_g, ln_ffn_b):
    bsz, s, d = x.shape
    depth = w_mod.shape[0]
    alpha = (2 * depth) ** 0.25
    dff = w_gate.shape[2]
    rows = bsz * s
    tm = _pick(s, (512, 256, 128))
    tf = _pick(dff, (1408, 1024, 512, 256, 128))
    blk = _pick(s, (256, 128))
    bkw = _pick(s, (1024, 512, 256))
    scale = HEAD_DIM ** -0.5

    inv = ROPE_THETA ** (-jnp.arange(0, HEAD_DIM, 2, dtype=F32) / HEAD_DIM)
    ang = positions.astype(F32)[..., None] * inv
    cos, sin = jnp.cos(ang).reshape(rows, -1), jnp.sin(ang).reshape(rows, -1)
    cos128 = jnp.concatenate([cos] * 4, axis=1)
    sg128 = jnp.concatenate([-sin, sin, -sin, sin], axis=1)

    mod = _modulation(c, w_mod.astype(BF16), b_mod)
    xf = x.reshape(rows, d)

    for i in range(depth):
        mod_l = mod[i].reshape(bsz, 6, d)
        if i % 2 == 0:
            w = w_in_even[i // 2]
            n_sb = n_dsa = d // (2 * HEAD_DIM)
            sbw, dsw, ixw = n_sb * HEAD_DIM, n_dsa * HEAD_DIM, N_IDX_HEADS * HEAD_DIM
            offs = [0]
            for width in (sbw, sbw, sbw, dsw, HEAD_DIM, HEAD_DIM, ixw, HEAD_DIM, N_IDX_HEADS):
                offs.append(offs[-1] + width)
            col = lambda j: w[:, offs[j]:offs[j + 1]]
            pad = jnp.zeros((d, LANES - N_IDX_HEADS), w.dtype)
            wp = jnp.concatenate([col(0), col(1), col(2), col(5), col(5), col(8), pad], axis=1)
            wr = jnp.concatenate([col(3), col(6), col(4), col(4), col(7), col(7)], axis=1)
            plain_outs = ((0, sbw, scale), (sbw, sbw, 1.0), (2 * sbw, sbw, 1.0),
                          (3 * sbw, LANES, 1.0), (3 * sbw + LANES, LANES, N_IDX_HEADS ** -0.5))
            rope_outs = ((0, dsw, scale), (dsw, ixw, scale), (dsw + ixw, LANES, 1.0),
                         (dsw + ixw + LANES, LANES, 1.0))
            dts = (BF16, BF16, BF16, BF16, F32, BF16, BF16, BF16, BF16)
            q_sb, k_sb, v_sb, vd2, wx, qd, qx, kd2, kx2 = _inproj(
                xf, mod_l, cos128, sg128, wp.astype(BF16), wr.astype(BF16), _rot_partner(wr).astype(BF16),
                plain_outs, rope_outs, dts, s, tm)
            r3 = lambda t: t.reshape(bsz, s, t.shape[1])
            vt_sb = v_sb.reshape(bsz, s // bkw, bkw, sbw // LANES, LANES).transpose(0, 3, 1, 4, 2)
            o_sb = _sb_attention(r3(q_sb), r3(k_sb), vt_sb, blk, bkw)
            wt = r3(wx)[:, :, :8].swapaxes(1, 2)
            vt2 = vd2.reshape(bsz, s // bkw, bkw, LANES).swapaxes(2, 3)
            vt2 = jnp.concatenate([vt2[:, :, :HEAD_DIM], jnp.ones_like(vt2[:, :, HEAD_DIM:])], axis=2)
            o_dsa = _dsa_attention(r3(qd), r3(qx), wt, r3(kd2), r3(kx2), vt2, blk, bkw,
                                   min(DSA_TOPK_MAX, s // 4))
            w_out = w_out_even[i // 2].astype(BF16)
            o_list = [o_sb.reshape(rows, sbw), o_dsa.reshape(rows, dsw)]
            w_list = [w_out[:sbw], w_out[sbw:]]
        else:
            j = i // 2
            w = w_in_odd[j]
            dw = w.shape[1] // 3
            lambda_init = 0.8 - 0.6 * math.exp(-0.3 * i)
            wp = w[:, 2 * dw:]
            wr = w[:, :2 * dw]
            plain_outs = ((0, dw, 1.0),)
            rope_outs = ((0, dw, scale), (dw, dw, 1.0))
            v_df, q_df, k_df = _inproj(
                xf, mod_l, cos128, sg128, wp.astype(BF16), wr.astype(BF16), _rot_partner(wr).astype(BF16),
                plain_outs, rope_outs, (BF16, BF16, BF16), s, tm)
            r3 = lambda t: t.reshape(bsz, s, t.shape[1])
            lam_params = jnp.stack([lam_q1[j], lam_k1[j], lam_q2[j], lam_k2[j]]).astype(F32)
            vt = v_df.reshape(bsz, s // bkw, bkw, dw // LANES, LANES).transpose(0, 3, 1, 4, 2)
            vt = jnp.concatenate([vt, jnp.ones_like(vt[:, :, :, :ONES_ROWS])], axis=3)
            o_df = _diff_attention(r3(q_df), r3(k_df), vt, lam_params, subln_g[j].astype(F32), blk, bkw,
                                   lambda_init)
            o_list = [o_df.reshape(rows, dw)]
            w_list = [w_out_odd[j].astype(BF16)]
        xf = _outproj_ln(o_list, w_list, xf, mod_l, ln_mix_g[i], ln_mix_b[i], alpha, s, tm)
        xf = _ffn_ln(xf, mod_l, w_gate[i].astype(BF16), w_up[i].astype(BF16), w_down[i].astype(BF16),
                     ln_ffn_g[i], ln_ffn_b[i], alpha, s, tm, tf)
    return xf.reshape(bsz, s, d)
```

```python
import functools
import math

import jax
import jax.numpy as jnp
from jax import lax
from jax.experimental import pallas as pl
from jax.experimental.pallas import tpu as pltpu

HEAD_DIM = 64
N_IDX_HEADS = 4
DSA_TOPK_MAX = 256
ROPE_THETA = 10000.0
LN_EPS = 1e-5
LANES = 128
NEG_BIG = -1e30
INT_MIN = -2 ** 31
LOG2E = 1.4426950408889634
ONES_ROWS = 16
COUNT_ROWS = 64
COUNT_ROWS16 = 128
I16_ROWS = 16
HALF_BIAS = 2 ** 15

F32 = jnp.float32
BF16 = jnp.bfloat16
I32 = jnp.int32
I16 = jnp.int16

VMEM_LIMIT = 56 * 1024 * 1024


def _cparams(sem):
    return pltpu.CompilerParams(dimension_semantics=sem, vmem_limit_bytes=VMEM_LIMIT)


def _nt(a, b):
    return lax.dot_general(a, b, (((1,), (1,)), ((), ())), preferred_element_type=F32)


def _nn(a, b):
    return jnp.dot(a, b, preferred_element_type=F32)


def _layer_norm(v, g, b):
    mu = jnp.mean(v, axis=-1, keepdims=True)
    d = v - mu
    var = jnp.mean(d * d, axis=-1, keepdims=True)
    return d * lax.rsqrt(var + LN_EPS) * g + b


def _mod_kernel(c_ref, w_ref, b_ref, o_ref):
    c = c_ref[...]
    a = (c * jax.nn.sigmoid(c)).astype(BF16)
    o_ref[0] = _nn(a, w_ref[0]) + b_ref[0]


def _modulation(c, w_mod_bf, b_mod):
    depth, d, n = w_mod_bf.shape
    bsz = c.shape[0]
    tn = n // 4
    return pl.pallas_call(
        _mod_kernel,
        grid=(depth, n // tn),
        in_specs=[pl.BlockSpec((bsz, d), lambda l, j: (0, 0)),
                  pl.BlockSpec((1, d, tn), lambda l, j: (l, 0, j)),
                  pl.BlockSpec((1, 1, tn), lambda l, j: (l, 0, j))],
        out_specs=pl.BlockSpec((1, bsz, tn), lambda l, j: (l, 0, j)),
        out_shape=jax.ShapeDtypeStruct((depth, bsz, n), F32),
        name="modulation",
        compiler_params=_cparams(("arbitrary", "arbitrary")),
    )(c, w_mod_bf, b_mod.reshape(depth, 1, n))


def _inproj_kernel(x_ref, mod_ref, cos_ref, sg_ref, wp_ref, wr_ref, wrr_ref, *out_refs,
                   plain_outs, rope_outs):
    x = x_ref[...]
    sh = mod_ref[0, 0:1, :]
    sc = mod_ref[0, 1:2, :]
    h = (x * (1.0 + sc) + sh).astype(BF16)
    plain = _nn(h, wp_ref[...])
    rope = _nn(h, wr_ref[...])
    rot = _nn(h, wrr_ref[...])
    cos = cos_ref[...]
    sg = sg_ref[...]
    n = 0
    for (c0, width, scale) in plain_outs:
        o = out_refs[n]
        v = plain[:, c0:c0 + width]
        if scale != 1.0:
            v = v * scale
        if o.dtype == F32:
            v = v.astype(BF16).astype(F32)
        o[...] = v.astype(o.dtype)
        n += 1
    for (c0, width, scale) in rope_outs:
        o = out_refs[n]
        for j in range(width // LANES):
            sl = slice(c0 + j * LANES, c0 + (j + 1) * LANES)
            v = rope[:, sl] * cos + rot[:, sl] * sg
            if scale != 1.0:
                v = v * scale
            o[:, j * LANES:(j + 1) * LANES] = v.astype(o.dtype)
        n += 1


def _inproj(xf, mod_l, cos128, sg128, wp, wr, wrr, plain_outs, rope_outs, out_dtypes, rows_per_batch, tm):
    r, d = xf.shape
    tiles_per_batch = rows_per_batch // tm
    widths = [w for (_, w, _) in plain_outs] + [w for (_, w, _) in rope_outs]
    row = lambda i: (i, 0)
    const = lambda i: (0, 0)
    return pl.pallas_call(
        functools.partial(_inproj_kernel, plain_outs=plain_outs, rope_outs=rope_outs),
        grid=(r // tm,),
        in_specs=[pl.BlockSpec((tm, d), row),
                  pl.BlockSpec((1, 6, d), lambda i: (i // tiles_per_batch, 0, 0)),
                  pl.BlockSpec((tm, LANES), row),
                  pl.BlockSpec((tm, LANES), row),
                  pl.BlockSpec(wp.shape, const),
                  pl.BlockSpec(wr.shape, const),
                  pl.BlockSpec(wrr.shape, const)],
        out_specs=[pl.BlockSpec((tm, w), row) for w in widths],
        out_shape=[jax.ShapeDtypeStruct((r, w), dt) for w, dt in zip(widths, out_dtypes)],
        name="inproj",
        compiler_params=_cparams(("arbitrary",)),
    )(xf, mod_l, cos128, sg128, wp, wr, wrr)


def _rot_partner(w):
    d, n = w.shape
    return w.reshape(d, n // HEAD_DIM, 2, HEAD_DIM // 2)[:, :, ::-1, :].reshape(d, n)


def _outproj_kernel(*refs, n_in, alpha):
    o_refs = refs[:n_in]
    w_refs = refs[n_in:2 * n_in]
    x_ref, mod_ref, g_ref, b_ref, out_ref = refs[2 * n_in:]
    y = _nn(o_refs[0][...], w_refs[0][...])
    for a, w in zip(o_refs[1:], w_refs[1:]):
        y = y + _nn(a[...], w[...])
    gate = mod_ref[0, 2:3, :]
    v = alpha * x_ref[...] + (1.0 + gate) * y
    out_ref[...] = _layer_norm(v, g_ref[...], b_ref[...])


def _outproj_ln(o_list, w_list, xf, mod_l, g, b, alpha, rows_per_batch, tm):
    r, d = xf.shape
    tiles_per_batch = rows_per_batch // tm
    row = lambda i: (i, 0)
    const = lambda i: (0, 0)
    n_in = len(o_list)
    return pl.pallas_call(
        functools.partial(_outproj_kernel, n_in=n_in, alpha=alpha),
        grid=(r // tm,),
        in_specs=([pl.BlockSpec((tm, o.shape[1]), row) for o in o_list]
                  + [pl.BlockSpec(w.shape, const) for w in w_list]
                  + [pl.BlockSpec((tm, d), row),
                     pl.BlockSpec((1, 6, d), lambda i: (i // tiles_per_batch, 0, 0)),
                     pl.BlockSpec((1, d), const),
                     pl.BlockSpec((1, d), const)]),
        out_specs=pl.BlockSpec((tm, d), row),
        out_shape=jax.ShapeDtypeStruct((r, d), F32),
        name="outproj_ln",
        compiler_params=_cparams(("arbitrary",)),
    )(*o_list, *w_list, xf, mod_l, g.reshape(1, d), b.reshape(1, d))


def _ffn_kernel(x_ref, mod_ref, wg_ref, wu_ref, wd_ref, g_ref, b_ref, out_ref, h_ref, acc_ref, *, alpha):
    f = pl.program_id(1)

    @pl.when(f == 0)
    def _():
        sh = mod_ref[0, 3:4, :]
        sc = mod_ref[0, 4:5, :]
        h_ref[...] = (x_ref[...] * (1.0 + sc) + sh).astype(BF16)
        acc_ref[...] = jnp.zeros_like(acc_ref)

    h = h_ref[...]
    gate = _nn(h, wg_ref[...])
    up = _nn(h, wu_ref[...])
    a = (gate * jax.nn.sigmoid(gate) * up).astype(BF16)
    acc_ref[...] += _nn(a, wd_ref[...])

    @pl.when(f == pl.num_programs(1) - 1)
    def _():
        gf = mod_ref[0, 5:6, :]
        v = alpha * x_ref[...] + (1.0 + gf) * acc_ref[...]
        out_ref[...] = _layer_norm(v, g_ref[...], b_ref[...])


def _ffn_ln(xf, mod_l, wg, wu, wd, g, b, alpha, rows_per_batch, tm, tf):
    r, d = xf.shape
    dff = wg.shape[1]
    tiles_per_batch = rows_per_batch // tm
    return pl.pallas_call(
        functools.partial(_ffn_kernel, alpha=alpha),
        grid=(r // tm, dff // tf),
        in_specs=[pl.BlockSpec((tm, d), lambda i, f: (i, 0)),
                  pl.BlockSpec((1, 6, d), lambda i, f: (i // tiles_per_batch, 0, 0)),
                  pl.BlockSpec((d, tf), lambda i, f: (0, f)),
                  pl.BlockSpec((d, tf), lambda i, f: (0, f)),
                  pl.BlockSpec((tf, d), lambda i, f: (f, 0)),
                  pl.BlockSpec((1, d), lambda i, f: (0, 0)),
                  pl.BlockSpec((1, d), lambda i, f: (0, 0))],
        out_specs=pl.BlockSpec((tm, d), lambda i, f: (i, 0)),
        out_shape=jax.ShapeDtypeStruct((r, d), F32),
        scratch_shapes=[pltpu.VMEM((tm, d), BF16), pltpu.VMEM((tm, d), F32)],
        name="ffn_ln",
        compiler_params=_cparams(("arbitrary", "arbitrary")),
    )(xf, mod_l, wg, wu, wd, g.reshape(1, d), b.reshape(1, d))


def _lane_half_masks():
    lane = lax.broadcasted_iota(I32, (1, LANES), 1)
    return lane < HEAD_DIM, lane >= HEAD_DIM


def _sb_kernel(q_ref, k_ref, vt_ref, o_ref, z_ref, *, bq, bk, sub):
    qi = pl.program_id(2)
    q = q_ref[0]
    lo_half, hi_half = _lane_half_masks()
    zero = jnp.zeros_like(q)
    qq = jnp.concatenate([jnp.where(lo_half, q, zero), jnp.where(hi_half, q, zero)], axis=0)
    rs = lax.broadcasted_iota(I32, (sub, sub), 0)
    cs = lax.broadcasted_iota(I32, (sub, sub), 1)
    later = (cs > rs).astype(BF16)
    n_full = (qi * bq) // bk

    def qk(kb, slot):
        off = pl.multiple_of(jnp.maximum(kb, 0) * bk, bk)
        z_ref[slot] = _nt(k_ref[0, pl.ds(off, bk), :], qq)

    def consume(kb, slot, carry, diag):
        c, acc = carry
        z = z_ref[slot]
        sp = jnp.maximum(z, 0.0) + jnp.log(1.0 + jnp.exp2(jnp.abs(z) * -LOG2E))
        if diag:
            kpos = lax.broadcasted_iota(I32, (bk, bq), 0) + (kb * bk - qi * bq)
            qpos = lax.broadcasted_iota(I32, (bk, bq), 1)
            past = kpos < qpos
            past = jnp.concatenate([past, past], axis=1)
            sp = jnp.where(past, sp, 0.0)
        u = z - sp
        sp_b = sp.astype(BF16)
        ws = [None] * (bk // sub)
        for j in reversed(range(bk // sub)):
            tail = _nn(later, sp_b[j * sub:(j + 1) * sub])
            ws[j] = jnp.exp(u[j * sub:(j + 1) * sub] - tail - c)
            c = c + tail[0:1] + sp[j * sub:j * sub + 1]
        w = jnp.concatenate(ws, axis=0)
        if diag:
            w = jnp.where(past, w, 0.0)
        return c, acc + _nn(vt_ref[0, 0, kb], w.astype(BF16))

    def pair(p, carry):
        kb = n_full - 1 - 2 * p
        qk(kb - 1, 0)
        carry = consume(kb, 1, carry, False)
        qk(kb - 2, 1)
        return consume(kb - 1, 0, carry, False)

    carry = (jnp.zeros((1, 2 * bq), F32), jnp.zeros((LANES, 2 * bq), F32))
    qk(n_full, 0)
    qk(n_full - 1, 1)
    carry = consume(n_full, 0, carry, True)
    carry = lax.fori_loop(0, n_full // 2, pair, carry)
    _, acc = lax.cond(n_full % 2 == 1, lambda cr: consume(0, 1, cr, False), lambda cr: cr, carry)
    out_t = jnp.concatenate([acc[:HEAD_DIM, :bq], acc[HEAD_DIM:, bq:]], axis=0)
    o_ref[0] = out_t.T.astype(o_ref.dtype)


def _sb_attention(q, k, vt, blk, bk):
    bsz, s, w = q.shape
    return pl.pallas_call(
        functools.partial(_sb_kernel, bq=blk, bk=bk, sub=min(bk, 256)),
        grid=(bsz, w // LANES, s // blk),
        in_specs=[pl.BlockSpec((1, blk, LANES), lambda b, h, i: (b, i, h)),
                  pl.BlockSpec((1, s, LANES), lambda b, h, i: (b, 0, h)),
                  pl.BlockSpec((1, 1) + vt.shape[2:], lambda b, h, i: (b, h, 0, 0, 0))],
        out_specs=pl.BlockSpec((1, blk, LANES), lambda b, h, i: (b, i, h)),
        out_shape=jax.ShapeDtypeStruct((bsz, s, w), BF16),
        scratch_shapes=[pltpu.VMEM((2, bk, 2 * blk), F32)],
        name="sb_attention",
        compiler_params=_cparams(("arbitrary", "arbitrary", "arbitrary")),
    )(q, k, vt)


def _fold_rows(x, target):
    while x.shape[0] > target:
        h = x.shape[0] // 2
        x = x[:h] + x[h:]
    return x


def _flash_update_t(s, m, acc, vt):
    m_new = jnp.maximum(m, jnp.max(s, axis=0, keepdims=True))
    p = jnp.exp(s - m_new)
    acc = jnp.exp(m - m_new) * acc + _nn(vt, p.astype(BF16))
    return m_new, acc


def _flash_init(v_rows, r):
    return (jnp.full((1, r), NEG_BIG, F32), jnp.zeros((v_rows, r), F32))


def _flash_pipelined_t(qq, k_ref, vt_block, s_ref, n_blocks, bk, loop_bias, tail_bias):
    last = n_blocks - 1
    n_pairs = last // 2

    def qk(kb, slot):
        off = pl.multiple_of(kb * bk, bk)
        s_ref[slot] = _nt(k_ref[0, pl.ds(off, bk), :], qq)

    def consume(kb, slot, carry, bias_fn):
        s = s_ref[slot]
        if bias_fn is not None:
            s = s + bias_fn(kb)
        return _flash_update_t(s, *carry, vt_block(kb))

    def pair(p, carry):
        qk(2 * p + 1, 1)
        carry = consume(2 * p, 0, carry, loop_bias)
        qk(2 * p + 2, 0)
        return consume(2 * p + 1, 1, carry, loop_bias)

    qk(0, 0)
    carry = lax.fori_loop(0, n_pairs, pair, _flash_init(vt_block(0).shape[0], qq.shape[0]))
    kb = 2 * n_pairs
    qk(jnp.minimum(kb + 1, last), 1)
    carry = consume(kb, 0, carry, tail_bias)
    return lax.cond(kb < last, lambda cr: consume(kb + 1, 1, cr, tail_bias), lambda cr: cr, carry)


def _dsa_kernel(qd_ref, qx_ref, wt_ref, kd_ref, kx_ref, vt_ref, o_ref, key_ref, hi_ref, lo_ref, s_ref, *,
                bq, bk, sub, topk):
    qb = pl.program_id(1)
    nkb = (qb * bq) // bk + 1
    lo_half, hi_half = _lane_half_masks()
    halves = (lo_half, hi_half)
    n_heads = qd_ref.shape[2] // HEAD_DIM

    qx = qx_ref[0]
    wt = wt_ref[0]
    qx_heads = []
    for hx in range(N_IDX_HEADS):
        chunk = qx[:, (hx // 2) * LANES:(hx // 2 + 1) * LANES]
        qx_heads.append(jnp.where(halves[hx % 2], chunk, jnp.zeros_like(chunk)))

    def score_block(kb, masked):
        off = pl.multiple_of(kb * bk, bk)
        kx = kx_ref[0, pl.ds(off, bk), :]
        score = jnp.zeros((bk, bq), F32)
        for hx in range(N_IDX_HEADS):
            r = jnp.maximum(_nt(kx, qx_heads[hx]).astype(BF16), 0.0).astype(F32)
            score = score + r * wt[hx:hx + 1, :]
        bits = lax.bitcast_convert_type(score, I32)
        key = jnp.where(bits < 0, INT_MIN - bits, bits)
        if masked:
            kpos = lax.broadcasted_iota(I32, (bk, bq), 0) + (kb * bk - qb * bq)
            qpos = lax.broadcasted_iota(I32, (bk, bq), 1)
            key = jnp.where(kpos <= qpos, key, INT_MIN)
        key_ref[kb] = key
        hi_ref[kb] = lax.shift_right_arithmetic(key, 16).astype(I16)
        lo_ref[kb] = ((key & (2 * HALF_BIAS - 1)) - HALF_BIAS).astype(I16)

    def score_loop_body(kb, _):
        score_block(kb, False)
        return 0

    lax.fori_loop(0, nkb - 1, score_loop_body, 0)
    score_block(nkb - 1, True)

    q_t = qb * bq + lax.broadcasted_iota(I32, (1, bq), 1)
    k_eff = jnp.minimum(topk, q_t + 1).astype(F32)

    def count_ge(thr):
        def body(kb, acc):
            for c in range(bk // COUNT_ROWS):
                keys = key_ref[kb, c * COUNT_ROWS:(c + 1) * COUNT_ROWS, :]
                acc = acc + _fold_rows(jnp.where(keys >= thr, 1.0, 0.0), 8)
            return acc
        acc = lax.fori_loop(0, nkb, body, jnp.zeros((8, bq), F32))
        return jnp.sum(acc, axis=0, keepdims=True)

    def count16_ge(ref, thr16):
        one, zero = jnp.int16(1), jnp.int16(0)

        def body(kb, acc):
            for c in range(bk // COUNT_ROWS16):
                half = ref[kb, c * COUNT_ROWS16:(c + 1) * COUNT_ROWS16, :]
                acc = acc + _fold_rows(jnp.where(half >= thr16, one, zero), I16_ROWS)
            return acc
        acc = lax.fori_loop(0, nkb, body, jnp.zeros((I16_ROWS, bq), I16))
        return jnp.sum(acc.astype(F32), axis=0, keepdims=True)

    def kth_largest16(ref, kth):
        def bit_step(i, biased):
            cand = biased | lax.shift_left(jnp.int32(1), 15 - i)
            cnt = count16_ge(ref, (cand - HALF_BIAS).astype(I16))
            return jnp.where(cnt >= kth, cand, biased)
        return lax.fori_loop(0, 16, bit_step, jnp.zeros((1, bq), I32)) - HALF_BIAS

    t_hi = kth_largest16(hi_ref, k_eff)
    above = jnp.where(t_hi >= HALF_BIAS - 1, 0.0,
                      count16_ge(hi_ref, jnp.minimum(t_hi + 1, HALF_BIAS - 1).astype(I16)))
    t_hi16 = t_hi.astype(I16)

    def keep_group(kb, _):
        lo_ref[kb] = jnp.where(hi_ref[kb] == t_hi16, lo_ref[kb], jnp.int16(-HALF_BIAS))
        return 0

    lax.fori_loop(0, nkb, keep_group, 0)
    t_lo = kth_largest16(lo_ref, k_eff - above)
    thr = t_hi * (2 * HALF_BIAS) + (t_lo + HALF_BIAS)
    need = k_eff - count_ge(thr + 1)

    rs = lax.broadcasted_iota(I32, (sub, sub), 0)
    cs = lax.broadcasted_iota(I32, (sub, sub), 1)
    earlier = (cs < rs).astype(BF16)

    def mask_block(kb, seen):
        for j in range(bk // sub):
            key = key_ref[kb, j * sub:(j + 1) * sub, :]
            eq = jnp.where(key == thr, 1.0, 0.0)
            rank = _nn(earlier, eq.astype(BF16)) + seen
            take = jnp.where(rank < need, 1, 0)
            sel = (key + take) > thr
            key_ref[kb, j * sub:(j + 1) * sub, :] = lax.bitcast_convert_type(jnp.where(sel, 0.0, NEG_BIG), I32)
            seen = seen + jnp.sum(eq, axis=0, keepdims=True)
        return seen

    lax.fori_loop(0, nkb, mask_block, jnp.zeros((1, bq), F32))

    qd = qd_ref[0]

    def pair_bias(kb):
        bias = lax.bitcast_convert_type(key_ref[kb], F32)
        return jnp.concatenate([bias, bias], axis=1)

    for hp in range(n_heads // 2):
        chunk = qd[:, hp * LANES:(hp + 1) * LANES]
        zero = jnp.zeros_like(chunk)
        qq = jnp.concatenate([jnp.where(lo_half, chunk, zero), jnp.where(hi_half, chunk, zero)], axis=0)

        _, acc = _flash_pipelined_t(qq, kd_ref, lambda kb: vt_ref[0, kb], s_ref, nkb, bk, pair_bias, pair_bias)
        o = acc[:HEAD_DIM] / acc[HEAD_DIM:HEAD_DIM + 1]
        pair = jnp.concatenate([o[:, :bq], o[:, bq:]], axis=0)
        o_ref[0, :, hp * LANES:(hp + 1) * LANES] = pair.T.astype(o_ref.dtype)


def _dsa_attention(qd, qx, wt, kd2, kx2, vt2, blk, bk, topk):
    bsz, s, wq = qd.shape
    kspec = pl.BlockSpec((1, s, LANES), lambda b, i: (b, 0, 0), pipeline_mode=pl.Buffered(1))
    tiles = (s // bk, bk, blk)
    return pl.pallas_call(
        functools.partial(_dsa_kernel, bq=blk, bk=bk, sub=min(bk, 256), topk=topk),
        grid=(bsz, s // blk),
        in_specs=[pl.BlockSpec((1, blk, wq), lambda b, i: (b, i, 0)),
                  pl.BlockSpec((1, blk, qx.shape[2]), lambda b, i: (b, i, 0)),
                  pl.BlockSpec((1, wt.shape[1], blk), lambda b, i: (b, 0, i)),
                  kspec, kspec,
                  pl.BlockSpec((1, s // bk, LANES, bk), lambda b, i: (b, 0, 0, 0), pipeline_mode=pl.Buffered(1))],
        out_specs=pl.BlockSpec((1, blk, wq), lambda b, i: (b, i, 0)),
        out_shape=jax.ShapeDtypeStruct((bsz, s, wq), BF16),
        scratch_shapes=[pltpu.VMEM(tiles, I32), pltpu.VMEM(tiles, I16), pltpu.VMEM(tiles, I16),
                        pltpu.VMEM((2, bk, 2 * blk), F32)],
        name="dsa_attention",
        compiler_params=_cparams(("arbitrary", "arbitrary")),
    )(qd, qx, wt, kd2, kx2, vt2)


def _diff_kernel(q_ref, k_ref, vt_ref, lam_ref, g_ref, o_ref, s_ref, *, bq, bk, lambda_init):
    qi = pl.program_id(2)
    q = q_ref[0]
    lo_half, hi_half = _lane_half_masks()
    zero = jnp.zeros_like(q)
    qq = jnp.concatenate([jnp.where(lo_half, q, zero), jnp.where(hi_half, q, zero)], axis=0)
    n_blocks = (qi * bq) // bk + 1

    def causal_bias(kb):
        kpos = lax.broadcasted_iota(I32, (bk, bq), 0) + (kb * bk - qi * bq)
        qpos = lax.broadcasted_iota(I32, (bk, bq), 1)
        bias = jnp.where(kpos <= qpos, 0.0, NEG_BIG)
        return jnp.concatenate([bias, bias], axis=1)

    _, acc = _flash_pipelined_t(qq, k_ref, lambda kb: vt_ref[0, 0, kb], s_ref, n_blocks, bk, None, causal_bias)
    l = acc[LANES:LANES + 1]
    acc = acc[:LANES]

    lp = lam_ref[...]
    lam = (jnp.exp(jnp.sum(lp[0:1] * lp[1:2], axis=1, keepdims=True))
           - jnp.exp(jnp.sum(lp[2:3] * lp[3:4], axis=1, keepdims=True)) + lambda_init)
    o = acc[:, :bq] / l[:, :bq] - lam * (acc[:, bq:] / l[:, bq:])
    ms = jnp.mean(o * o, axis=0, keepdims=True)
    o = o * lax.rsqrt(ms + LN_EPS) * g_ref[...] * (1.0 - lambda_init)
    o_ref[0] = o.T.astype(o_ref.dtype)


def _diff_attention(q, k, vt, lam_params, subln_g, blk, bk, lambda_init):
    bsz, s, w = q.shape
    return pl.pallas_call(
        functools.partial(_diff_kernel, bq=blk, bk=bk, lambda_init=lambda_init),
        grid=(bsz, w // LANES, s // blk),
        in_specs=[pl.BlockSpec((1, blk, LANES), lambda b, h, i: (b, i, h)),
                  pl.BlockSpec((1, s, LANES), lambda b, h, i: (b, 0, h)),
                  pl.BlockSpec((1, 1) + vt.shape[2:], lambda b, h, i: (b, h, 0, 0, 0)),
                  pl.BlockSpec(lam_params.shape, lambda b, h, i: (0, 0)),
                  pl.BlockSpec((LANES, 1), lambda b, h, i: (0, 0))],
        out_specs=pl.BlockSpec((1, blk, LANES), lambda b, h, i: (b, i, h)),
        out_shape=jax.ShapeDtypeStruct((bsz, s, w), BF16),
        scratch_shapes=[pltpu.VMEM((2, bk, 2 * blk), F32)],
        name="diff_attention",
        compiler_params=_cparams(("arbitrary", "arbitrary", "arbitrary")),
    )(q, k, vt, lam_params, subln_g.reshape(LANES, 1))


def _pick(n, prefs):
    for p in prefs:
        if n % p == 0:
            return p
    return n


def kernel(x, c, positions, w_mod, b_mod, w_in_even, w_out_even, w_in_odd, lam_q1, lam_k1, lam_q2, lam_k2,
           subln_g, w_out_odd, ln_mix_g, ln_mix_b, w_gate, w_up, w_down, ln_ffn_g, ln_ffn_b):
    bsz, s, d = x.shape
    depth = w_mod.shape[0]
    alpha = (2 * depth) ** 0.25
    dff = w_gate.shape[2]
    rows = bsz * s
    tm = _pick(s, (512, 256, 128))
    tf = _pick(dff, (1408, 1024, 512, 256, 128))
    blk = _pick(s, (256, 128))
    bk_sb = _pick(s, (1024, 512, 256))
    bk_dsa = _pick(s, (1024, 512, 256))
    bk_diff = _pick(s, (1024, 512, 256))
    scale = HEAD_DIM ** -0.5

    inv = ROPE_THETA ** (-jnp.arange(0, HEAD_DIM, 2, dtype=F32) / HEAD_DIM)
    ang = positions.astype(F32)[..., None] * inv
    cos, sin = jnp.cos(ang).reshape(rows, -1), jnp.sin(ang).reshape(rows, -1)
    cos128 = jnp.concatenate([cos] * 4, axis=1)
    sg128 = jnp.concatenate([-sin, sin, -sin, sin], axis=1)

    mod = _modulation(c, w_mod.astype(BF16), b_mod)
    xf = x.reshape(rows, d)

    for i in range(depth):
        mod_l = mod[i].reshape(bsz, 6, d)
        if i % 2 == 0:
            w = w_in_even[i // 2]
            n_sb = n_dsa = d // (2 * HEAD_DIM)
            sbw, dsw, ixw = n_sb * HEAD_DIM, n_dsa * HEAD_DIM, N_IDX_HEADS * HEAD_DIM
            offs = [0]
            for width in (sbw, sbw, sbw, dsw, HEAD_DIM, HEAD_DIM, ixw, HEAD_DIM, N_IDX_HEADS):
                offs.append(offs[-1] + width)
            col = lambda j: w[:, offs[j]:offs[j + 1]]
            pad = jnp.zeros((d, LANES - N_IDX_HEADS), w.dtype)
            wp = jnp.concatenate([col(0), col(1), col(2), col(5), col(5), col(8), pad], axis=1)
            wr = jnp.concatenate([col(3), col(6), col(4), col(4), col(7), col(7)], axis=1)
            plain_outs = ((0, sbw, scale), (sbw, sbw, 1.0), (2 * sbw, sbw, 1.0),
                          (3 * sbw, LANES, 1.0), (3 * sbw + LANES, LANES, N_IDX_HEADS ** -0.5))
            rope_outs = ((0, dsw, scale), (dsw, ixw, scale), (dsw + ixw, LANES, 1.0),
                         (dsw + ixw + LANES, LANES, 1.0))
            dts = (BF16, BF16, BF16, BF16, F32, BF16, BF16, BF16, BF16)
            q_sb, k_sb, v_sb, vd2, wx, qd, qx, kd2, kx2 = _inproj(
                xf, mod_l, cos128, sg128, wp.astype(BF16), wr.astype(BF16), _rot_partner(wr).astype(BF16),
                plain_outs, rope_outs, dts, s, tm)
            r3 = lambda t: t.reshape(bsz, s, t.shape[1])
            vt_sb = v_sb.reshape(bsz, s // bk_sb, bk_sb, sbw // LANES, LANES).transpose(0, 3, 1, 4, 2)
            o_sb = _sb_attention(r3(q_sb), r3(k_sb), vt_sb, blk, bk_sb)
            wt = r3(wx)[:, :, :8].swapaxes(1, 2)
            vt2 = vd2.reshape(bsz, s // bk_dsa, bk_dsa, LANES).swapaxes(2, 3)
            vt2 = jnp.concatenate([vt2[:, :, :HEAD_DIM], jnp.ones_like(vt2[:, :, HEAD_DIM:])], axis=2)
            o_dsa = _dsa_attention(r3(qd), r3(qx), wt, r3(kd2), r3(kx2), vt2, blk, bk_dsa,
                                   min(DSA_TOPK_MAX, s // 4))
            w_out = w_out_even[i // 2].astype(BF16)
            o_list = [o_sb.reshape(rows, sbw), o_dsa.reshape(rows, dsw)]
            w_list = [w_out[:sbw], w_out[sbw:]]
        else:
            j = i // 2
            w = w_in_odd[j]
            dw = w.shape[1] // 3
            lambda_init = 0.8 - 0.6 * math.exp(-0.3 * i)
            wp = w[:, 2 * dw:]
            wr = w[:, :2 * dw]
            plain_outs = ((0, dw, 1.0),)
            rope_outs = ((0, dw, scale), (dw, dw, 1.0))
            v_df, q_df, k_df = _inproj(
                xf, mod_l, cos128, sg128, wp.astype(BF16), wr.astype(BF16), _rot_partner(wr).astype(BF16),
                plain_outs, rope_outs, (BF16, BF16, BF16), s, tm)
            r3 = lambda t: t.reshape(bsz, s, t.shape[1])
            lam_params = jnp.stack([lam_q1[j], lam_k1[j], lam_q2[j], lam_k2[j]]).astype(F32)
            vt = v_df.reshape(bsz, s // bk_diff, bk_diff, dw // LANES, LANES).transpose(0, 3, 1, 4, 2)
            vt = jnp.concatenate([vt, jnp.ones_like(vt[:, :, :, :ONES_ROWS])], axis=3)
            o_df = _diff_attention(r3(q_df), r3(k_df), vt, lam_params, subln_g[j].astype(F32), blk, bk_diff,
                                   lambda_init)
            o_list = [o_df.reshape(rows, dw)]
            w_list = [w_out_odd[j].astype(BF16)]
        xf = _outproj_ln(o_list, w_list, xf, mod_l, ln_mix_g[i], ln_mix_b[i], alpha, s, tm)
        xf = _ffn_ln(xf, mod_l, w_gate[i].astype(BF16), w_up[i].astype(BF16), w_down[i].astype(BF16),
                     ln_ffn_g[i], ln_ffn_b[i], alpha, s, tm, tf)
    return xf.reshape(bsz, s, d)
```

```python
import functools
import math

import jax
import jax.numpy as jnp
from jax import lax
from jax.experimental import pallas as pl
from jax.experimental.pallas import tpu as pltpu

HEAD_DIM = 64
N_IDX_HEADS = 4
DSA_TOPK_MAX = 256
ROPE_THETA = 10000.0
LN_EPS = 1e-5
LANES = 128
NEG_BIG = -1e30
INT_MIN = -2 ** 31
LOG2E = 1.4426950408889634
SB_EXP_FLOOR = 105.0
ONES_ROWS = 16
COUNT_ROWS = 64
COUNT_ROWS16 = 128
I16_ROWS = 16
HALF_BIAS = 2 ** 15

F32 = jnp.float32
BF16 = jnp.bfloat16
I32 = jnp.int32
I16 = jnp.int16

VMEM_LIMIT = 56 * 1024 * 1024


def _cparams(sem):
    return pltpu.CompilerParams(dimension_semantics=sem, vmem_limit_bytes=VMEM_LIMIT)


def _nt(a, b):
    return lax.dot_general(a, b, (((1,), (1,)), ((), ())), preferred_element_type=F32)


def _nn(a, b):
    return jnp.dot(a, b, preferred_element_type=F32)


def _layer_norm(v, g, b):
    mu = jnp.mean(v, axis=-1, keepdims=True)
    d = v - mu
    var = jnp.mean(d * d, axis=-1, keepdims=True)
    return d * lax.rsqrt(var + LN_EPS) * g + b


def _mod_kernel(c_ref, w_ref, b_ref, o_ref):
    c = c_ref[...]
    a = (c * jax.nn.sigmoid(c)).astype(BF16)
    o_ref[0] = _nn(a, w_ref[0]) + b_ref[0]


def _modulation(c, w_mod_bf, b_mod):
    depth, d, n = w_mod_bf.shape
    bsz = c.shape[0]
    tn = n // 4
    return pl.pallas_call(
        _mod_kernel,
        grid=(depth, n // tn),
        in_specs=[pl.BlockSpec((bsz, d), lambda l, j: (0, 0)),
                  pl.BlockSpec((1, d, tn), lambda l, j: (l, 0, j)),
                  pl.BlockSpec((1, 1, tn), lambda l, j: (l, 0, j))],
        out_specs=pl.BlockSpec((1, bsz, tn), lambda l, j: (l, 0, j)),
        out_shape=jax.ShapeDtypeStruct((depth, bsz, n), F32),
        name="modulation",
        compiler_params=_cparams(("arbitrary", "arbitrary")),
    )(c, w_mod_bf, b_mod.reshape(depth, 1, n))


def _inproj_kernel(x_ref, mod_ref, cos_ref, sg_ref, wp_ref, wr_ref, wrr_ref, *out_refs,
                   plain_outs, rope_outs):
    x = x_ref[...]
    sh = mod_ref[0, 0:1, :]
    sc = mod_ref[0, 1:2, :]
    h = (x * (1.0 + sc) + sh).astype(BF16)
    plain = _nn(h, wp_ref[...])
    rope = _nn(h, wr_ref[...])
    rot = _nn(h, wrr_ref[...])
    cos = cos_ref[...]
    sg = sg_ref[...]
    n = 0
    for (c0, width, scale) in plain_outs:
        o = out_refs[n]
        v = plain[:, c0:c0 + width]
        if scale != 1.0:
            v = v * scale
        if o.dtype == F32:
            v = v.astype(BF16).astype(F32)
        o[...] = v.astype(o.dtype)
        n += 1
    for (c0, width, scale) in rope_outs:
        o = out_refs[n]
        for j in range(width // LANES):
            sl = slice(c0 + j * LANES, c0 + (j + 1) * LANES)
            v = rope[:, sl] * cos + rot[:, sl] * sg
            if scale != 1.0:
                v = v * scale
            o[:, j * LANES:(j + 1) * LANES] = v.astype(o.dtype)
        n += 1


def _inproj(xf, mod_l, cos128, sg128, wp, wr, wrr, plain_outs, rope_outs, out_dtypes, rows_per_batch, tm):
    r, d = xf.shape
    tiles_per_batch = rows_per_batch // tm
    widths = [w for (_, w, _) in plain_outs] + [w for (_, w, _) in rope_outs]
    row = lambda i: (i, 0)
    const = lambda i: (0, 0)
    return pl.pallas_call(
        functools.partial(_inproj_kernel, plain_outs=plain_outs, rope_outs=rope_outs),
        grid=(r // tm,),
        in_specs=[pl.BlockSpec((tm, d), row),
                  pl.BlockSpec((1, 6, d), lambda i: (i // tiles_per_batch, 0, 0)),
                  pl.BlockSpec((tm, LANES), row),
                  pl.BlockSpec((tm, LANES), row),
                  pl.BlockSpec(wp.shape, const),
                  pl.BlockSpec(wr.shape, const),
                  pl.BlockSpec(wrr.shape, const)],
        out_specs=[pl.BlockSpec((tm, w), row) for w in widths],
        out_shape=[jax.ShapeDtypeStruct((r, w), dt) for w, dt in zip(widths, out_dtypes)],
        name="inproj",
        compiler_params=_cparams(("arbitrary",)),
    )(xf, mod_l, cos128, sg128, wp, wr, wrr)


def _rot_partner(w):
    d, n = w.shape
    return w.reshape(d, n // HEAD_DIM, 2, HEAD_DIM // 2)[:, :, ::-1, :].reshape(d, n)


def _outproj_kernel(*refs, n_in, alpha):
    o_refs = refs[:n_in]
    w_refs = refs[n_in:2 * n_in]
    x_ref, mod_ref, g_ref, b_ref, out_ref = refs[2 * n_in:]
    y = _nn(o_refs[0][...], w_refs[0][...])
    for a, w in zip(o_refs[1:], w_refs[1:]):
        y = y + _nn(a[...], w[...])
    gate = mod_ref[0, 2:3, :]
    v = alpha * x_ref[...] + (1.0 + gate) * y
    out_ref[...] = _layer_norm(v, g_ref[...], b_ref[...])


def _outproj_ln(o_list, w_list, xf, mod_l, g, b, alpha, rows_per_batch, tm):
    r, d = xf.shape
    tiles_per_batch = rows_per_batch // tm
    row = lambda i: (i, 0)
    const = lambda i: (0, 0)
    n_in = len(o_list)
    return pl.pallas_call(
        functools.partial(_outproj_kernel, n_in=n_in, alpha=alpha),
        grid=(r // tm,),
        in_specs=([pl.BlockSpec((tm, o.shape[1]), row) for o in o_list]
                  + [pl.BlockSpec(w.shape, const) for w in w_list]
                  + [pl.BlockSpec((tm, d), row),
                     pl.BlockSpec((1, 6, d), lambda i: (i // tiles_per_batch, 0, 0)),
                     pl.BlockSpec((1, d), const),
                     pl.BlockSpec((1, d), const)]),
        out_specs=pl.BlockSpec((tm, d), row),
        out_shape=jax.ShapeDtypeStruct((r, d), F32),
        name="outproj_ln",
        compiler_params=_cparams(("arbitrary",)),
    )(*o_list, *w_list, xf, mod_l, g.reshape(1, d), b.reshape(1, d))


def _ffn_kernel(x_ref, mod_ref, wg_ref, wu_ref, wd_ref, g_ref, b_ref, out_ref, h_ref, acc_ref, *, alpha):
    f = pl.program_id(1)

    @pl.when(f == 0)
    def _():
        sh = mod_ref[0, 3:4, :]
        sc = mod_ref[0, 4:5, :]
        h_ref[...] = (x_ref[...] * (1.0 + sc) + sh).astype(BF16)
        acc_ref[...] = jnp.zeros_like(acc_ref)

    h = h_ref[...]
    gate = _nn(h, wg_ref[...])
    up = _nn(h, wu_ref[...])
    a = (gate * jax.nn.sigmoid(gate) * up).astype(BF16)
    acc_ref[...] += _nn(a, wd_ref[...])

    @pl.when(f == pl.num_programs(1) - 1)
    def _():
        gf = mod_ref[0, 5:6, :]
        v = alpha * x_ref[...] + (1.0 + gf) * acc_ref[...]
        out_ref[...] = _layer_norm(v, g_ref[...], b_ref[...])


def _ffn_ln(xf, mod_l, wg, wu, wd, g, b, alpha, rows_per_batch, tm, tf):
    r, d = xf.shape
    dff = wg.shape[1]
    tiles_per_batch = rows_per_batch // tm
    return pl.pallas_call(
        functools.partial(_ffn_kernel, alpha=alpha),
        grid=(r // tm, dff // tf),
        in_specs=[pl.BlockSpec((tm, d), lambda i, f: (i, 0)),
                  pl.BlockSpec((1, 6, d), lambda i, f: (i // tiles_per_batch, 0, 0)),
                  pl.BlockSpec((d, tf), lambda i, f: (0, f)),
                  pl.BlockSpec((d, tf), lambda i, f: (0, f)),
                  pl.BlockSpec((tf, d), lambda i, f: (f, 0)),
                  pl.BlockSpec((1, d), lambda i, f: (0, 0)),
                  pl.BlockSpec((1, d), lambda i, f: (0, 0))],
        out_specs=pl.BlockSpec((tm, d), lambda i, f: (i, 0)),
        out_shape=jax.ShapeDtypeStruct((r, d), F32),
        scratch_shapes=[pltpu.VMEM((tm, d), BF16), pltpu.VMEM((tm, d), F32)],
        name="ffn_ln",
        compiler_params=_cparams(("arbitrary", "arbitrary")),
    )(xf, mod_l, wg, wu, wd, g.reshape(1, d), b.reshape(1, d))


def _lane_half_masks():
    lane = lax.broadcasted_iota(I32, (1, LANES), 1)
    return lane < HEAD_DIM, lane >= HEAD_DIM


def _tri_mask(n_keys, n_queries, reps, strict):
    kpos = lax.broadcasted_iota(I32, (n_keys, n_queries), 0)
    qpos = lax.broadcasted_iota(I32, (n_keys, n_queries), 1)
    tri = kpos < qpos if strict else kpos <= qpos
    return jnp.concatenate([tri] * reps, axis=1)


def _switch(index, branches, operand):
    def build(lo, hi):
        if hi - lo == 1:
            return branches[lo]
        mid = (lo + hi) // 2
        return lambda x: lax.cond(index < mid, build(lo, mid), build(mid, hi), x)
    return build(0, len(branches))(operand)


def _sb_kernel(q_ref, k_ref, vt_ref, o_ref, *, bq):
    qi = pl.program_id(2)
    q = q_ref[0]
    lo_half, hi_half = _lane_half_masks()
    zero = jnp.zeros_like(q)
    qq = jnp.concatenate([jnp.where(lo_half, q, zero), jnp.where(hi_half, q, zero)], axis=0)
    rs = lax.broadcasted_iota(I32, (bq, bq), 0)
    cs = lax.broadcasted_iota(I32, (bq, bq), 1)
    later = (cs > rs).astype(BF16)

    def block(kb, c, acc, diag):
        off = pl.multiple_of(kb * bq, bq)
        z = _nt(k_ref[0, pl.ds(off, bq), :], qq)
        sp = jnp.maximum(z, 0.0) + jnp.log(1.0 + jnp.exp2(jnp.abs(z) * -LOG2E))
        if diag:
            past = _tri_mask(bq, bq, 2, strict=True)
            sp = jnp.where(past, sp, 0.0)
        tail = _nn(later, sp.astype(BF16))
        w = jnp.exp(z - sp - tail - c)
        if diag:
            w = jnp.where(past, w, 0.0)
        acc = acc + _nn(vt_ref[0, 0, kb], w.astype(BF16))
        return c + tail[0:1] + sp[0:1], acc

    c, acc = block(qi, jnp.zeros((1, 2 * bq), F32), jnp.zeros((LANES, 2 * bq), F32), True)

    def more(state):
        kb, c_min, _, _ = state
        return jnp.logical_and(kb >= 0, c_min <= SB_EXP_FLOOR)

    def step(state):
        kb, _, c, acc = state
        c, acc = block(kb, c, acc, False)
        return kb - 1, jnp.min(c), c, acc

    _, _, _, acc = lax.while_loop(more, step, (qi - 1, jnp.min(c), c, acc))
    out_t = jnp.concatenate([acc[:HEAD_DIM, :bq], acc[HEAD_DIM:, bq:]], axis=0)
    o_ref[0] = out_t.T.astype(o_ref.dtype)


def _sb_attention(q, k, vt, blk):
    bsz, s, w = q.shape
    return pl.pallas_call(
        functools.partial(_sb_kernel, bq=blk),
        grid=(bsz, w // LANES, s // blk),
        in_specs=[pl.BlockSpec((1, blk, LANES), lambda b, h, i: (b, i, h)),
                  pl.BlockSpec((1, s, LANES), lambda b, h, i: (b, 0, h)),
                  pl.BlockSpec((1, 1) + vt.shape[2:], lambda b, h, i: (b, h, 0, 0, 0))],
        out_specs=pl.BlockSpec((1, blk, LANES), lambda b, h, i: (b, i, h)),
        out_shape=jax.ShapeDtypeStruct((bsz, s, w), BF16),
        name="sb_attention",
        compiler_params=_cparams(("arbitrary", "arbitrary", "arbitrary")),
    )(q, k, vt)


def _fold_rows(x, target):
    while x.shape[0] > target:
        h = x.shape[0] // 2
        x = x[:h] + x[h:]
    return x


def _flash_update_t(s, m, acc, vt):
    m_new = jnp.maximum(m, jnp.max(s, axis=0, keepdims=True))
    p = jnp.exp(s - m_new)
    acc = jnp.exp(m - m_new) * acc + _nn(vt, p.astype(BF16))
    return m_new, acc


def _flash_init(v_rows, r):
    return (jnp.full((1, r), NEG_BIG, F32), jnp.zeros((v_rows, r), F32))


def _flash_pipelined_t(qq, k_ref, vt_block, s_ref, n_blocks, bk, loop_bias, tail_bias):
    last = n_blocks - 1
    n_pairs = last // 2

    def qk(kb, slot):
        off = pl.multiple_of(kb * bk, bk)
        s_ref[slot] = _nt(k_ref[0, pl.ds(off, bk), :], qq)

    def consume(kb, slot, carry, bias_fn):
        s = s_ref[slot]
        if bias_fn is not None:
            s = s + bias_fn(kb)
        return _flash_update_t(s, *carry, vt_block(kb))

    def pair(p, carry):
        qk(2 * p + 1, 1)
        carry = consume(2 * p, 0, carry, loop_bias)
        qk(2 * p + 2, 0)
        return consume(2 * p + 1, 1, carry, loop_bias)

    qk(0, 0)
    carry = lax.fori_loop(0, n_pairs, pair, _flash_init(vt_block(0).shape[0], qq.shape[0]))
    kb = 2 * n_pairs
    qk(jnp.minimum(kb + 1, last), 1)
    carry = consume(kb, 0, carry, tail_bias)
    return lax.cond(kb < last, lambda cr: consume(kb + 1, 1, cr, tail_bias), lambda cr: cr, carry)


def _dsa_kernel(qd_ref, qx_ref, wt_ref, kd_ref, kx_ref, vt_ref, o_ref, key_ref, hi_ref, lo_ref, s_ref, *,
                bq, bk, sub, topk):
    qb = pl.program_id(1)
    nkb = (qb * bq) // bk + 1
    lo_half, hi_half = _lane_half_masks()
    halves = (lo_half, hi_half)
    n_heads = qd_ref.shape[2] // HEAD_DIM

    qx = qx_ref[0]
    wt = wt_ref[0]
    qx_heads = []
    for hx in range(N_IDX_HEADS):
        chunk = qx[:, (hx // 2) * LANES:(hx // 2 + 1) * LANES]
        qx_heads.append(jnp.where(halves[hx % 2], chunk, jnp.zeros_like(chunk)))

    def score_block(kb, masked):
        off = pl.multiple_of(kb * bk, bk)
        kx = kx_ref[0, pl.ds(off, bk), :]
        score = jnp.zeros((bk, bq), F32)
        for hx in range(N_IDX_HEADS):
            r = jnp.maximum(_nt(kx, qx_heads[hx]).astype(BF16), 0.0).astype(F32)
            score = score + r * wt[hx:hx + 1, :]
        bits = lax.bitcast_convert_type(score, I32)
        key = jnp.where(bits < 0, INT_MIN - bits, bits)
        if masked:
            kpos = lax.broadcasted_iota(I32, (bk, bq), 0) + (kb * bk - qb * bq)
            qpos = lax.broadcasted_iota(I32, (bk, bq), 1)
            key = jnp.where(kpos <= qpos, key, INT_MIN)
        key_ref[kb] = key
        hi_ref[kb] = lax.shift_right_arithmetic(key, 16).astype(I16)
        lo_ref[kb] = ((key & (2 * HALF_BIAS - 1)) - HALF_BIAS).astype(I16)

    def score_loop_body(kb, _):
        score_block(kb, False)
        return 0

    lax.fori_loop(0, nkb - 1, score_loop_body, 0)
    score_block(nkb - 1, True)

    q_t = qb * bq + lax.broadcasted_iota(I32, (1, bq), 1)
    k_eff = jnp.minimum(topk, q_t + 1).astype(F32)

    def count_ge(thr):
        def body(kb, acc):
            for c in range(bk // COUNT_ROWS):
                keys = key_ref[kb, c * COUNT_ROWS:(c + 1) * COUNT_ROWS, :]
                acc = acc + _fold_rows(jnp.where(keys >= thr, 1.0, 0.0), 8)
            return acc
        acc = lax.fori_loop(0, nkb, body, jnp.zeros((8, bq), F32))
        return jnp.sum(acc, axis=0, keepdims=True)

    def count16_ge(ref, thr16):
        one, zero = jnp.int16(1), jnp.int16(0)

        def body(kb, acc):
            for c in range(bk // COUNT_ROWS16):
                half = ref[kb, c * COUNT_ROWS16:(c + 1) * COUNT_ROWS16, :]
                acc = acc + _fold_rows(jnp.where(half >= thr16, one, zero), I16_ROWS)
            return acc
        acc = lax.fori_loop(0, nkb, body, jnp.zeros((I16_ROWS, bq), I16))
        return jnp.sum(acc.astype(F32), axis=0, keepdims=True)

    def kth_largest16(ref, kth):
        def bit_step(i, biased):
            cand = biased | lax.shift_left(jnp.int32(1), 15 - i)
            cnt = count16_ge(ref, (cand - HALF_BIAS).astype(I16))
            return jnp.where(cnt >= kth, cand, biased)
        return lax.fori_loop(0, 16, bit_step, jnp.zeros((1, bq), I32)) - HALF_BIAS

    t_hi = kth_largest16(hi_ref, k_eff)
    above = jnp.where(t_hi >= HALF_BIAS - 1, 0.0,
                      count16_ge(hi_ref, jnp.minimum(t_hi + 1, HALF_BIAS - 1).astype(I16)))
    t_hi16 = t_hi.astype(I16)

    def keep_group(kb, _):
        lo_ref[kb] = jnp.where(hi_ref[kb] == t_hi16, lo_ref[kb], jnp.int16(-HALF_BIAS))
        return 0

    lax.fori_loop(0, nkb, keep_group, 0)
    t_lo = kth_largest16(lo_ref, k_eff - above)
    thr = t_hi * (2 * HALF_BIAS) + (t_lo + HALF_BIAS)
    need = k_eff - count_ge(thr + 1)

    rs = lax.broadcasted_iota(I32, (sub, sub), 0)
    cs = lax.broadcasted_iota(I32, (sub, sub), 1)
    earlier = (cs < rs).astype(BF16)

    def mask_block(kb, seen):
        for j in range(bk // sub):
            key = key_ref[kb, j * sub:(j + 1) * sub, :]
            eq = jnp.where(key == thr, 1.0, 0.0)
            rank = _nn(earlier, eq.astype(BF16)) + seen
            take = jnp.where(rank < need, 1, 0)
            sel = (key + take) > thr
            key_ref[kb, j * sub:(j + 1) * sub, :] = lax.bitcast_convert_type(jnp.where(sel, 0.0, NEG_BIG), I32)
            seen = seen + jnp.sum(eq, axis=0, keepdims=True)
        return seen

    lax.fori_loop(0, nkb, mask_block, jnp.zeros((1, bq), F32))

    qd = qd_ref[0]

    def pair_bias(kb):
        bias = lax.bitcast_convert_type(key_ref[kb], F32)
        return jnp.concatenate([bias, bias], axis=1)

    for hp in range(n_heads // 2):
        chunk = qd[:, hp * LANES:(hp + 1) * LANES]
        zero = jnp.zeros_like(chunk)
        qq = jnp.concatenate([jnp.where(lo_half, chunk, zero), jnp.where(hi_half, chunk, zero)], axis=0)

        _, acc = _flash_pipelined_t(qq, kd_ref, lambda kb: vt_ref[0, kb], s_ref, nkb, bk, pair_bias, pair_bias)
        o = acc[:HEAD_DIM] / acc[HEAD_DIM:HEAD_DIM + 1]
        pair = jnp.concatenate([o[:, :bq], o[:, bq:]], axis=0)
        o_ref[0, :, hp * LANES:(hp + 1) * LANES] = pair.T.astype(o_ref.dtype)


def _dsa_attention(qd, qx, wt, kd2, kx2, vt2, blk, bk, topk):
    bsz, s, wq = qd.shape
    kspec = pl.BlockSpec((1, s, LANES), lambda b, i: (b, 0, 0), pipeline_mode=pl.Buffered(1))
    tiles = (s // bk, bk, blk)
    return pl.pallas_call(
        functools.partial(_dsa_kernel, bq=blk, bk=bk, sub=min(bk, 256), topk=topk),
        grid=(bsz, s // blk),
        in_specs=[pl.BlockSpec((1, blk, wq), lambda b, i: (b, i, 0)),
                  pl.BlockSpec((1, blk, qx.shape[2]), lambda b, i: (b, i, 0)),
                  pl.BlockSpec((1, wt.shape[1], blk), lambda b, i: (b, 0, i)),
                  kspec, kspec,
                  pl.BlockSpec((1, s // bk, LANES, bk), lambda b, i: (b, 0, 0, 0), pipeline_mode=pl.Buffered(1))],
        out_specs=pl.BlockSpec((1, blk, wq), lambda b, i: (b, i, 0)),
        out_shape=jax.ShapeDtypeStruct((bsz, s, wq), BF16),
        scratch_shapes=[pltpu.VMEM(tiles, I32), pltpu.VMEM(tiles, I16), pltpu.VMEM(tiles, I16),
                        pltpu.VMEM((2, bk, 2 * blk), F32)],
        name="dsa_attention",
        compiler_params=_cparams(("arbitrary", "arbitrary")),
    )(qd, qx, wt, kd2, kx2, vt2)


def _diff_kernel(q_ref, k_ref, vt_ref, lam_ref, g_ref, o_ref, s_ref, *, bq, bk, lambda_init):
    qi = pl.program_id(2)
    q = q_ref[0]
    lo_half, hi_half = _lane_half_masks()
    zero = jnp.zeros_like(q)
    qq = jnp.concatenate([jnp.where(lo_half, q, zero), jnp.where(hi_half, q, zero)], axis=0)
    n_full = (qi * bq) // bk

    def qk(kb, slot):
        off = pl.multiple_of(kb * bk, bk)
        s_ref[slot] = _nt(k_ref[0, pl.ds(off, bk), :], qq)

    def consume(kb, slot, carry):
        return _flash_update_t(s_ref[slot], *carry, vt_ref[0, 0, kb])

    def consume_diag(n_sub, carry):
        qk(0, 0)
        nk = n_sub * bq
        s = s_ref[1, :nk, :]
        s_last = jnp.where(_tri_mask(bq, bq, 2, strict=False), s[nk - bq:], NEG_BIG)
        s = s_last if n_sub == 1 else jnp.concatenate([s[:nk - bq], s_last], axis=0)
        return _flash_update_t(s, *carry, vt_ref[0, 0, n_full, :, :nk])

    def pair(p, carry):
        qk(2 * p + 1, 1)
        carry = consume(2 * p, 0, carry)
        qk(2 * p + 2, 0)
        return consume(2 * p + 1, 1, carry)

    qk(n_full, 1)
    diag_sub = (qi * bq - n_full * bk) // bq
    carry = _switch(diag_sub, [functools.partial(consume_diag, v + 1) for v in range(bk // bq)],
                    _flash_init(vt_ref.shape[3], 2 * bq))
    carry = lax.fori_loop(0, n_full // 2, pair, carry)
    _, acc = lax.cond(n_full % 2 == 1, lambda cr: consume(n_full - 1, 0, cr), lambda cr: cr, carry)
    l = acc[LANES:LANES + 1]
    acc = acc[:LANES]

    lp = lam_ref[...]
    lam = (jnp.exp(jnp.sum(lp[0:1] * lp[1:2], axis=1, keepdims=True))
           - jnp.exp(jnp.sum(lp[2:3] * lp[3:4], axis=1, keepdims=True)) + lambda_init)
    o = acc[:, :bq] / l[:, :bq] - lam * (acc[:, bq:] / l[:, bq:])
    ms = jnp.mean(o * o, axis=0, keepdims=True)
    o = o * lax.rsqrt(ms + LN_EPS) * g_ref[...] * (1.0 - lambda_init)
    o_ref[0] = o.T.astype(o_ref.dtype)


def _diff_attention(q, k, vt, lam_params, subln_g, blk, bk, lambda_init):
    bsz, s, w = q.shape
    return pl.pallas_call(
        functools.partial(_diff_kernel, bq=blk, bk=bk, lambda_init=lambda_init),
        grid=(bsz, w // LANES, s // blk),
        in_specs=[pl.BlockSpec((1, blk, LANES), lambda b, h, i: (b, i, h)),
                  pl.BlockSpec((1, s, LANES), lambda b, h, i: (b, 0, h)),
                  pl.BlockSpec((1, 1) + vt.shape[2:], lambda b, h, i: (b, h, 0, 0, 0)),
                  pl.BlockSpec(lam_params.shape, lambda b, h, i: (0, 0)),
                  pl.BlockSpec((LANES, 1), lambda b, h, i: (0, 0))],
        out_specs=pl.BlockSpec((1, blk, LANES), lambda b, h, i: (b, i, h)),
        out_shape=jax.ShapeDtypeStruct((bsz, s, w), BF16),
        scratch_shapes=[pltpu.VMEM((2, bk, 2 * blk), F32)],
        name="diff_attention",
        compiler_params=_cparams(("arbitrary", "arbitrary", "arbitrary")),
    )(q, k, vt, lam_params, subln_g.reshape(LANES, 1))


def _pick(n, prefs):
    for p in prefs:
        if n % p == 0:
            return p
    return n


def kernel(x, c, positions, w_mod, b_mod, w_in_even, w_out_even, w_in_odd, lam_q1, lam_k1, lam_q2, lam_k2,
           subln_g, w_out_odd, ln_mix_g, ln_mix_b, w_gate, w_up, w_down, ln_ffn_g, ln_ffn_b):
    bsz, s, d = x.shape
    depth = w_mod.shape[0]
    alpha = (2 * depth) ** 0.25
    dff = w_gate.shape[2]
    rows = bsz * s
    tm = _pick(s, (512, 256, 128))
    tf = _pick(dff, (1408, 1024, 512, 256, 128))
    blk = _pick(s, (256, 128))
    bk_dsa = _pick(s, (1024, 512, 256))
    bk_diff = _pick(s, (1024, 512, 256))
    scale = HEAD_DIM ** -0.5

    inv = ROPE_THETA ** (-jnp.arange(0, HEAD_DIM, 2, dtype=F32) / HEAD_DIM)
    ang = positions.astype(F32)[..., None] * inv
    cos, sin = jnp.cos(ang).reshape(rows, -1), jnp.sin(ang).reshape(rows, -1)
    cos128 = jnp.concatenate([cos] * 4, axis=1)
    sg128 = jnp.concatenate([-sin, sin, -sin, sin], axis=1)

    mod = _modulation(c, w_mod.astype(BF16), b_mod)
    xf = x.reshape(rows, d)

    for i in range(depth):
        mod_l = mod[i].reshape(bsz, 6, d)
        if i % 2 == 0:
            w = w_in_even[i // 2]
            n_sb = n_dsa = d // (2 * HEAD_DIM)
            sbw, dsw, ixw = n_sb * HEAD_DIM, n_dsa * HEAD_DIM, N_IDX_HEADS * HEAD_DIM
            offs = [0]
            for width in (sbw, sbw, sbw, dsw, HEAD_DIM, HEAD_DIM, ixw, HEAD_DIM, N_IDX_HEADS):
                offs.append(offs[-1] + width)
            col = lambda j: w[:, offs[j]:offs[j + 1]]
            pad = jnp.zeros((d, LANES - N_IDX_HEADS), w.dtype)
            wp = jnp.concatenate([col(0), col(1), col(2), col(5), col(5), col(8), pad], axis=1)
            wr = jnp.concatenate([col(3), col(6), col(4), col(4), col(7), col(7)], axis=1)
            plain_outs = ((0, sbw, scale), (sbw, sbw, 1.0), (2 * sbw, sbw, 1.0),
                          (3 * sbw, LANES, 1.0), (3 * sbw + LANES, LANES, N_IDX_HEADS ** -0.5))
            rope_outs = ((0, dsw, scale), (dsw, ixw, scale), (dsw + ixw, LANES, 1.0),
                         (dsw + ixw + LANES, LANES, 1.0))
            dts = (BF16, BF16, BF16, BF16, F32, BF16, BF16, BF16, BF16)
            q_sb, k_sb, v_sb, vd2, wx, qd, qx, kd2, kx2 = _inproj(
                xf, mod_l, cos128, sg128, wp.astype(BF16), wr.astype(BF16), _rot_partner(wr).astype(BF16),
                plain_outs, rope_outs, dts, s, tm)
            r3 = lambda t: t.reshape(bsz, s, t.shape[1])
            vt_sb = v_sb.reshape(bsz, s // blk, blk, sbw // LANES, LANES).transpose(0, 3, 1, 4, 2)
            o_sb = _sb_attention(r3(q_sb), r3(k_sb), vt_sb, blk)
            wt = r3(wx)[:, :, :8].swapaxes(1, 2)
            vt2 = vd2.reshape(bsz, s // bk_dsa, bk_dsa, LANES).swapaxes(2, 3)
            vt2 = jnp.concatenate([vt2[:, :, :HEAD_DIM], jnp.ones_like(vt2[:, :, HEAD_DIM:])], axis=2)
            o_dsa = _dsa_attention(r3(qd), r3(qx), wt, r3(kd2), r3(kx2), vt2, blk, bk_dsa,
                                   min(DSA_TOPK_MAX, s // 4))
            w_out = w_out_even[i // 2].astype(BF16)
            o_list = [o_sb.reshape(rows, sbw), o_dsa.reshape(rows, dsw)]
            w_list = [w_out[:sbw], w_out[sbw:]]
        else:
            j = i // 2
            w = w_in_odd[j]
            dw = w.shape[1] // 3
            lambda_init = 0.8 - 0.6 * math.exp(-0.3 * i)
            wp = w[:, 2 * dw:]
            wr = w[:, :2 * dw]
            plain_outs = ((0, dw, 1.0),)
            rope_outs = ((0, dw, scale), (dw, dw, 1.0))
            v_df, q_df, k_df = _inproj(
                xf, mod_l, cos128, sg128, wp.astype(BF16), wr.astype(BF16), _rot_partner(wr).astype(BF16),
                plain_outs, rope_outs, (BF16, BF16, BF16), s, tm)
            r3 = lambda t: t.reshape(bsz, s, t.shape[1])
            lam_params = jnp.stack([lam_q1[j], lam_k1[j], lam_q2[j], lam_k2[j]]).astype(F32)
            vt = v_df.reshape(bsz, s // bk_diff, bk_diff, dw // LANES, LANES).transpose(0, 3, 1, 4, 2)
            vt = jnp.concatenate([vt, jnp.ones_like(vt[:, :, :, :ONES_ROWS])], axis=3)
            o_df = _diff_attention(r3(q_df), r3(k_df), vt, lam_params, subln_g[j].astype(F32), blk, bk_diff,
                                   lambda_init)
            o_list = [o_df.reshape(rows, dw)]
            w_list = [w_out_odd[j].astype(BF16)]
        xf = _outproj_ln(o_list, w_list, xf, mod_l, ln_mix_g[i], ln_mix_b[i], alpha, s, tm)
        xf = _ffn_ln(xf, mod_l, w_gate[i].astype(BF16), w_up[i].astype(BF16), w_down[i].astype(BF16),
                     ln_ffn_g[i], ln_ffn_b[i], alpha, s, tm, tf)
    return xf.reshape(bsz, s, d)
```

```python
import functools
import math

import jax
import jax.numpy as jnp
from jax import lax
from jax.experimental import pallas as pl
from jax.experimental.pallas import tpu as pltpu

HEAD_DIM = 64
N_IDX_HEADS = 4
DSA_TOPK_MAX = 256
DSA_HEADS_PER_PASS = 4
ROPE_THETA = 10000.0
LN_EPS = 1e-5
LANES = 128
NEG_BIG = -1e30
INT_MIN = -2 ** 31
LOG2E = 1.4426950408889634
SB_EXP_FLOOR = 105.0
ONES_ROWS = 16
COUNT_ROWS = 64
COUNT_ROWS16 = 128
I16_ROWS = 16
HALF_BIAS = 2 ** 15

F32 = jnp.float32
BF16 = jnp.bfloat16
I32 = jnp.int32
I16 = jnp.int16

VMEM_LIMIT = 56 * 1024 * 1024


def _cparams(sem):
    return pltpu.CompilerParams(dimension_semantics=sem, vmem_limit_bytes=VMEM_LIMIT)


def _nt(a, b):
    return lax.dot_general(a, b, (((1,), (1,)), ((), ())), preferred_element_type=F32)


def _nn(a, b):
    return jnp.dot(a, b, preferred_element_type=F32)


def _layer_norm(v, g, b):
    mu = jnp.mean(v, axis=-1, keepdims=True)
    d = v - mu
    var = jnp.mean(d * d, axis=-1, keepdims=True)
    return d * lax.rsqrt(var + LN_EPS) * g + b


def _mod_kernel(c_ref, w_ref, b_ref, o_ref):
    c = c_ref[...]
    a = (c * jax.nn.sigmoid(c)).astype(BF16)
    o_ref[0] = _nn(a, w_ref[0]) + b_ref[0]


def _modulation(c, w_mod_bf, b_mod):
    depth, d, n = w_mod_bf.shape
    bsz = c.shape[0]
    tn = n // 4
    return pl.pallas_call(
        _mod_kernel,
        grid=(depth, n // tn),
        in_specs=[pl.BlockSpec((bsz, d), lambda l, j: (0, 0)),
                  pl.BlockSpec((1, d, tn), lambda l, j: (l, 0, j)),
                  pl.BlockSpec((1, 1, tn), lambda l, j: (l, 0, j))],
        out_specs=pl.BlockSpec((1, bsz, tn), lambda l, j: (l, 0, j)),
        out_shape=jax.ShapeDtypeStruct((depth, bsz, n), F32),
        name="modulation",
        compiler_params=_cparams(("arbitrary", "arbitrary")),
    )(c, w_mod_bf, b_mod.reshape(depth, 1, n))


def _inproj_kernel(x_ref, mod_ref, cos_ref, sg_ref, wp_ref, wr_ref, wrr_ref, *out_refs,
                   plain_outs, rope_outs):
    x = x_ref[...]
    sh = mod_ref[0, 0:1, :]
    sc = mod_ref[0, 1:2, :]
    h = (x * (1.0 + sc) + sh).astype(BF16)
    plain = _nn(h, wp_ref[...])
    rope = _nn(h, wr_ref[...])
    rot = _nn(h, wrr_ref[...])
    cos = cos_ref[...]
    sg = sg_ref[...]
    n = 0
    for (c0, width, scale) in plain_outs:
        o = out_refs[n]
        v = plain[:, c0:c0 + width]
        if scale != 1.0:
            v = v * scale
        if o.dtype == F32:
            v = v.astype(BF16).astype(F32)
        o[...] = v.astype(o.dtype)
        n += 1
    for (c0, width, scale) in rope_outs:
        o = out_refs[n]
        for j in range(width // LANES):
            sl = slice(c0 + j * LANES, c0 + (j + 1) * LANES)
            v = rope[:, sl] * cos + rot[:, sl] * sg
            if scale != 1.0:
                v = v * scale
            o[:, j * LANES:(j + 1) * LANES] = v.astype(o.dtype)
        n += 1


def _inproj(xf, mod_l, cos128, sg128, wp, wr, wrr, plain_outs, rope_outs, out_dtypes, rows_per_batch, tm):
    r, d = xf.shape
    tiles_per_batch = rows_per_batch // tm
    widths = [w for (_, w, _) in plain_outs] + [w for (_, w, _) in rope_outs]
    row = lambda i: (i, 0)
    const = lambda i: (0, 0)
    return pl.pallas_call(
        functools.partial(_inproj_kernel, plain_outs=plain_outs, rope_outs=rope_outs),
        grid=(r // tm,),
        in_specs=[pl.BlockSpec((tm, d), row),
                  pl.BlockSpec((1, 6, d), lambda i: (i // tiles_per_batch, 0, 0)),
                  pl.BlockSpec((tm, LANES), row),
                  pl.BlockSpec((tm, LANES), row),
                  pl.BlockSpec(wp.shape, const),
                  pl.BlockSpec(wr.shape, const),
                  pl.BlockSpec(wrr.shape, const)],
        out_specs=[pl.BlockSpec((tm, w), row) for w in widths],
        out_shape=[jax.ShapeDtypeStruct((r, w), dt) for w, dt in zip(widths, out_dtypes)],
        name="inproj",
        compiler_params=_cparams(("arbitrary",)),
    )(xf, mod_l, cos128, sg128, wp, wr, wrr)


def _rot_partner(w):
    d, n = w.shape
    return w.reshape(d, n // HEAD_DIM, 2, HEAD_DIM // 2)[:, :, ::-1, :].reshape(d, n)


def _outproj_kernel(*refs, n_in, alpha):
    o_refs = refs[:n_in]
    w_refs = refs[n_in:2 * n_in]
    x_ref, mod_ref, g_ref, b_ref, out_ref = refs[2 * n_in:]
    y = _nn(o_refs[0][...], w_refs[0][...])
    for a, w in zip(o_refs[1:], w_refs[1:]):
        y = y + _nn(a[...], w[...])
    gate = mod_ref[0, 2:3, :]
    v = alpha * x_ref[...] + (1.0 + gate) * y
    out_ref[...] = _layer_norm(v, g_ref[...], b_ref[...])


def _outproj_ln(o_list, w_list, xf, mod_l, g, b, alpha, rows_per_batch, tm):
    r, d = xf.shape
    tiles_per_batch = rows_per_batch // tm
    row = lambda i: (i, 0)
    const = lambda i: (0, 0)
    n_in = len(o_list)
    return pl.pallas_call(
        functools.partial(_outproj_kernel, n_in=n_in, alpha=alpha),
        grid=(r // tm,),
        in_specs=([pl.BlockSpec((tm, o.shape[1]), row) for o in o_list]
                  + [pl.BlockSpec(w.shape, const) for w in w_list]
                  + [pl.BlockSpec((tm, d), row),
                     pl.BlockSpec((1, 6, d), lambda i: (i // tiles_per_batch, 0, 0)),
                     pl.BlockSpec((1, d), const),
                     pl.BlockSpec((1, d), const)]),
        out_specs=pl.BlockSpec((tm, d), row),
        out_shape=jax.ShapeDtypeStruct((r, d), F32),
        name="outproj_ln",
        compiler_params=_cparams(("arbitrary",)),
    )(*o_list, *w_list, xf, mod_l, g.reshape(1, d), b.reshape(1, d))


def _ffn_kernel(x_ref, mod_ref, wg_ref, wu_ref, wd_ref, g_ref, b_ref, out_ref, h_ref, acc_ref, *, alpha):
    f = pl.program_id(1)

    @pl.when(f == 0)
    def _():
        sh = mod_ref[0, 3:4, :]
        sc = mod_ref[0, 4:5, :]
        h_ref[...] = (x_ref[...] * (1.0 + sc) + sh).astype(BF16)
        acc_ref[...] = jnp.zeros_like(acc_ref)

    h = h_ref[...]
    gate = _nn(h, wg_ref[...])
    up = _nn(h, wu_ref[...])
    a = (gate * jax.nn.sigmoid(gate) * up).astype(BF16)
    acc_ref[...] += _nn(a, wd_ref[...])

    @pl.when(f == pl.num_programs(1) - 1)
    def _():
        gf = mod_ref[0, 5:6, :]
        v = alpha * x_ref[...] + (1.0 + gf) * acc_ref[...]
        out_ref[...] = _layer_norm(v, g_ref[...], b_ref[...])


def _ffn_ln(xf, mod_l, wg, wu, wd, g, b, alpha, rows_per_batch, tm, tf):
    r, d = xf.shape
    dff = wg.shape[1]
    tiles_per_batch = rows_per_batch // tm
    return pl.pallas_call(
        functools.partial(_ffn_kernel, alpha=alpha),
        grid=(r // tm, dff // tf),
        in_specs=[pl.BlockSpec((tm, d), lambda i, f: (i, 0)),
                  pl.BlockSpec((1, 6, d), lambda i, f: (i // tiles_per_batch, 0, 0)),
                  pl.BlockSpec((d, tf), lambda i, f: (0, f)),
                  pl.BlockSpec((d, tf), lambda i, f: (0, f)),
                  pl.BlockSpec((tf, d), lambda i, f: (f, 0)),
                  pl.BlockSpec((1, d), lambda i, f: (0, 0)),
                  pl.BlockSpec((1, d), lambda i, f: (0, 0))],
        out_specs=pl.BlockSpec((tm, d), lambda i, f: (i, 0)),
        out_shape=jax.ShapeDtypeStruct((r, d), F32),
        scratch_shapes=[pltpu.VMEM((tm, d), BF16), pltpu.VMEM((tm, d), F32)],
        name="ffn_ln",
        compiler_params=_cparams(("arbitrary", "arbitrary")),
    )(xf, mod_l, wg, wu, wd, g.reshape(1, d), b.reshape(1, d))


def _lane_half_masks():
    lane = lax.broadcasted_iota(I32, (1, LANES), 1)
    return lane < HEAD_DIM, lane >= HEAD_DIM


def _tri_mask(n_keys, n_queries, reps, strict):
    kpos = lax.broadcasted_iota(I32, (n_keys, n_queries), 0)
    qpos = lax.broadcasted_iota(I32, (n_keys, n_queries), 1)
    tri = kpos < qpos if strict else kpos <= qpos
    return jnp.concatenate([tri] * reps, axis=1)


def _switch(index, branches, operand):
    def build(lo, hi):
        if hi - lo == 1:
            return branches[lo]
        mid = (lo + hi) // 2
        return lambda x: lax.cond(index < mid, build(lo, mid), build(mid, hi), x)
    return build(0, len(branches))(operand)


def _sb_kernel(q_ref, k_ref, vt_ref, o_ref, *, bq):
    qi = pl.program_id(2)
    q = q_ref[0]
    lo_half, hi_half = _lane_half_masks()
    zero = jnp.zeros_like(q)
    qq = jnp.concatenate([jnp.where(lo_half, q, zero), jnp.where(hi_half, q, zero)], axis=0)
    rs = lax.broadcasted_iota(I32, (bq, bq), 0)
    cs = lax.broadcasted_iota(I32, (bq, bq), 1)
    later = (cs > rs).astype(BF16)

    def block(kb, c, acc, diag):
        off = pl.multiple_of(kb * bq, bq)
        z = _nt(k_ref[0, pl.ds(off, bq), :], qq)
        sp = jnp.maximum(z, 0.0) + jnp.log(1.0 + jnp.exp2(jnp.abs(z) * -LOG2E))
        if diag:
            past = _tri_mask(bq, bq, 2, strict=True)
            sp = jnp.where(past, sp, 0.0)
        tail = _nn(later, sp.astype(BF16))
        w = jnp.exp(z - sp - tail - c)
        if diag:
            w = jnp.where(past, w, 0.0)
        acc = acc + _nn(vt_ref[0, 0, kb], w.astype(BF16))
        return c + tail[0:1] + sp[0:1], acc

    c, acc = block(qi, jnp.zeros((1, 2 * bq), F32), jnp.zeros((LANES, 2 * bq), F32), True)

    def more(state):
        kb, c_min, _, _ = state
        return jnp.logical_and(kb >= 0, c_min <= SB_EXP_FLOOR)

    def step(state):
        kb, _, c, acc = state
        c, acc = block(kb, c, acc, False)
        return kb - 1, jnp.min(c), c, acc

    _, _, _, acc = lax.while_loop(more, step, (qi - 1, jnp.min(c), c, acc))
    out_t = jnp.concatenate([acc[:HEAD_DIM, :bq], acc[HEAD_DIM:, bq:]], axis=0)
    o_ref[0] = out_t.T.astype(o_ref.dtype)


def _sb_attention(q, k, vt, blk):
    bsz, s, w = q.shape
    return pl.pallas_call(
        functools.partial(_sb_kernel, bq=blk),
        grid=(bsz, w // LANES, s // blk),
        in_specs=[pl.BlockSpec((1, blk, LANES), lambda b, h, i: (b, i, h)),
                  pl.BlockSpec((1, s, LANES), lambda b, h, i: (b, 0, h)),
                  pl.BlockSpec((1, 1) + vt.shape[2:], lambda b, h, i: (b, h, 0, 0, 0))],
        out_specs=pl.BlockSpec((1, blk, LANES), lambda b, h, i: (b, i, h)),
        out_shape=jax.ShapeDtypeStruct((bsz, s, w), BF16),
        name="sb_attention",
        compiler_params=_cparams(("arbitrary", "arbitrary", "arbitrary")),
    )(q, k, vt)


def _fold_rows(x, target):
    while x.shape[0] > target:
        h = x.shape[0] // 2
        x = x[:h] + x[h:]
    return x


def _flash_update_t(s, m, acc, vt):
    m_new = jnp.maximum(m, jnp.max(s, axis=0, keepdims=True))
    p = jnp.exp(s - m_new)
    acc = jnp.exp(m - m_new) * acc + _nn(vt, p.astype(BF16))
    return m_new, acc


def _flash_init(v_rows, r):
    return (jnp.full((1, r), NEG_BIG, F32), jnp.zeros((v_rows, r), F32))


def _flash_pipelined_t(qq, k_ref, vt_block, s_ref, n_blocks, bk, loop_bias, tail_bias):
    last = n_blocks - 1
    n_pairs = last // 2

    def qk(kb, slot):
        off = pl.multiple_of(kb * bk, bk)
        s_ref[slot] = _nt(k_ref[0, pl.ds(off, bk), :], qq)

    def consume(kb, slot, carry, bias_fn):
        s = s_ref[slot]
        if bias_fn is not None:
            s = s + bias_fn(kb)
        return _flash_update_t(s, *carry, vt_block(kb))

    def pair(p, carry):
        qk(2 * p + 1, 1)
        carry = consume(2 * p, 0, carry, loop_bias)
        qk(2 * p + 2, 0)
        return consume(2 * p + 1, 1, carry, loop_bias)

    qk(0, 0)
    carry = lax.fori_loop(0, n_pairs, pair, _flash_init(vt_block(0).shape[0], qq.shape[0]))
    kb = 2 * n_pairs
    qk(jnp.minimum(kb + 1, last), 1)
    carry = consume(kb, 0, carry, tail_bias)
    return lax.cond(kb < last, lambda cr: consume(kb + 1, 1, cr, tail_bias), lambda cr: cr, carry)


def _dsa_select(qx_ref, wt_ref, kx_ref, key_ref, hi_ref, lo_ref, *, qb, nkb, bq, bk, topk):
    halves = _lane_half_masks()

    qx = qx_ref[0]
    wt = wt_ref[0]
    qx_heads = []
    for hx in range(N_IDX_HEADS):
        chunk = qx[:, (hx // 2) * LANES:(hx // 2 + 1) * LANES]
        qx_heads.append(jnp.where(halves[hx % 2], chunk, jnp.zeros_like(chunk)))

    def score_block(kb, masked):
        off = pl.multiple_of(kb * bk, bk)
        kx = kx_ref[0, pl.ds(off, bk), :]
        score = jnp.zeros((bk, bq), F32)
        for hx in range(N_IDX_HEADS):
            r = jnp.maximum(_nt(kx, qx_heads[hx]).astype(BF16), 0.0).astype(F32)
            score = score + r * wt[hx:hx + 1, :]
        bits = lax.bitcast_convert_type(score, I32)
        key = jnp.where(bits < 0, INT_MIN - bits, bits)
        if masked:
            kpos = lax.broadcasted_iota(I32, (bk, bq), 0) + (kb * bk - qb * bq)
            qpos = lax.broadcasted_iota(I32, (bk, bq), 1)
            key = jnp.where(kpos <= qpos, key, INT_MIN)
        key_ref[kb] = key
        hi_ref[kb] = lax.shift_right_arithmetic(key, 16).astype(I16)
        lo_ref[kb] = ((key & (2 * HALF_BIAS - 1)) - HALF_BIAS).astype(I16)

    def score_loop_body(kb, _):
        score_block(kb, False)
        return 0

    lax.fori_loop(0, nkb - 1, score_loop_body, 0)
    score_block(nkb - 1, True)

    q_t = qb * bq + lax.broadcasted_iota(I32, (1, bq), 1)
    k_eff = jnp.minimum(topk, q_t + 1).astype(F32)

    def count_ge(thr):
        def body(kb, acc):
            for c in range(bk // COUNT_ROWS):
                keys = key_ref[kb, c * COUNT_ROWS:(c + 1) * COUNT_ROWS, :]
                acc = acc + _fold_rows(jnp.where(keys >= thr, 1.0, 0.0), 8)
            return acc
        acc = lax.fori_loop(0, nkb, body, jnp.zeros((8, bq), F32))
        return jnp.sum(acc, axis=0, keepdims=True)

    def count16_ge(ref, thr16):
        one, zero = jnp.int16(1), jnp.int16(0)

        def body(kb, acc):
            for c in range(bk // COUNT_ROWS16):
                half = ref[kb, c * COUNT_ROWS16:(c + 1) * COUNT_ROWS16, :]
                acc = acc + _fold_rows(jnp.where(half >= thr16, one, zero), I16_ROWS)
            return acc
        acc = lax.fori_loop(0, nkb, body, jnp.zeros((I16_ROWS, bq), I16))
        return jnp.sum(acc.astype(F32), axis=0, keepdims=True)

    def kth_largest16(ref, kth):
        def bit_step(i, biased):
            cand = biased | lax.shift_left(jnp.int32(1), 15 - i)
            cnt = count16_ge(ref, (cand - HALF_BIAS).astype(I16))
            return jnp.where(cnt >= kth, cand, biased)
        return lax.fori_loop(0, 16, bit_step, jnp.zeros((1, bq), I32)) - HALF_BIAS

    t_hi = kth_largest16(hi_ref, k_eff)
    above = jnp.where(t_hi >= HALF_BIAS - 1, 0.0,
                      count16_ge(hi_ref, jnp.minimum(t_hi + 1, HALF_BIAS - 1).astype(I16)))
    t_hi16 = t_hi.astype(I16)

    def keep_group(kb, _):
        lo_ref[kb] = jnp.where(hi_ref[kb] == t_hi16, lo_ref[kb], jnp.int16(-HALF_BIAS))
        return 0

    lax.fori_loop(0, nkb, keep_group, 0)
    t_lo = kth_largest16(lo_ref, k_eff - above)
    thr = t_hi * (2 * HALF_BIAS) + (t_lo + HALF_BIAS)
    need = k_eff - count_ge(thr + 1)
    return thr, need


def _dsa_mask(key_ref, thr, need, *, nkb, bq, bk, sub):
    rs = lax.broadcasted_iota(I32, (sub, sub), 0)
    cs = lax.broadcasted_iota(I32, (sub, sub), 1)
    earlier = (cs < rs).astype(BF16)

    def mask_block(kb, seen):
        for j in range(bk // sub):
            key = key_ref[kb, j * sub:(j + 1) * sub, :]
            eq = jnp.where(key == thr, 1.0, 0.0)
            rank = _nn(earlier, eq.astype(BF16)) + seen
            take = jnp.where(rank < need, 1, 0)
            sel = (key + take) > thr
            key_ref[kb, j * sub:(j + 1) * sub, :] = lax.bitcast_convert_type(jnp.where(sel, 0.0, NEG_BIG), I32)
            seen = seen + jnp.sum(eq, axis=0, keepdims=True)
        return seen

    lax.fori_loop(0, nkb, mask_block, jnp.zeros((1, bq), F32))


def _dsa_attend(qd_ref, kd_ref, vt_ref, o_ref, key_ref, s_ref, *, nkb, bq, bk):
    lo_half, hi_half = _lane_half_masks()
    n_heads = qd_ref.shape[2] // HEAD_DIM
    qd = qd_ref[0]
    pairs_per_pass = DSA_HEADS_PER_PASS // 2

    def stacked_bias(kb):
        bias = lax.bitcast_convert_type(key_ref[kb], F32)
        return jnp.concatenate([bias] * DSA_HEADS_PER_PASS, axis=1)

    for g in range(n_heads // DSA_HEADS_PER_PASS):
        stacked = []
        for hp in range(g * pairs_per_pass, (g + 1) * pairs_per_pass):
            chunk = qd[:, hp * LANES:(hp + 1) * LANES]
            zero = jnp.zeros_like(chunk)
            stacked += [jnp.where(lo_half, chunk, zero), jnp.where(hi_half, chunk, zero)]
        qq = jnp.concatenate(stacked, axis=0)

        _, acc = _flash_pipelined_t(qq, kd_ref, lambda kb: vt_ref[0, kb], s_ref, nkb, bk, stacked_bias, stacked_bias)
        o = acc[:HEAD_DIM] / acc[HEAD_DIM:HEAD_DIM + 1]
        for j in range(pairs_per_pass):
            hp = g * pairs_per_pass + j
            pair = jnp.concatenate([o[:, 2 * j * bq:(2 * j + 1) * bq], o[:, (2 * j + 1) * bq:(2 * j + 2) * bq]],
                                   axis=0)
            o_ref[0, :, hp * LANES:(hp + 1) * LANES] = pair.T.astype(o_ref.dtype)


def _dsa_kernel(qd_ref, qx_ref, wt_ref, kd_ref, kx_ref, vt_ref, o_ref, key_ref, *, bq, bk, sub, topk):
    qb = pl.program_id(1)
    nkb = (qb * bq) // bk + 1
    halves16 = pltpu.VMEM(key_ref.shape, I16)
    thr, need = pl.run_scoped(
        functools.partial(_dsa_select, qx_ref, wt_ref, kx_ref, key_ref, qb=qb, nkb=nkb, bq=bq, bk=bk, topk=topk),
        halves16, halves16)
    _dsa_mask(key_ref, thr, need, nkb=nkb, bq=bq, bk=bk, sub=sub)
    pl.run_scoped(functools.partial(_dsa_attend, qd_ref, kd_ref, vt_ref, o_ref, key_ref, nkb=nkb, bq=bq, bk=bk),
                  pltpu.VMEM((2, bk, DSA_HEADS_PER_PASS * bq), F32))


def _dsa_attention(qd, qx, wt, kd2, kx2, vt2, blk, bk, topk):
    bsz, s, wq = qd.shape
    kspec = pl.BlockSpec((1, s, LANES), lambda b, i: (b, 0, 0), pipeline_mode=pl.Buffered(1))
    tiles = (s // bk, bk, blk)
    return pl.pallas_call(
        functools.partial(_dsa_kernel, bq=blk, bk=bk, sub=min(bk, 256), topk=topk),
        grid=(bsz, s // blk),
        in_specs=[pl.BlockSpec((1, blk, wq), lambda b, i: (b, i, 0)),
                  pl.BlockSpec((1, blk, qx.shape[2]), lambda b, i: (b, i, 0)),
                  pl.BlockSpec((1, wt.shape[1], blk), lambda b, i: (b, 0, i)),
                  kspec, kspec,
                  pl.BlockSpec((1,) + vt2.shape[1:], lambda b, i: (b, 0, 0, 0), pipeline_mode=pl.Buffered(1))],
        out_specs=pl.BlockSpec((1, blk, wq), lambda b, i: (b, i, 0)),
        out_shape=jax.ShapeDtypeStruct((bsz, s, wq), BF16),
        scratch_shapes=[pltpu.VMEM(tiles, I32)],
        name="dsa_attention",
        compiler_params=_cparams(("arbitrary", "arbitrary")),
    )(qd, qx, wt, kd2, kx2, vt2)


def _diff_kernel(q_ref, k_ref, vt_ref, lam_ref, g_ref, o_ref, s_ref, *, bq, bk, lambda_init):
    qi = pl.program_id(2)
    q = q_ref[0]
    lo_half, hi_half = _lane_half_masks()
    zero = jnp.zeros_like(q)
    qq = jnp.concatenate([jnp.where(lo_half, q, zero), jnp.where(hi_half, q, zero)], axis=0)
    n_full = (qi * bq) // bk

    def qk(kb, slot):
        off = pl.multiple_of(kb * bk, bk)
        s_ref[slot] = _nt(k_ref[0, pl.ds(off, bk), :], qq)

    def consume(kb, slot, carry):
        return _flash_update_t(s_ref[slot], *carry, vt_ref[0, 0, kb])

    def consume_diag(n_sub, carry):
        qk(0, 0)
        nk = n_sub * bq
        s = s_ref[1, :nk, :]
        s_last = jnp.where(_tri_mask(bq, bq, 2, strict=False), s[nk - bq:], NEG_BIG)
        s = s_last if n_sub == 1 else jnp.concatenate([s[:nk - bq], s_last], axis=0)
        return _flash_update_t(s, *carry, vt_ref[0, 0, n_full, :, :nk])

    def pair(p, carry):
        qk(2 * p + 1, 1)
        carry = consume(2 * p, 0, carry)
        qk(2 * p + 2, 0)
        return consume(2 * p + 1, 1, carry)

    qk(n_full, 1)
    diag_sub = (qi * bq - n_full * bk) // bq
    carry = _switch(diag_sub, [functools.partial(consume_diag, v + 1) for v in range(bk // bq)],
                    _flash_init(vt_ref.shape[3], 2 * bq))
    carry = lax.fori_loop(0, n_full // 2, pair, carry)
    _, acc = lax.cond(n_full % 2 == 1, lambda cr: consume(n_full - 1, 0, cr), lambda cr: cr, carry)
    l = acc[LANES:LANES + 1]
    acc = acc[:LANES]

    lp = lam_ref[...]
    lam = (jnp.exp(jnp.sum(lp[0:1] * lp[1:2], axis=1, keepdims=True))
           - jnp.exp(jnp.sum(lp[2:3] * lp[3:4], axis=1, keepdims=True)) + lambda_init)
    o = acc[:, :bq] / l[:, :bq] - lam * (acc[:, bq:] / l[:, bq:])
    ms = jnp.mean(o * o, axis=0, keepdims=True)
    o = o * lax.rsqrt(ms + LN_EPS) * g_ref[...] * (1.0 - lambda_init)
    o_ref[0] = o.T.astype(o_ref.dtype)


def _diff_attention(q, k, vt, lam_params, subln_g, blk, bk, lambda_init):
    bsz, s, w = q.shape
    return pl.pallas_call(
        functools.partial(_diff_kernel, bq=blk, bk=bk, lambda_init=lambda_init),
        grid=(bsz, w // LANES, s // blk),
        in_specs=[pl.BlockSpec((1, blk, LANES), lambda b, h, i: (b, i, h)),
                  pl.BlockSpec((1, s, LANES), lambda b, h, i: (b, 0, h)),
                  pl.BlockSpec((1, 1) + vt.shape[2:], lambda b, h, i: (b, h, 0, 0, 0)),
                  pl.BlockSpec(lam_params.shape, lambda b, h, i: (0, 0)),
                  pl.BlockSpec((LANES, 1), lambda b, h, i: (0, 0))],
        out_specs=pl.BlockSpec((1, blk, LANES), lambda b, h, i: (b, i, h)),
        out_shape=jax.ShapeDtypeStruct((bsz, s, w), BF16),
        scratch_shapes=[pltpu.VMEM((2, bk, 2 * blk), F32)],
        name="diff_attention",
        compiler_params=_cparams(("arbitrary", "arbitrary", "arbitrary")),
    )(q, k, vt, lam_params, subln_g.reshape(LANES, 1))


def _pick(n, prefs):
    for p in prefs:
        if n % p == 0:
            return p
    return n


def kernel(x, c, positions, w_mod, b_mod, w_in_even, w_out_even, w_in_odd, lam_q1, lam_k1, lam_q2, lam_k2,
           subln_g, w_out_odd, ln_mix_g, ln_mix_b, w_gate, w_up, w_down, ln_ffn_g, ln_ffn_b):
    bsz, s, d = x.shape
    depth = w_mod.shape[0]
    alpha = (2 * depth) ** 0.25
    dff = w_gate.shape[2]
    rows = bsz * s
    tm = _pick(s, (512, 256, 128))
    tf = _pick(dff, (1408, 1024, 512, 256, 128))
    blk = _pick(s, (256, 128))
    bk_dsa = _pick(s, (1024, 512, 256))
    bk_diff = _pick(s, (1024, 512, 256))
    scale = HEAD_DIM ** -0.5

    inv = ROPE_THETA ** (-jnp.arange(0, HEAD_DIM, 2, dtype=F32) / HEAD_DIM)
    ang = positions.astype(F32)[..., None] * inv
    cos, sin = jnp.cos(ang).reshape(rows, -1), jnp.sin(ang).reshape(rows, -1)
    cos128 = jnp.concatenate([cos] * 4, axis=1)
    sg128 = jnp.concatenate([-sin, sin, -sin, sin], axis=1)

    mod = _modulation(c, w_mod.astype(BF16), b_mod)
    xf = x.reshape(rows, d)

    for i in range(depth):
        mod_l = mod[i].reshape(bsz, 6, d)
        if i % 2 == 0:
            w = w_in_even[i // 2]
            n_sb = n_dsa = d // (2 * HEAD_DIM)
            sbw, dsw, ixw = n_sb * HEAD_DIM, n_dsa * HEAD_DIM, N_IDX_HEADS * HEAD_DIM
            offs = [0]
            for width in (sbw, sbw, sbw, dsw, HEAD_DIM, HEAD_DIM, ixw, HEAD_DIM, N_IDX_HEADS):
                offs.append(offs[-1] + width)
            col = lambda j: w[:, offs[j]:offs[j + 1]]
            pad = jnp.zeros((d, LANES - N_IDX_HEADS), w.dtype)
            wp = jnp.concatenate([col(0), col(1), col(2), col(5), col(5), col(8), pad], axis=1)
            wr = jnp.concatenate([col(3), col(6), col(4), col(4), col(7), col(7)], axis=1)
            plain_outs = ((0, sbw, scale), (sbw, sbw, 1.0), (2 * sbw, sbw, 1.0),
                          (3 * sbw, LANES, 1.0), (3 * sbw + LANES, LANES, N_IDX_HEADS ** -0.5))
            rope_outs = ((0, dsw, scale), (dsw, ixw, scale), (dsw + ixw, LANES, 1.0),
                         (dsw + ixw + LANES, LANES, 1.0))
            dts = (BF16, BF16, BF16, BF16, F32, BF16, BF16, BF16, BF16)
            q_sb, k_sb, v_sb, vd2, wx, qd, qx, kd2, kx2 = _inproj(
                xf, mod_l, cos128, sg128, wp.astype(BF16), wr.astype(BF16), _rot_partner(wr).astype(BF16),
                plain_outs, rope_outs, dts, s, tm)
            r3 = lambda t: t.reshape(bsz, s, t.shape[1])
            vt_sb = v_sb.reshape(bsz, s // blk, blk, sbw // LANES, LANES).transpose(0, 3, 1, 4, 2)
            o_sb = _sb_attention(r3(q_sb), r3(k_sb), vt_sb, blk)
            wt = r3(wx)[:, :, :8].swapaxes(1, 2)
            vt2 = vd2.reshape(bsz, s // bk_dsa, bk_dsa, LANES).swapaxes(2, 3)
            vt2 = jnp.concatenate([vt2[:, :, :HEAD_DIM], jnp.ones_like(vt2[:, :, :ONES_ROWS])], axis=2)
            o_dsa = _dsa_attention(r3(qd), r3(qx), wt, r3(kd2), r3(kx2), vt2, blk, bk_dsa,
                                   min(DSA_TOPK_MAX, s // 4))
            w_out = w_out_even[i // 2].astype(BF16)
            o_list = [o_sb.reshape(rows, sbw), o_dsa.reshape(rows, dsw)]
            w_list = [w_out[:sbw], w_out[sbw:]]
        else:
            j = i // 2
            w = w_in_odd[j]
            dw = w.shape[1] // 3
            lambda_init = 0.8 - 0.6 * math.exp(-0.3 * i)
            wp = w[:, 2 * dw:]
            wr = w[:, :2 * dw]
            plain_outs = ((0, dw, 1.0),)
            rope_outs = ((0, dw, scale), (dw, dw, 1.0))
            v_df, q_df, k_df = _inproj(
                xf, mod_l, cos128, sg128, wp.astype(BF16), wr.astype(BF16), _rot_partner(wr).astype(BF16),
                plain_outs, rope_outs, (BF16, BF16, BF16), s, tm)
            r3 = lambda t: t.reshape(bsz, s, t.shape[1])
            lam_params = jnp.stack([lam_q1[j], lam_k1[j], lam_q2[j], lam_k2[j]]).astype(F32)
            vt = v_df.reshape(bsz, s // bk_diff, bk_diff, dw // LANES, LANES).transpose(0, 3, 1, 4, 2)
            vt = jnp.concatenate([vt, jnp.ones_like(vt[:, :, :, :ONES_ROWS])], axis=3)
            o_df = _diff_attention(r3(q_df), r3(k_df), vt, lam_params, subln_g[j].astype(F32), blk, bk_diff,
                                   lambda_init)
            o_list = [o_df.reshape(rows, dw)]
            w_list = [w_out_odd[j].astype(BF16)]
        xf = _outproj_ln(o_list, w_list, xf, mod_l, ln_mix_g[i], ln_mix_b[i], alpha, s, tm)
        xf = _ffn_ln(xf, mod_l, w_gate[i].astype(BF16), w_up[i].astype(BF16), w_down[i].astype(BF16),
                     ln_ffn_g[i], ln_ffn_b[i], alpha, s, tm, tf)
    return xf.reshape(bsz, s, d)
```

```python
import functools
import math

import jax
import jax.numpy as jnp
from jax import lax
from jax.experimental import pallas as pl
from jax.experimental.pallas import tpu as pltpu

HEAD_DIM = 64
N_IDX_HEADS = 4
DSA_TOPK_MAX = 256
DSA_HEADS_PER_PASS = 4
DIFF_HEADS_PER_STEP = 1
ROPE_THETA = 10000.0
LN_EPS = 1e-5
LANES = 128
NEG_BIG = -1e30
INT_MIN = -2 ** 31
LOG2E = 1.4426950408889634
SB_EXP_FLOOR = 105.0
ONES_ROWS = 16
COUNT_ROWS = 64
COUNT_ROWS16 = 128
I16_ROWS = 16
HALF_BIAS = 2 ** 15

F32 = jnp.float32
BF16 = jnp.bfloat16
I32 = jnp.int32
I16 = jnp.int16

VMEM_LIMIT = 56 * 1024 * 1024


def _cparams(sem):
    return pltpu.CompilerParams(dimension_semantics=sem, vmem_limit_bytes=VMEM_LIMIT)


def _nt(a, b):
    return lax.dot_general(a, b, (((1,), (1,)), ((), ())), preferred_element_type=F32)


def _nn(a, b):
    return jnp.dot(a, b, preferred_element_type=F32)


def _layer_norm(v, g, b):
    mu = jnp.mean(v, axis=-1, keepdims=True)
    d = v - mu
    var = jnp.mean(d * d, axis=-1, keepdims=True)
    return d * lax.rsqrt(var + LN_EPS) * g + b


def _mod_kernel(c_ref, w_ref, b_ref, o_ref):
    c = c_ref[...]
    a = (c * jax.nn.sigmoid(c)).astype(BF16)
    o_ref[0] = _nn(a, w_ref[0]) + b_ref[0]


def _modulation(c, w_mod_bf, b_mod):
    depth, d, n = w_mod_bf.shape
    bsz = c.shape[0]
    tn = n // 4
    return pl.pallas_call(
        _mod_kernel,
        grid=(depth, n // tn),
        in_specs=[pl.BlockSpec((bsz, d), lambda l, j: (0, 0)),
                  pl.BlockSpec((1, d, tn), lambda l, j: (l, 0, j)),
                  pl.BlockSpec((1, 1, tn), lambda l, j: (l, 0, j))],
        out_specs=pl.BlockSpec((1, bsz, tn), lambda l, j: (l, 0, j)),
        out_shape=jax.ShapeDtypeStruct((depth, bsz, n), F32),
        name="modulation",
        compiler_params=_cparams(("arbitrary", "arbitrary")),
    )(c, w_mod_bf, b_mod.reshape(depth, 1, n))


def _inproj_kernel(x_ref, mod_ref, cos_ref, sg_ref, wp_ref, wr_ref, wrr_ref, *out_refs,
                   plain_outs, rope_outs):
    x = x_ref[...]
    sh = mod_ref[0, 0:1, :]
    sc = mod_ref[0, 1:2, :]
    h = (x * (1.0 + sc) + sh).astype(BF16)
    plain = _nn(h, wp_ref[...])
    rope = _nn(h, wr_ref[...])
    rot = _nn(h, wrr_ref[...])
    cos = cos_ref[...]
    sg = sg_ref[...]
    n = 0
    for (c0, width, scale) in plain_outs:
        o = out_refs[n]
        v = plain[:, c0:c0 + width]
        if scale != 1.0:
            v = v * scale
        if o.dtype == F32:
            v = v.astype(BF16).astype(F32)
        o[...] = v.astype(o.dtype)
        n += 1
    for (c0, width, scale) in rope_outs:
        o = out_refs[n]
        for j in range(width // LANES):
            sl = slice(c0 + j * LANES, c0 + (j + 1) * LANES)
            v = rope[:, sl] * cos + rot[:, sl] * sg
            if scale != 1.0:
                v = v * scale
            o[:, j * LANES:(j + 1) * LANES] = v.astype(o.dtype)
        n += 1


def _inproj(xf, mod_l, cos128, sg128, wp, wr, wrr, plain_outs, rope_outs, out_dtypes, rows_per_batch, tm):
    r, d = xf.shape
    tiles_per_batch = rows_per_batch // tm
    widths = [w for (_, w, _) in plain_outs] + [w for (_, w, _) in rope_outs]
    row = lambda i: (i, 0)
    const = lambda i: (0, 0)
    return pl.pallas_call(
        functools.partial(_inproj_kernel, plain_outs=plain_outs, rope_outs=rope_outs),
        grid=(r // tm,),
        in_specs=[pl.BlockSpec((tm, d), row),
                  pl.BlockSpec((1, 6, d), lambda i: (i // tiles_per_batch, 0, 0)),
                  pl.BlockSpec((tm, LANES), row),
                  pl.BlockSpec((tm, LANES), row),
                  pl.BlockSpec(wp.shape, const),
                  pl.BlockSpec(wr.shape, const),
                  pl.BlockSpec(wrr.shape, const)],
        out_specs=[pl.BlockSpec((tm, w), row) for w in widths],
        out_shape=[jax.ShapeDtypeStruct((r, w), dt) for w, dt in zip(widths, out_dtypes)],
        name="inproj",
        compiler_params=_cparams(("arbitrary",)),
    )(xf, mod_l, cos128, sg128, wp, wr, wrr)


def _rot_partner(w):
    d, n = w.shape
    return w.reshape(d, n // HEAD_DIM, 2, HEAD_DIM // 2)[:, :, ::-1, :].reshape(d, n)


def _outproj_kernel(*refs, n_in, alpha):
    o_refs = refs[:n_in]
    w_refs = refs[n_in:2 * n_in]
    x_ref, mod_ref, g_ref, b_ref, out_ref = refs[2 * n_in:]
    y = _nn(o_refs[0][...], w_refs[0][...])
    for a, w in zip(o_refs[1:], w_refs[1:]):
        y = y + _nn(a[...], w[...])
    gate = mod_ref[0, 2:3, :]
    v = alpha * x_ref[...] + (1.0 + gate) * y
    out_ref[...] = _layer_norm(v, g_ref[...], b_ref[...])


def _outproj_ln(o_list, w_list, xf, mod_l, g, b, alpha, rows_per_batch, tm):
    r, d = xf.shape
    tiles_per_batch = rows_per_batch // tm
    row = lambda i: (i, 0)
    const = lambda i: (0, 0)
    n_in = len(o_list)
    return pl.pallas_call(
        functools.partial(_outproj_kernel, n_in=n_in, alpha=alpha),
        grid=(r // tm,),
        in_specs=([pl.BlockSpec((tm, o.shape[1]), row) for o in o_list]
                  + [pl.BlockSpec(w.shape, const) for w in w_list]
                  + [pl.BlockSpec((tm, d), row),
                     pl.BlockSpec((1, 6, d), lambda i: (i // tiles_per_batch, 0, 0)),
                     pl.BlockSpec((1, d), const),
                     pl.BlockSpec((1, d), const)]),
        out_specs=pl.BlockSpec((tm, d), row),
        out_shape=jax.ShapeDtypeStruct((r, d), F32),
        name="outproj_ln",
        compiler_params=_cparams(("arbitrary",)),
    )(*o_list, *w_list, xf, mod_l, g.reshape(1, d), b.reshape(1, d))


def _ffn_kernel(x_ref, mod_ref, wg_ref, wu_ref, wd_ref, g_ref, b_ref, out_ref, h_ref, acc_ref, *, alpha):
    f = pl.program_id(1)

    @pl.when(f == 0)
    def _():
        sh = mod_ref[0, 3:4, :]
        sc = mod_ref[0, 4:5, :]
        h_ref[...] = (x_ref[...] * (1.0 + sc) + sh).astype(BF16)
        acc_ref[...] = jnp.zeros_like(acc_ref)

    h = h_ref[...]
    gate = _nn(h, wg_ref[...])
    up = _nn(h, wu_ref[...])
    a = (gate * jax.nn.sigmoid(gate) * up).astype(BF16)
    acc_ref[...] += _nn(a, wd_ref[...])

    @pl.when(f == pl.num_programs(1) - 1)
    def _():
        gf = mod_ref[0, 5:6, :]
        v = alpha * x_ref[...] + (1.0 + gf) * acc_ref[...]
        out_ref[...] = _layer_norm(v, g_ref[...], b_ref[...])


def _ffn_ln(xf, mod_l, wg, wu, wd, g, b, alpha, rows_per_batch, tm, tf):
    r, d = xf.shape
    dff = wg.shape[1]
    tiles_per_batch = rows_per_batch // tm
    resident = pl.Buffered(1) if tf == dff else None
    return pl.pallas_call(
        functools.partial(_ffn_kernel, alpha=alpha),
        grid=(r // tm, dff // tf),
        in_specs=[pl.BlockSpec((tm, d), lambda i, f: (i, 0)),
                  pl.BlockSpec((1, 6, d), lambda i, f: (i // tiles_per_batch, 0, 0)),
                  pl.BlockSpec((d, tf), lambda i, f: (0, f), pipeline_mode=resident),
                  pl.BlockSpec((d, tf), lambda i, f: (0, f), pipeline_mode=resident),
                  pl.BlockSpec((tf, d), lambda i, f: (f, 0), pipeline_mode=resident),
                  pl.BlockSpec((1, d), lambda i, f: (0, 0)),
                  pl.BlockSpec((1, d), lambda i, f: (0, 0))],
        out_specs=pl.BlockSpec((tm, d), lambda i, f: (i, 0)),
        out_shape=jax.ShapeDtypeStruct((r, d), F32),
        scratch_shapes=[pltpu.VMEM((tm, d), BF16), pltpu.VMEM((tm, d), F32)],
        name="ffn_ln",
        compiler_params=_cparams(("arbitrary", "arbitrary")),
    )(xf, mod_l, wg, wu, wd, g.reshape(1, d), b.reshape(1, d))


def _lane_half_masks():
    lane = lax.broadcasted_iota(I32, (1, LANES), 1)
    return lane < HEAD_DIM, lane >= HEAD_DIM


def _tri_mask(n_keys, n_queries, reps, strict):
    kpos = lax.broadcasted_iota(I32, (n_keys, n_queries), 0)
    qpos = lax.broadcasted_iota(I32, (n_keys, n_queries), 1)
    tri = kpos < qpos if strict else kpos <= qpos
    return jnp.concatenate([tri] * reps, axis=1)


def _switch(index, branches, operand):
    def build(lo, hi):
        if hi - lo == 1:
            return branches[lo]
        mid = (lo + hi) // 2
        return lambda x: lax.cond(index < mid, build(lo, mid), build(mid, hi), x)
    return build(0, len(branches))(operand)


def _sb_kernel(q_ref, k_ref, vt_ref, o_ref, *, bq):
    qi = pl.program_id(2)
    q = q_ref[0]
    lo_half, hi_half = _lane_half_masks()
    zero = jnp.zeros_like(q)
    qq = jnp.concatenate([jnp.where(lo_half, q, zero), jnp.where(hi_half, q, zero)], axis=0)
    rs = lax.broadcasted_iota(I32, (bq, bq), 0)
    cs = lax.broadcasted_iota(I32, (bq, bq), 1)
    later = (cs > rs).astype(BF16)

    def block(kb, c, acc, diag):
        off = pl.multiple_of(kb * bq, bq)
        z = _nt(k_ref[0, pl.ds(off, bq), :], qq)
        sp = jnp.maximum(z, 0.0) + jnp.log(1.0 + jnp.exp2(jnp.abs(z) * -LOG2E))
        if diag:
            past = _tri_mask(bq, bq, 2, strict=True)
            sp = jnp.where(past, sp, 0.0)
        tail = _nn(later, sp.astype(BF16))
        w = jnp.exp(z - sp - tail - c)
        if diag:
            w = jnp.where(past, w, 0.0)
        acc = acc + _nn(vt_ref[0, 0, kb], w.astype(BF16))
        return c + tail[0:1] + sp[0:1], acc

    c, acc = block(qi, jnp.zeros((1, 2 * bq), F32), jnp.zeros((LANES, 2 * bq), F32), True)

    def more(state):
        kb, c_min, _, _ = state
        return jnp.logical_and(kb >= 0, c_min <= SB_EXP_FLOOR)

    def step(state):
        kb, _, c, acc = state
        c, acc = block(kb, c, acc, False)
        return kb - 1, jnp.min(c), c, acc

    _, _, _, acc = lax.while_loop(more, step, (qi - 1, jnp.min(c), c, acc))
    out_t = jnp.concatenate([acc[:HEAD_DIM, :bq], acc[HEAD_DIM:, bq:]], axis=0)
    o_ref[0] = out_t.T.astype(o_ref.dtype)


def _sb_attention(q, k, vt, blk):
    bsz, s, w = q.shape
    return pl.pallas_call(
        functools.partial(_sb_kernel, bq=blk),
        grid=(bsz, w // LANES, s // blk),
        in_specs=[pl.BlockSpec((1, blk, LANES), lambda b, h, i: (b, i, h)),
                  pl.BlockSpec((1, s, LANES), lambda b, h, i: (b, 0, h)),
                  pl.BlockSpec((1, 1) + vt.shape[2:], lambda b, h, i: (b, h, 0, 0, 0))],
        out_specs=pl.BlockSpec((1, blk, LANES), lambda b, h, i: (b, i, h)),
        out_shape=jax.ShapeDtypeStruct((bsz, s, w), BF16),
        name="sb_attention",
        compiler_params=_cparams(("arbitrary", "arbitrary", "arbitrary")),
    )(q, k, vt)


def _fold_rows(x, target):
    while x.shape[0] > target:
        h = x.shape[0] // 2
        x = x[:h] + x[h:]
    return x


def _flash_update_t(s, m, acc, vt):
    m_new = jnp.maximum(m, jnp.max(s, axis=0, keepdims=True))
    p = jnp.exp(s - m_new).astype(BF16)
    if isinstance(vt, (list, tuple)):
        w = p.shape[1] // len(vt)
        pv = jnp.concatenate([_nn(v, p[:, g * w:(g + 1) * w]) for g, v in enumerate(vt)], axis=1)
    else:
        pv = _nn(vt, p)
    return m_new, jnp.exp(m - m_new) * acc + pv


def _flash_init(v_rows, r):
    return (jnp.full((1, r), NEG_BIG, F32), jnp.zeros((v_rows, r), F32))


def _flash_pipelined_t(qq, k_ref, vt_block, s_ref, n_blocks, bk, loop_bias, tail_bias):
    last = n_blocks - 1
    n_pairs = last // 2

    def qk(kb, slot):
        off = pl.multiple_of(kb * bk, bk)
        s_ref[slot] = _nt(k_ref[0, pl.ds(off, bk), :], qq)

    def consume(kb, slot, carry, bias_fn):
        s = s_ref[slot]
        if bias_fn is not None:
            s = s + bias_fn(kb)
        return _flash_update_t(s, *carry, vt_block(kb))

    def pair(p, carry):
        qk(2 * p + 1, 1)
        carry = consume(2 * p, 0, carry, loop_bias)
        qk(2 * p + 2, 0)
        return consume(2 * p + 1, 1, carry, loop_bias)

    qk(0, 0)
    carry = lax.fori_loop(0, n_pairs, pair, _flash_init(vt_block(0).shape[0], qq.shape[0]))
    kb = 2 * n_pairs
    qk(jnp.minimum(kb + 1, last), 1)
    carry = consume(kb, 0, carry, tail_bias)
    return lax.cond(kb < last, lambda cr: consume(kb + 1, 1, cr, tail_bias), lambda cr: cr, carry)


def _dsa_select(qx_ref, wt_ref, kx_ref, key_ref, hi_ref, lo_ref, *, qb, nkb, bq, bk, topk):
    halves = _lane_half_masks()

    qx = qx_ref[0]
    wt = wt_ref[0]
    qx_heads = []
    for hx in range(N_IDX_HEADS):
        chunk = qx[:, (hx // 2) * LANES:(hx // 2 + 1) * LANES]
        qx_heads.append(jnp.where(halves[hx % 2], chunk, jnp.zeros_like(chunk)))

    def score_block(kb, masked):
        off = pl.multiple_of(kb * bk, bk)
        kx = kx_ref[0, pl.ds(off, bk), :]
        score = jnp.zeros((bk, bq), F32)
        for hx in range(N_IDX_HEADS):
            r = jnp.maximum(_nt(kx, qx_heads[hx]).astype(BF16), 0.0).astype(F32)
            score = score + r * wt[hx:hx + 1, :]
        bits = lax.bitcast_convert_type(score, I32)
        key = jnp.where(bits < 0, INT_MIN - bits, bits)
        if masked:
            kpos = lax.broadcasted_iota(I32, (bk, bq), 0) + (kb * bk - qb * bq)
            qpos = lax.broadcasted_iota(I32, (bk, bq), 1)
            key = jnp.where(kpos <= qpos, key, INT_MIN)
        key_ref[kb] = key
        hi_ref[kb] = lax.shift_right_arithmetic(key, 16).astype(I16)
        lo_ref[kb] = ((key & (2 * HALF_BIAS - 1)) - HALF_BIAS).astype(I16)

    def score_loop_body(kb, _):
        score_block(kb, False)
        return 0

    lax.fori_loop(0, nkb - 1, score_loop_body, 0)
    score_block(nkb - 1, True)

    q_t = qb * bq + lax.broadcasted_iota(I32, (1, bq), 1)
    k_eff = jnp.minimum(topk, q_t + 1).astype(F32)

    def count_ge(thr):
        def body(kb, acc):
            for c in range(bk // COUNT_ROWS):
                keys = key_ref[kb, c * COUNT_ROWS:(c + 1) * COUNT_ROWS, :]
                acc = acc + _fold_rows(jnp.where(keys >= thr, 1.0, 0.0), 8)
            return acc
        acc = lax.fori_loop(0, nkb, body, jnp.zeros((8, bq), F32))
        return jnp.sum(acc, axis=0, keepdims=True)

    def count16_ge(ref, thr16):
        one, zero = jnp.int16(1), jnp.int16(0)

        def body(kb, acc):
            for c in range(bk // COUNT_ROWS16):
                half = ref[kb, c * COUNT_ROWS16:(c + 1) * COUNT_ROWS16, :]
                acc = acc + _fold_rows(jnp.where(half >= thr16, one, zero), I16_ROWS)
            return acc
        acc = lax.fori_loop(0, nkb, body, jnp.zeros((I16_ROWS, bq), I16))
        return jnp.sum(acc.astype(F32), axis=0, keepdims=True)

    def kth_largest16(ref, kth):
        def bit_step(i, biased):
            cand = biased | lax.shift_left(jnp.int32(1), 15 - i)
            cnt = count16_ge(ref, (cand - HALF_BIAS).astype(I16))
            return jnp.where(cnt >= kth, cand, biased)
        return lax.fori_loop(0, 16, bit_step, jnp.zeros((1, bq), I32)) - HALF_BIAS

    t_hi = kth_largest16(hi_ref, k_eff)
    above = jnp.where(t_hi >= HALF_BIAS - 1, 0.0,
                      count16_ge(hi_ref, jnp.minimum(t_hi + 1, HALF_BIAS - 1).astype(I16)))
    t_hi16 = t_hi.astype(I16)

    def keep_group(kb, _):
        lo_ref[kb] = jnp.where(hi_ref[kb] == t_hi16, lo_ref[kb], jnp.int16(-HALF_BIAS))
        return 0

    lax.fori_loop(0, nkb, keep_group, 0)
    t_lo = kth_largest16(lo_ref, k_eff - above)
    thr = t_hi * (2 * HALF_BIAS) + (t_lo + HALF_BIAS)
    need = k_eff - count_ge(thr + 1)
    return thr, need


def _dsa_mask(key_ref, thr, need, *, nkb, bq, bk, sub):
    rs = lax.broadcasted_iota(I32, (sub, sub), 0)
    cs = lax.broadcasted_iota(I32, (sub, sub), 1)
    earlier = (cs < rs).astype(BF16)

    def mask_block(kb, seen):
        for j in range(bk // sub):
            key = key_ref[kb, j * sub:(j + 1) * sub, :]
            eq = jnp.where(key == thr, 1.0, 0.0)
            rank = _nn(earlier, eq.astype(BF16)) + seen
            take = jnp.where(rank < need, 1, 0)
            sel = (key + take) > thr
            key_ref[kb, j * sub:(j + 1) * sub, :] = lax.bitcast_convert_type(jnp.where(sel, 0.0, NEG_BIG), I32)
            seen = seen + jnp.sum(eq, axis=0, keepdims=True)
        return seen

    lax.fori_loop(0, nkb, mask_block, jnp.zeros((1, bq), F32))


def _dsa_attend(qd_ref, kd_ref, vt_ref, o_ref, key_ref, s_ref, *, nkb, bq, bk):
    lo_half, hi_half = _lane_half_masks()
    n_heads = qd_ref.shape[2] // HEAD_DIM
    qd = qd_ref[0]
    pairs_per_pass = DSA_HEADS_PER_PASS // 2

    def stacked_bias(kb):
        bias = lax.bitcast_convert_type(key_ref[kb], F32)
        return jnp.concatenate([bias] * DSA_HEADS_PER_PASS, axis=1)

    for g in range(n_heads // DSA_HEADS_PER_PASS):
        stacked = []
        for hp in range(g * pairs_per_pass, (g + 1) * pairs_per_pass):
            chunk = qd[:, hp * LANES:(hp + 1) * LANES]
            zero = jnp.zeros_like(chunk)
            stacked += [jnp.where(lo_half, chunk, zero), jnp.where(hi_half, chunk, zero)]
        qq = jnp.concatenate(stacked, axis=0)

        _, acc = _flash_pipelined_t(qq, kd_ref, lambda kb: vt_ref[0, kb], s_ref, nkb, bk, stacked_bias, stacked_bias)
        o = acc[:HEAD_DIM] / acc[HEAD_DIM:HEAD_DIM + 1]
        for j in range(pairs_per_pass):
            hp = g * pairs_per_pass + j
            pair = jnp.concatenate([o[:, 2 * j * bq:(2 * j + 1) * bq], o[:, (2 * j + 1) * bq:(2 * j + 2) * bq]],
                                   axis=0)
            o_ref[0, :, hp * LANES:(hp + 1) * LANES] = pair.T.astype(o_ref.dtype)


def _dsa_kernel(qd_ref, qx_ref, wt_ref, kd_ref, kx_ref, vt_ref, o_ref, key_ref, *, bq, bk, sub, topk):
    qb = pl.program_id(1)
    nkb = (qb * bq) // bk + 1
    halves16 = pltpu.VMEM(key_ref.shape, I16)
    thr, need = pl.run_scoped(
        functools.partial(_dsa_select, qx_ref, wt_ref, kx_ref, key_ref, qb=qb, nkb=nkb, bq=bq, bk=bk, topk=topk),
        halves16, halves16)
    _dsa_mask(key_ref, thr, need, nkb=nkb, bq=bq, bk=bk, sub=sub)
    pl.run_scoped(functools.partial(_dsa_attend, qd_ref, kd_ref, vt_ref, o_ref, key_ref, nkb=nkb, bq=bq, bk=bk),
                  pltpu.VMEM((2, bk, DSA_HEADS_PER_PASS * bq), F32))


def _dsa_attention(qd, qx, wt, kd2, kx2, vt2, blk, bk, topk):
    bsz, s, wq = qd.shape
    kspec = pl.BlockSpec((1, s, LANES), lambda b, i: (b, 0, 0), pipeline_mode=pl.Buffered(1))
    tiles = (s // bk, bk, blk)
    return pl.pallas_call(
        functools.partial(_dsa_kernel, bq=blk, bk=bk, sub=min(bk, 256), topk=topk),
        grid=(bsz, s // blk),
        in_specs=[pl.BlockSpec((1, blk, wq), lambda b, i: (b, i, 0)),
                  pl.BlockSpec((1, blk, qx.shape[2]), lambda b, i: (b, i, 0)),
                  pl.BlockSpec((1, wt.shape[1], blk), lambda b, i: (b, 0, i)),
                  kspec, kspec,
                  pl.BlockSpec((1,) + vt2.shape[1:], lambda b, i: (b, 0, 0, 0), pipeline_mode=pl.Buffered(1))],
        out_specs=pl.BlockSpec((1, blk, wq), lambda b, i: (b, i, 0)),
        out_shape=jax.ShapeDtypeStruct((bsz, s, wq), BF16),
        scratch_shapes=[pltpu.VMEM(tiles, I32)],
        name="dsa_attention",
        compiler_params=_cparams(("arbitrary", "arbitrary")),
    )(qd, qx, wt, kd2, kx2, vt2)


def _diff_kernel(q_ref, k_ref, vt_ref, lam_ref, g_ref, o_ref, s_ref, *, bq, bk, lambda_init):
    qi = pl.program_id(2)
    heads = DIFF_HEADS_PER_STEP
    width = 2 * bq
    lo_half, hi_half = _lane_half_masks()
    qqs = []
    for h in range(heads):
        q = q_ref[0, :, h * LANES:(h + 1) * LANES]
        zero = jnp.zeros_like(q)
        qqs.append(jnp.concatenate([jnp.where(lo_half, q, zero), jnp.where(hi_half, q, zero)], axis=0))
    n_full = (qi * bq) // bk

    def qk(kb, slot):
        off = pl.multiple_of(kb * bk, bk)
        for h in range(heads):
            s_ref[slot, :, h * width:(h + 1) * width] = _nt(k_ref[0, pl.ds(off, bk), h * LANES:(h + 1) * LANES], qqs[h])

    def consume(kb, slot, carry):
        return _flash_update_t(s_ref[slot], *carry, [vt_ref[0, h, kb] for h in range(heads)])

    def consume_diag(n_sub, carry):
        qk(0, 0)
        nk = n_sub * bq
        s = s_ref[1, :nk, :]
        s_last = jnp.where(_tri_mask(bq, bq, 2 * heads, strict=False), s[nk - bq:], NEG_BIG)
        s = s_last if n_sub == 1 else jnp.concatenate([s[:nk - bq], s_last], axis=0)
        return _flash_update_t(s, *carry, [vt_ref[0, h, n_full, :, :nk] for h in range(heads)])

    def pair(p, carry):
        qk(2 * p + 1, 1)
        carry = consume(2 * p, 0, carry)
        qk(2 * p + 2, 0)
        return consume(2 * p + 1, 1, carry)

    qk(n_full, 1)
    diag_sub = (qi * bq - n_full * bk) // bq
    carry = _switch(diag_sub, [functools.partial(consume_diag, v + 1) for v in range(bk // bq)],
                    _flash_init(vt_ref.shape[3], heads * width))
    carry = lax.fori_loop(0, n_full // 2, pair, carry)
    _, acc_all = lax.cond(n_full % 2 == 1, lambda cr: consume(n_full - 1, 0, cr), lambda cr: cr, carry)

    lp = lam_ref[...]
    lam = (jnp.exp(jnp.sum(lp[0:1] * lp[1:2], axis=1, keepdims=True))
           - jnp.exp(jnp.sum(lp[2:3] * lp[3:4], axis=1, keepdims=True)) + lambda_init)
    for h in range(heads):
        l = acc_all[LANES:LANES + 1, h * width:(h + 1) * width]
        acc = acc_all[:LANES, h * width:(h + 1) * width]
        o = acc[:, :bq] / l[:, :bq] - lam * (acc[:, bq:] / l[:, bq:])
        ms = jnp.mean(o * o, axis=0, keepdims=True)
        o = o * lax.rsqrt(ms + LN_EPS) * g_ref[...] * (1.0 - lambda_init)
        o_ref[0, :, h * LANES:(h + 1) * LANES] = o.T.astype(o_ref.dtype)


def _diff_attention(q, k, vt, lam_params, subln_g, blk, bk, lambda_init):
    bsz, s, w = q.shape
    heads = DIFF_HEADS_PER_STEP
    once = pl.Buffered(1)
    return pl.pallas_call(
        functools.partial(_diff_kernel, bq=blk, bk=bk, lambda_init=lambda_init),
        grid=(bsz, w // (heads * LANES), s // blk),
        in_specs=[pl.BlockSpec((1, blk, heads * LANES), lambda b, h, i: (b, i, h)),
                  pl.BlockSpec((1, s, heads * LANES), lambda b, h, i: (b, 0, h), pipeline_mode=once),
                  pl.BlockSpec((1, heads) + vt.shape[2:], lambda b, h, i: (b, h, 0, 0, 0), pipeline_mode=once),
                  pl.BlockSpec(lam_params.shape, lambda b, h, i: (0, 0)),
                  pl.BlockSpec((LANES, 1), lambda b, h, i: (0, 0))],
        out_specs=pl.BlockSpec((1, blk, heads * LANES), lambda b, h, i: (b, i, h)),
        out_shape=jax.ShapeDtypeStruct((bsz, s, w), BF16),
        scratch_shapes=[pltpu.VMEM((2, bk, heads * 2 * blk), F32)],
        name="diff_attention",
        compiler_params=_cparams(("arbitrary", "arbitrary", "arbitrary")),
    )(q, k, vt, lam_params, subln_g.reshape(LANES, 1))


def _pick(n, prefs):
    for p in prefs:
        if n % p == 0:
            return p
    return n


def kernel(x, c, positions, w_mod, b_mod, w_in_even, w_out_even, w_in_odd, lam_q1, lam_k1, lam_q2, lam_k2,
           subln_g, w_out_odd, ln_mix_g, ln_mix_b, w_gate, w_up, w_down, ln_ffn_g, ln_ffn_b):
    bsz, s, d = x.shape
    depth = w_mod.shape[0]
    alpha = (2 * depth) ** 0.25
    dff = w_gate.shape[2]
    rows = bsz * s
    tm = _pick(s, (512, 256, 128))
    tf = dff
    blk = _pick(s, (256, 128))
    bk_dsa = _pick(s, (1024, 512, 256))
    bk_diff = _pick(s, (1024, 512, 256))
    bq_diff = _pick(s, (512, 256, 128))
    scale = HEAD_DIM ** -0.5

    inv = ROPE_THETA ** (-jnp.arange(0, HEAD_DIM, 2, dtype=F32) / HEAD_DIM)
    ang = positions.astype(F32)[..., None] * inv
    cos, sin = jnp.cos(ang).reshape(rows, -1), jnp.sin(ang).reshape(rows, -1)
    cos128 = jnp.concatenate([cos] * 4, axis=1)
    sg128 = jnp.concatenate([-sin, sin, -sin, sin], axis=1)

    mod = _modulation(c, w_mod.astype(BF16), b_mod)
    xf = x.reshape(rows, d)

    for i in range(depth):
        mod_l = mod[i].reshape(bsz, 6, d)
        if i % 2 == 0:
            w = w_in_even[i // 2]
            n_sb = n_dsa = d // (2 * HEAD_DIM)
            sbw, dsw, ixw = n_sb * HEAD_DIM, n_dsa * HEAD_DIM, N_IDX_HEADS * HEAD_DIM
            offs = [0]
            for width in (sbw, sbw, sbw, dsw, HEAD_DIM, HEAD_DIM, ixw, HEAD_DIM, N_IDX_HEADS):
                offs.append(offs[-1] + width)
            col = lambda j: w[:, offs[j]:offs[j + 1]]
            pad = jnp.zeros((d, LANES - N_IDX_HEADS), w.dtype)
            wp = jnp.concatenate([col(0), col(1), col(2), col(5), col(5), col(8), pad], axis=1)
            wr = jnp.concatenate([col(3), col(6), col(4), col(4), col(7), col(7)], axis=1)
            plain_outs = ((0, sbw, scale), (sbw, sbw, 1.0), (2 * sbw, sbw, 1.0),
                          (3 * sbw, LANES, 1.0), (3 * sbw + LANES, LANES, N_IDX_HEADS ** -0.5))
            rope_outs = ((0, dsw, scale), (dsw, ixw, scale), (dsw + ixw, LANES, 1.0),
                         (dsw + ixw + LANES, LANES, 1.0))
            dts = (BF16, BF16, BF16, BF16, F32, BF16, BF16, BF16, BF16)
            q_sb, k_sb, v_sb, vd2, wx, qd, qx, kd2, kx2 = _inproj(
                xf, mod_l, cos128, sg128, wp.astype(BF16), wr.astype(BF16), _rot_partner(wr).astype(BF16),
                plain_outs, rope_outs, dts, s, tm)
            r3 = lambda t: t.reshape(bsz, s, t.shape[1])
            vt_sb = v_sb.reshape(bsz, s // blk, blk, sbw // LANES, LANES).transpose(0, 3, 1, 4, 2)
            o_sb = _sb_attention(r3(q_sb), r3(k_sb), vt_sb, blk)
            wt = r3(wx)[:, :, :8].swapaxes(1, 2)
            vt2 = vd2.reshape(bsz, s // bk_dsa, bk_dsa, LANES).swapaxes(2, 3)
            vt2 = jnp.concatenate([vt2[:, :, :HEAD_DIM], jnp.ones_like(vt2[:, :, :ONES_ROWS])], axis=2)
            o_dsa = _dsa_attention(r3(qd), r3(qx), wt, r3(kd2), r3(kx2), vt2, blk, bk_dsa,
                                   min(DSA_TOPK_MAX, s // 4))
            w_out = w_out_even[i // 2].astype(BF16)
            o_list = [o_sb.reshape(rows, sbw), o_dsa.reshape(rows, dsw)]
            w_list = [w_out[:sbw], w_out[sbw:]]
        else:
            j = i // 2
            w = w_in_odd[j]
            dw = w.shape[1] // 3
            lambda_init = 0.8 - 0.6 * math.exp(-0.3 * i)
            wp = w[:, 2 * dw:]
            wr = w[:, :2 * dw]
            plain_outs = ((0, dw, 1.0),)
            rope_outs = ((0, dw, scale), (dw, dw, 1.0))
            v_df, q_df, k_df = _inproj(
                xf, mod_l, cos128, sg128, wp.astype(BF16), wr.astype(BF16), _rot_partner(wr).astype(BF16),
                plain_outs, rope_outs, (BF16, BF16, BF16), s, tm)
            r3 = lambda t: t.reshape(bsz, s, t.shape[1])
            lam_params = jnp.stack([lam_q1[j], lam_k1[j], lam_q2[j], lam_k2[j]]).astype(F32)
            vt = v_df.reshape(bsz, s // bk_diff, bk_diff, dw // LANES, LANES).transpose(0, 3, 1, 4, 2)
            vt = jnp.concatenate([vt, jnp.ones_like(vt[:, :, :, :ONES_ROWS])], axis=3)
            o_df = _diff_attention(r3(q_df), r3(k_df), vt, lam_params, subln_g[j].astype(F32), bq_diff, bk_diff,
                                   lambda_init)
            o_list = [o_df.reshape(rows, dw)]
            w_list = [w_out_odd[j].astype(BF16)]
        xf = _outproj_ln(o_list, w_list, xf, mod_l, ln_mix_g[i], ln_mix_b[i], alpha, s, tm)
        xf = _ffn_ln(xf, mod_l, w_gate[i].astype(BF16), w_up[i].astype(BF16), w_down[i].astype(BF16),
                     ln_ffn_g[i], ln_ffn_b[i], alpha, s, tm, tf)
    return xf.reshape(bsz, s, d)
```

```python
import functools
import math

import jax
import jax.numpy as jnp
from jax import lax
from jax.experimental import pallas as pl
from jax.experimental.pallas import tpu as pltpu

HEAD_DIM = 64
N_IDX_HEADS = 4
DSA_TOPK_MAX = 256
DSA_HEADS_PER_PASS = 4
DIFF_HEADS_PER_STEP = 1
ROPE_THETA = 10000.0
LN_EPS = 1e-5
LANES = 128
NEG_BIG = -1e30
INT_MIN = -2 ** 31
LOG2E = 1.4426950408889634
SB_EXP_FLOOR = 105.0
ONES_ROWS = 16
COUNT_ROWS16 = 128
I16_ROWS = 16
HALF_BIAS = 2 ** 15

F32 = jnp.float32
BF16 = jnp.bfloat16
I32 = jnp.int32
I16 = jnp.int16

VMEM_LIMIT = 56 * 1024 * 1024


def _cparams(sem):
    return pltpu.CompilerParams(dimension_semantics=sem, vmem_limit_bytes=VMEM_LIMIT)


def _nt(a, b):
    return lax.dot_general(a, b, (((1,), (1,)), ((), ())), preferred_element_type=F32)


def _nn(a, b):
    return jnp.dot(a, b, preferred_element_type=F32)


def _layer_norm(v, g, b):
    mu = jnp.mean(v, axis=-1, keepdims=True)
    d = v - mu
    var = jnp.mean(d * d, axis=-1, keepdims=True)
    return d * lax.rsqrt(var + LN_EPS) * g + b


def _mod_kernel(c_ref, w_ref, b_ref, o_ref):
    c = c_ref[...]
    a = (c * jax.nn.sigmoid(c)).astype(BF16)
    o_ref[0] = _nn(a, w_ref[0]) + b_ref[0]


def _modulation(c, w_mod_bf, b_mod):
    depth, d, n = w_mod_bf.shape
    bsz = c.shape[0]
    tn = n // 4
    return pl.pallas_call(
        _mod_kernel,
        grid=(depth, n // tn),
        in_specs=[pl.BlockSpec((bsz, d), lambda l, j: (0, 0)),
                  pl.BlockSpec((1, d, tn), lambda l, j: (l, 0, j)),
                  pl.BlockSpec((1, 1, tn), lambda l, j: (l, 0, j))],
        out_specs=pl.BlockSpec((1, bsz, tn), lambda l, j: (l, 0, j)),
        out_shape=jax.ShapeDtypeStruct((depth, bsz, n), F32),
        name="modulation",
        compiler_params=_cparams(("arbitrary", "arbitrary")),
    )(c, w_mod_bf, b_mod.reshape(depth, 1, n))


def _inproj_kernel(x_ref, mod_ref, cos_ref, sin_ref, wp_ref, wr_ref, wrr_ref, *out_refs,
                   plain_outs, rope_outs):
    x = x_ref[...]
    sh = mod_ref[0, 0:1, :]
    sc = mod_ref[0, 1:2, :]
    h = (x * (1.0 + sc) + sh).astype(BF16)
    plain = _nn(h, wp_ref[...])
    rope = _nn(h, wr_ref[...])
    rot = _nn(h, wrr_ref[...])
    c32, s32 = cos_ref[...], sin_ref[...]
    cos = jnp.concatenate([c32] * 4, axis=1)
    sg = jnp.concatenate([-s32, s32, -s32, s32], axis=1)
    n = 0
    for (c0, width, scale) in plain_outs:
        o = out_refs[n]
        v = plain[:, c0:c0 + width]
        if scale != 1.0:
            v = v * scale
        if o.dtype == F32:
            v = v.astype(BF16).astype(F32)
        o[...] = v.astype(o.dtype)
        n += 1
    for (c0, width, scale) in rope_outs:
        o = out_refs[n]
        for j in range(width // LANES):
            sl = slice(c0 + j * LANES, c0 + (j + 1) * LANES)
            v = rope[:, sl] * cos + rot[:, sl] * sg
            if scale != 1.0:
                v = v * scale
            o[:, j * LANES:(j + 1) * LANES] = v.astype(o.dtype)
        n += 1


def _inproj(xf, mod_l, cos_tab, sin_tab, wp, wr, wrr, plain_outs, rope_outs, out_dtypes, rows_per_batch, tm):
    r, d = xf.shape
    tiles_per_batch = rows_per_batch // tm
    widths = [w for (_, w, _) in plain_outs] + [w for (_, w, _) in rope_outs]
    row = lambda i: (i, 0)
    const = lambda i: (0, 0)
    return pl.pallas_call(
        functools.partial(_inproj_kernel, plain_outs=plain_outs, rope_outs=rope_outs),
        grid=(r // tm,),
        in_specs=[pl.BlockSpec((tm, d), row),
                  pl.BlockSpec((1, 6, d), lambda i: (i // tiles_per_batch, 0, 0)),
                  pl.BlockSpec((tm, cos_tab.shape[1]), row),
                  pl.BlockSpec((tm, sin_tab.shape[1]), row),
                  pl.BlockSpec(wp.shape, const),
                  pl.BlockSpec(wr.shape, const),
                  pl.BlockSpec(wrr.shape, const)],
        out_specs=[pl.BlockSpec((tm, w), row) for w in widths],
        out_shape=[jax.ShapeDtypeStruct((r, w), dt) for w, dt in zip(widths, out_dtypes)],
        name="inproj",
        compiler_params=_cparams(("arbitrary",)),
    )(xf, mod_l, cos_tab, sin_tab, wp, wr, wrr)


def _rot_partner(w):
    d, n = w.shape
    return w.reshape(d, n // HEAD_DIM, 2, HEAD_DIM // 2)[:, :, ::-1, :].reshape(d, n)


def _outproj_kernel(*refs, n_in, alpha):
    o_refs = refs[:n_in]
    w_refs = refs[n_in:2 * n_in]
    x_ref, mod_ref, g_ref, b_ref, out_ref = refs[2 * n_in:]
    y = _nn(o_refs[0][...], w_refs[0][...])
    for a, w in zip(o_refs[1:], w_refs[1:]):
        y = y + _nn(a[...], w[...])
    gate = mod_ref[0, 2:3, :]
    v = alpha * x_ref[...] + (1.0 + gate) * y
    out_ref[...] = _layer_norm(v, g_ref[...], b_ref[...])


def _outproj_ln(o_list, w_list, xf, mod_l, g, b, alpha, rows_per_batch, tm):
    r, d = xf.shape
    tiles_per_batch = rows_per_batch // tm
    row = lambda i: (i, 0)
    const = lambda i: (0, 0)
    n_in = len(o_list)
    return pl.pallas_call(
        functools.partial(_outproj_kernel, n_in=n_in, alpha=alpha),
        grid=(r // tm,),
        in_specs=([pl.BlockSpec((tm, o.shape[1]), row) for o in o_list]
                  + [pl.BlockSpec(w.shape, const) for w in w_list]
                  + [pl.BlockSpec((tm, d), row),
                     pl.BlockSpec((1, 6, d), lambda i: (i // tiles_per_batch, 0, 0)),
                     pl.BlockSpec((1, d), const),
                     pl.BlockSpec((1, d), const)]),
        out_specs=pl.BlockSpec((tm, d), row),
        out_shape=jax.ShapeDtypeStruct((r, d), F32),
        name="outproj_ln",
        compiler_params=_cparams(("arbitrary",)),
    )(*o_list, *w_list, xf, mod_l, g.reshape(1, d), b.reshape(1, d))


def _ffn_kernel(x_ref, mod_ref, wg_ref, wu_ref, wd_ref, g_ref, b_ref, out_ref, h_ref, acc_ref, *, alpha):
    f = pl.program_id(1)

    @pl.when(f == 0)
    def _():
        sh = mod_ref[0, 3:4, :]
        sc = mod_ref[0, 4:5, :]
        h_ref[...] = (x_ref[...] * (1.0 + sc) + sh).astype(BF16)
        acc_ref[...] = jnp.zeros_like(acc_ref)

    h = h_ref[...]
    gate = _nn(h, wg_ref[...])
    up = _nn(h, wu_ref[...])
    a = (gate * jax.nn.sigmoid(gate) * up).astype(BF16)
    acc_ref[...] += _nn(a, wd_ref[...])

    @pl.when(f == pl.num_programs(1) - 1)
    def _():
        gf = mod_ref[0, 5:6, :]
        v = alpha * x_ref[...] + (1.0 + gf) * acc_ref[...]
        out_ref[...] = _layer_norm(v, g_ref[...], b_ref[...])


def _ffn_ln(xf, mod_l, wg, wu, wd, g, b, alpha, rows_per_batch, tm, tf):
    r, d = xf.shape
    dff = wg.shape[1]
    tiles_per_batch = rows_per_batch // tm
    resident = pl.Buffered(1) if tf == dff else None
    return pl.pallas_call(
        functools.partial(_ffn_kernel, alpha=alpha),
        grid=(r // tm, dff // tf),
        in_specs=[pl.BlockSpec((tm, d), lambda i, f: (i, 0)),
                  pl.BlockSpec((1, 6, d), lambda i, f: (i // tiles_per_batch, 0, 0)),
                  pl.BlockSpec((d, tf), lambda i, f: (0, f), pipeline_mode=resident),
                  pl.BlockSpec((d, tf), lambda i, f: (0, f), pipeline_mode=resident),
                  pl.BlockSpec((tf, d), lambda i, f: (f, 0), pipeline_mode=resident),
                  pl.BlockSpec((1, d), lambda i, f: (0, 0)),
                  pl.BlockSpec((1, d), lambda i, f: (0, 0))],
        out_specs=pl.BlockSpec((tm, d), lambda i, f: (i, 0)),
        out_shape=jax.ShapeDtypeStruct((r, d), F32),
        scratch_shapes=[pltpu.VMEM((tm, d), BF16), pltpu.VMEM((tm, d), F32)],
        name="ffn_ln",
        compiler_params=_cparams(("arbitrary", "arbitrary")),
    )(xf, mod_l, wg, wu, wd, g.reshape(1, d), b.reshape(1, d))


def _lane_half_masks():
    lane = lax.broadcasted_iota(I32, (1, LANES), 1)
    return lane < HEAD_DIM, lane >= HEAD_DIM


def _tri_mask(n_keys, n_queries, reps, strict):
    kpos = lax.broadcasted_iota(I32, (n_keys, n_queries), 0)
    qpos = lax.broadcasted_iota(I32, (n_keys, n_queries), 1)
    tri = kpos < qpos if strict else kpos <= qpos
    return jnp.concatenate([tri] * reps, axis=1)


def _switch(index, branches, operand):
    def build(lo, hi):
        if hi - lo == 1:
            return branches[lo]
        mid = (lo + hi) // 2
        return lambda x: lax.cond(index < mid, build(lo, mid), build(mid, hi), x)
    return build(0, len(branches))(operand)


def _sb_kernel(q_ref, k_ref, vt_ref, o_ref, *, bq):
    qi = pl.program_id(2)
    q = q_ref[0]
    lo_half, hi_half = _lane_half_masks()
    zero = jnp.zeros_like(q)
    qq = jnp.concatenate([jnp.where(lo_half, q, zero), jnp.where(hi_half, q, zero)], axis=0)
    rs = lax.broadcasted_iota(I32, (bq, bq), 0)
    cs = lax.broadcasted_iota(I32, (bq, bq), 1)
    later = (cs > rs).astype(BF16)

    def block(kb, c, acc, diag):
        off = pl.multiple_of(kb * bq, bq)
        z = _nt(k_ref[0, pl.ds(off, bq), :], qq)
        sp = jnp.maximum(z, 0.0) + jnp.log(1.0 + jnp.exp2(jnp.abs(z) * -LOG2E))
        if diag:
            past = _tri_mask(bq, bq, 2, strict=True)
            sp = jnp.where(past, sp, 0.0)
        tail = _nn(later, sp.astype(BF16))
        w = jnp.exp(z - sp - tail - c)
        if diag:
            w = jnp.where(past, w, 0.0)
        acc = acc + _nn(vt_ref[0, 0, kb], w.astype(BF16))
        return c + tail[0:1] + sp[0:1], acc

    c, acc = block(qi, jnp.zeros((1, 2 * bq), F32), jnp.zeros((LANES, 2 * bq), F32), True)

    def more(state):
        kb, c_min, _, _ = state
        return jnp.logical_and(kb >= 0, c_min <= SB_EXP_FLOOR)

    def step(state):
        kb, _, c, acc = state
        c, acc = block(kb, c, acc, False)
        return kb - 1, jnp.min(c), c, acc

    _, _, _, acc = lax.while_loop(more, step, (qi - 1, jnp.min(c), c, acc))
    out_t = jnp.concatenate([acc[:HEAD_DIM, :bq], acc[HEAD_DIM:, bq:]], axis=0)
    o_ref[0] = out_t.T.astype(o_ref.dtype)


def _sb_attention(q, k, vt, blk):
    bsz, s, w = q.shape
    return pl.pallas_call(
        functools.partial(_sb_kernel, bq=blk),
        grid=(bsz, w // LANES, s // blk),
        in_specs=[pl.BlockSpec((1, blk, LANES), lambda b, h, i: (b, i, h)),
                  pl.BlockSpec((1, s, LANES), lambda b, h, i: (b, 0, h)),
                  pl.BlockSpec((1, 1) + vt.shape[2:], lambda b, h, i: (b, h, 0, 0, 0))],
        out_specs=pl.BlockSpec((1, blk, LANES), lambda b, h, i: (b, i, h)),
        out_shape=jax.ShapeDtypeStruct((bsz, s, w), BF16),
        name="sb_attention",
        compiler_params=_cparams(("arbitrary", "arbitrary", "arbitrary")),
    )(q, k, vt)


def _fold_rows(x, target):
    while x.shape[0] > target:
        h = x.shape[0] // 2
        x = x[:h] + x[h:]
    return x


def _flash_update_t(s, m, acc, vt):
    m_new = jnp.maximum(m, jnp.max(s, axis=0, keepdims=True))
    p = jnp.exp(s - m_new).astype(BF16)
    if isinstance(vt, (list, tuple)):
        w = p.shape[1] // len(vt)
        pv = jnp.concatenate([_nn(v, p[:, g * w:(g + 1) * w]) for g, v in enumerate(vt)], axis=1)
    else:
        pv = _nn(vt, p)
    return m_new, jnp.exp(m - m_new) * acc + pv


def _flash_init(v_rows, r):
    return (jnp.full((1, r), NEG_BIG, F32), jnp.zeros((v_rows, r), F32))


def _flash_pipelined_t(qq, k_ref, vt_block, s_ref, n_blocks, bk, loop_bias, tail_bias):
    last = n_blocks - 1
    n_pairs = last // 2

    def qk(kb, slot):
        off = pl.multiple_of(kb * bk, bk)
        s_ref[slot] = _nt(k_ref[0, pl.ds(off, bk), :], qq)

    def consume(kb, slot, carry, bias_fn):
        s = s_ref[slot]
        if bias_fn is not None:
            s = s + bias_fn(kb)
        return _flash_update_t(s, *carry, vt_block(kb))

    def pair(p, carry):
        qk(2 * p + 1, 1)
        carry = consume(2 * p, 0, carry, loop_bias)
        qk(2 * p + 2, 0)
        return consume(2 * p + 1, 1, carry, loop_bias)

    qk(0, 0)
    carry = lax.fori_loop(0, n_pairs, pair, _flash_init(vt_block(0).shape[0], qq.shape[0]))
    kb = 2 * n_pairs
    qk(jnp.minimum(kb + 1, last), 1)
    carry = consume(kb, 0, carry, tail_bias)
    return lax.cond(kb < last, lambda cr: consume(kb + 1, 1, cr, tail_bias), lambda cr: cr, carry)


def _dsa_select(qx_ref, wt_ref, kx_ref, key_ref, hi_ref, lo_ref, *, qb, nkb, bq, bk, topk):
    halves = _lane_half_masks()

    qx = qx_ref[0]
    wt = wt_ref[0]
    qx_heads = []
    for hx in range(N_IDX_HEADS):
        chunk = qx[:, (hx // 2) * LANES:(hx // 2 + 1) * LANES]
        qx_heads.append(jnp.where(halves[hx % 2], chunk, jnp.zeros_like(chunk)))

    def score_block(kb, masked):
        off = pl.multiple_of(kb * bk, bk)
        kx = kx_ref[0, pl.ds(off, bk), :]
        score = jnp.zeros((bk, bq), F32)
        for hx in range(N_IDX_HEADS):
            r = jnp.maximum(_nt(kx, qx_heads[hx]).astype(BF16), 0.0).astype(F32)
            score = score + r * wt[hx:hx + 1, :]
        bits = lax.bitcast_convert_type(score, I32)
        key = jnp.where(bits < 0, INT_MIN - bits, bits)
        if masked:
            kpos = lax.broadcasted_iota(I32, (bk, bq), 0) + (kb * bk - qb * bq)
            qpos = lax.broadcasted_iota(I32, (bk, bq), 1)
            key = jnp.where(kpos <= qpos, key, INT_MIN)
        key_ref[kb] = key
        hi_ref[kb] = lax.shift_right_arithmetic(key, 16).astype(I16)
        lo_ref[kb] = ((key & (2 * HALF_BIAS - 1)) - HALF_BIAS).astype(I16)

    def score_loop_body(kb, _):
        score_block(kb, False)
        return 0

    lax.fori_loop(0, nkb - 1, score_loop_body, 0)
    score_block(nkb - 1, True)

    q_t = qb * bq + lax.broadcasted_iota(I32, (1, bq), 1)
    k_eff = jnp.minimum(topk, q_t + 1).astype(F32)

    def count16_ge(ref, thr16):
        one, zero = jnp.int16(1), jnp.int16(0)

        def body(kb, acc):
            for c in range(bk // COUNT_ROWS16):
                half = ref[kb, c * COUNT_ROWS16:(c + 1) * COUNT_ROWS16, :]
                acc = acc + _fold_rows(jnp.where(half >= thr16, one, zero), I16_ROWS)
            return acc
        acc = lax.fori_loop(0, nkb, body, jnp.zeros((I16_ROWS, bq), I16))
        return jnp.sum(acc.astype(F32), axis=0, keepdims=True)

    def kth_largest16(ref, kth):
        def bit_step(i, biased):
            cand = biased | lax.shift_left(jnp.int32(1), 15 - i)
            cnt = count16_ge(ref, (cand - HALF_BIAS).astype(I16))
            return jnp.where(cnt >= kth, cand, biased)
        return lax.fori_loop(0, 16, bit_step, jnp.zeros((1, bq), I32)) - HALF_BIAS

    t_hi = kth_largest16(hi_ref, k_eff)
    above = jnp.where(t_hi >= HALF_BIAS - 1, 0.0,
                      count16_ge(hi_ref, jnp.minimum(t_hi + 1, HALF_BIAS - 1).astype(I16)))
    t_hi16 = t_hi.astype(I16)

    def keep_group(kb, _):
        lo_ref[kb] = jnp.where(hi_ref[kb] == t_hi16, lo_ref[kb], jnp.int16(-HALF_BIAS))
        return 0

    lax.fori_loop(0, nkb, keep_group, 0)
    t_lo = kth_largest16(lo_ref, k_eff - above)
    thr = t_hi * (2 * HALF_BIAS) + (t_lo + HALF_BIAS)
    above_lo = jnp.where(t_lo >= HALF_BIAS - 1, 0.0,
                         count16_ge(lo_ref, jnp.minimum(t_lo + 1, HALF_BIAS - 1).astype(I16)))
    need = k_eff - above - above_lo
    return thr, need


def _dsa_mask(key_ref, thr, need, *, nkb, bq, bk, sub):
    rs = lax.broadcasted_iota(I32, (sub, sub), 0)
    cs = lax.broadcasted_iota(I32, (sub, sub), 1)
    earlier = (cs < rs).astype(BF16)

    def mask_block(kb, seen):
        for j in range(bk // sub):
            key = key_ref[kb, j * sub:(j + 1) * sub, :]
            eq = jnp.where(key == thr, 1.0, 0.0)
            rank = _nn(earlier, eq.astype(BF16)) + seen
            take = jnp.where(rank < need, 1, 0)
            sel = (key + take) > thr
            key_ref[kb, j * sub:(j + 1) * sub, :] = lax.bitcast_convert_type(jnp.where(sel, 0.0, NEG_BIG), I32)
            seen = seen + jnp.sum(eq, axis=0, keepdims=True)
        return seen

    lax.fori_loop(0, nkb, mask_block, jnp.zeros((1, bq), F32))


def _dsa_attend(qd_ref, kd_ref, vt_ref, o_ref, key_ref, s_ref, *, nkb, bq, bk):
    lo_half, hi_half = _lane_half_masks()
    n_heads = qd_ref.shape[2] // HEAD_DIM
    qd = qd_ref[0]
    pairs_per_pass = DSA_HEADS_PER_PASS // 2

    def stacked_bias(kb):
        bias = lax.bitcast_convert_type(key_ref[kb], F32)
        return jnp.concatenate([bias] * DSA_HEADS_PER_PASS, axis=1)

    for g in range(n_heads // DSA_HEADS_PER_PASS):
        stacked = []
        for hp in range(g * pairs_per_pass, (g + 1) * pairs_per_pass):
            chunk = qd[:, hp * LANES:(hp + 1) * LANES]
            zero = jnp.zeros_like(chunk)
            stacked += [jnp.where(lo_half, chunk, zero), jnp.where(hi_half, chunk, zero)]
        qq = jnp.concatenate(stacked, axis=0)

        _, acc = _flash_pipelined_t(qq, kd_ref, lambda kb: vt_ref[0, kb], s_ref, nkb, bk, stacked_bias, stacked_bias)
        o = acc[:HEAD_DIM] / acc[HEAD_DIM:HEAD_DIM + 1]
        for j in range(pairs_per_pass):
            hp = g * pairs_per_pass + j
            pair = jnp.concatenate([o[:, 2 * j * bq:(2 * j + 1) * bq], o[:, (2 * j + 1) * bq:(2 * j + 2) * bq]],
                                   axis=0)
            o_ref[0, :, hp * LANES:(hp + 1) * LANES] = pair.T.astype(o_ref.dtype)


def _dsa_kernel(qd_ref, qx_ref, wt_ref, kd_ref, kx_ref, vt_ref, o_ref, key_ref, *, bq, bk, sub, topk):
    qb = pl.program_id(1)
    nkb = (qb * bq) // bk + 1
    halves16 = pltpu.VMEM(key_ref.shape, I16)
    thr, need = pl.run_scoped(
        functools.partial(_dsa_select, qx_ref, wt_ref, kx_ref, key_ref, qb=qb, nkb=nkb, bq=bq, bk=bk, topk=topk),
        halves16, halves16)
    _dsa_mask(key_ref, thr, need, nkb=nkb, bq=bq, bk=bk, sub=sub)
    pl.run_scoped(functools.partial(_dsa_attend, qd_ref, kd_ref, vt_ref, o_ref, key_ref, nkb=nkb, bq=bq, bk=bk),
                  pltpu.VMEM((2, bk, DSA_HEADS_PER_PASS * bq), F32))


def _dsa_attention(qd, qx, wt, kd2, kx2, vt2, blk, bk, topk):
    bsz, s, wq = qd.shape
    kspec = pl.BlockSpec((1, s, LANES), lambda b, i: (b, 0, 0), pipeline_mode=pl.Buffered(1))
    tiles = (s // bk, bk, blk)
    return pl.pallas_call(
        functools.partial(_dsa_kernel, bq=blk, bk=bk, sub=min(bk, 256), topk=topk),
        grid=(bsz, s // blk),
        in_specs=[pl.BlockSpec((1, blk, wq), lambda b, i: (b, i, 0)),
                  pl.BlockSpec((1, blk, qx.shape[2]), lambda b, i: (b, i, 0)),
                  pl.BlockSpec((1, wt.shape[1], blk), lambda b, i: (b, 0, i)),
                  kspec, kspec,
                  pl.BlockSpec((1,) + vt2.shape[1:], lambda b, i: (b, 0, 0, 0), pipeline_mode=pl.Buffered(1))],
        out_specs=pl.BlockSpec((1, blk, wq), lambda b, i: (b, i, 0)),
        out_shape=jax.ShapeDtypeStruct((bsz, s, wq), BF16),
        scratch_shapes=[pltpu.VMEM(tiles, I32)],
        name="dsa_attention",
        compiler_params=_cparams(("arbitrary", "arbitrary")),
    )(qd, qx, wt, kd2, kx2, vt2)


def _diff_kernel(q_ref, k_ref, vt_ref, lam_ref, g_ref, o_ref, s_ref, *, bq, bk, lambda_init):
    qi = pl.program_id(2)
    heads = DIFF_HEADS_PER_STEP
    width = 2 * bq
    lo_half, hi_half = _lane_half_masks()
    qqs = []
    for h in range(heads):
        q = q_ref[0, :, h * LANES:(h + 1) * LANES]
        zero = jnp.zeros_like(q)
        qqs.append(jnp.concatenate([jnp.where(lo_half, q, zero), jnp.where(hi_half, q, zero)], axis=0))
    n_full = (qi * bq) // bk

    def qk(kb, slot):
        off = pl.multiple_of(kb * bk, bk)
        for h in range(heads):
            s_ref[slot, :, h * width:(h + 1) * width] = _nt(k_ref[0, pl.ds(off, bk), h * LANES:(h + 1) * LANES], qqs[h])

    def consume(kb, slot, carry):
        return _flash_update_t(s_ref[slot], *carry, [vt_ref[0, h, kb] for h in range(heads)])

    def consume_diag(n_sub, carry):
        qk(0, 0)
        nk = n_sub * bq
        s = s_ref[1, :nk, :]
        s_last = jnp.where(_tri_mask(bq, bq, 2 * heads, strict=False), s[nk - bq:], NEG_BIG)
        s = s_last if n_sub == 1 else jnp.concatenate([s[:nk - bq], s_last], axis=0)
        return _flash_update_t(s, *carry, [vt_ref[0, h, n_full, :, :nk] for h in range(heads)])

    def pair(p, carry):
        qk(2 * p + 1, 1)
        carry = consume(2 * p, 0, carry)
        qk(2 * p + 2, 0)
        return consume(2 * p + 1, 1, carry)

    qk(n_full, 1)
    diag_sub = (qi * bq - n_full * bk) // bq
    carry = _switch(diag_sub, [functools.partial(consume_diag, v + 1) for v in range(bk // bq)],
                    _flash_init(vt_ref.shape[3], heads * width))
    carry = lax.fori_loop(0, n_full // 2, pair, carry)
    _, acc_all = lax.cond(n_full % 2 == 1, lambda cr: consume(n_full - 1, 0, cr), lambda cr: cr, carry)

    lp = lam_ref[...]
    lam = (jnp.exp(jnp.sum(lp[0:1] * lp[1:2], axis=1, keepdims=True))
           - jnp.exp(jnp.sum(lp[2:3] * lp[3:4], axis=1, keepdims=True)) + lambda_init)
    for h in range(heads):
        l = acc_all[LANES:LANES + 1, h * width:(h + 1) * width]
        acc = acc_all[:LANES, h * width:(h + 1) * width]
        o = acc[:, :bq] / l[:, :bq] - lam * (acc[:, bq:] / l[:, bq:])
        ms = jnp.mean(o * o, axis=0, keepdims=True)
        o = o * lax.rsqrt(ms + LN_EPS) * g_ref[...] * (1.0 - lambda_init)
        o_ref[0, :, h * LANES:(h + 1) * LANES] = o.T.astype(o_ref.dtype)


def _diff_attention(q, k, vt, lam_params, subln_g, blk, bk, lambda_init):
    bsz, s, w = q.shape
    heads = DIFF_HEADS_PER_STEP
    once = pl.Buffered(1)
    return pl.pallas_call(
        functools.partial(_diff_kernel, bq=blk, bk=bk, lambda_init=lambda_init),
        grid=(bsz, w // (heads * LANES), s // blk),
        in_specs=[pl.BlockSpec((1, blk, heads * LANES), lambda b, h, i: (b, i, h)),
                  pl.BlockSpec((1, s, heads * LANES), lambda b, h, i: (b, 0, h), pipeline_mode=once),
                  pl.BlockSpec((1, heads) + vt.shape[2:], lambda b, h, i: (b, h, 0, 0, 0), pipeline_mode=once),
                  pl.BlockSpec(lam_params.shape, lambda b, h, i: (0, 0)),
                  pl.BlockSpec((LANES, 1), lambda b, h, i: (0, 0))],
        out_specs=pl.BlockSpec((1, blk, heads * LANES), lambda b, h, i: (b, i, h)),
        out_shape=jax.ShapeDtypeStruct((bsz, s, w), BF16),
        scratch_shapes=[pltpu.VMEM((2, bk, heads * 2 * blk), F32)],
        name="diff_attention",
        compiler_params=_cparams(("arbitrary", "arbitrary", "arbitrary")),
    )(q, k, vt, lam_params, subln_g.reshape(LANES, 1))


def _pick(n, prefs):
    for p in prefs:
        if n % p == 0:
            return p
    return n


def kernel(x, c, positions, w_mod, b_mod, w_in_even, w_out_even, w_in_odd, lam_q1, lam_k1, lam_q2, lam_k2,
           subln_g, w_out_odd, ln_mix_g, ln_mix_b, w_gate, w_up, w_down, ln_ffn_g, ln_ffn_b):
    bsz, s, d = x.shape
    depth = w_mod.shape[0]
    alpha = (2 * depth) ** 0.25
    dff = w_gate.shape[2]
    rows = bsz * s
    tm = _pick(s, (512, 256, 128))
    tf = dff
    blk = _pick(s, (256, 128))
    bk_dsa = _pick(s, (1024, 512, 256))
    bk_diff = _pick(s, (1024, 512, 256))
    bq_diff = _pick(s, (512, 256, 128))
    assert s % tm == 0 and s % blk == 0 and bk_dsa % blk == 0 and bk_diff % bq_diff == 0, (s, tm, blk)
    assert d % (2 * LANES) == 0 and dff % LANES == 0, (d, dff)
    scale = HEAD_DIM ** -0.5

    inv = ROPE_THETA ** (-jnp.arange(0, HEAD_DIM, 2, dtype=F32) / HEAD_DIM)
    ang = positions.astype(F32)[..., None] * inv
    cos_tab, sin_tab = jnp.cos(ang).reshape(rows, -1), jnp.sin(ang).reshape(rows, -1)

    mod = _modulation(c, w_mod.astype(BF16), b_mod)
    xf = x.reshape(rows, d)

    for i in range(depth):
        mod_l = mod[i].reshape(bsz, 6, d)
        if i % 2 == 0:
            w = w_in_even[i // 2]
            n_sb = n_dsa = d // (2 * HEAD_DIM)
            sbw, dsw, ixw = n_sb * HEAD_DIM, n_dsa * HEAD_DIM, N_IDX_HEADS * HEAD_DIM
            offs = [0]
            for width in (sbw, sbw, sbw, dsw, HEAD_DIM, HEAD_DIM, ixw, HEAD_DIM, N_IDX_HEADS):
                offs.append(offs[-1] + width)
            col = lambda j: w[:, offs[j]:offs[j + 1]]
            pad = jnp.zeros((d, LANES - N_IDX_HEADS), w.dtype)
            wp = jnp.concatenate([col(0), col(1), col(2), col(5), col(5), col(8), pad], axis=1)
            wr = jnp.concatenate([col(3), col(6), col(4), col(4), col(7), col(7)], axis=1)
            plain_outs = ((0, sbw, scale), (sbw, sbw, 1.0), (2 * sbw, sbw, 1.0),
                          (3 * sbw, LANES, 1.0), (3 * sbw + LANES, LANES, N_IDX_HEADS ** -0.5))
            rope_outs = ((0, dsw, scale), (dsw, ixw, scale), (dsw + ixw, LANES, 1.0),
                         (dsw + ixw + LANES, LANES, 1.0))
            dts = (BF16, BF16, BF16, BF16, F32, BF16, BF16, BF16, BF16)
            q_sb, k_sb, v_sb, vd2, wx, qd, qx, kd2, kx2 = _inproj(
                xf, mod_l, cos_tab, sin_tab, wp.astype(BF16), wr.astype(BF16), _rot_partner(wr).astype(BF16),
                plain_outs, rope_outs, dts, s, tm)
            r3 = lambda t: t.reshape(bsz, s, t.shape[1])
            vt_sb = v_sb.reshape(bsz, s // blk, blk, sbw // LANES, LANES).transpose(0, 3, 1, 4, 2)
            o_sb = _sb_attention(r3(q_sb), r3(k_sb), vt_sb, blk)
            wt = r3(wx)[:, :, :8].swapaxes(1, 2)
            vt2 = vd2.reshape(bsz, s // bk_dsa, bk_dsa, LANES).swapaxes(2, 3)
            vt2 = jnp.concatenate([vt2[:, :, :HEAD_DIM], jnp.ones_like(vt2[:, :, :ONES_ROWS])], axis=2)
            o_dsa = _dsa_attention(r3(qd), r3(qx), wt, r3(kd2), r3(kx2), vt2, blk, bk_dsa,
                                   min(DSA_TOPK_MAX, s // 4))
            w_out = w_out_even[i // 2].astype(BF16)
            o_list = [o_sb.reshape(rows, sbw), o_dsa.reshape(rows, dsw)]
            w_list = [w_out[:sbw], w_out[sbw:]]
        else:
            j = i // 2
            w = w_in_odd[j]
            dw = w.shape[1] // 3
            lambda_init = 0.8 - 0.6 * math.exp(-0.3 * i)
            wp = w[:, 2 * dw:]
            wr = w[:, :2 * dw]
            plain_outs = ((0, dw, 1.0),)
            rope_outs = ((0, dw, scale), (dw, dw, 1.0))
            v_df, q_df, k_df = _inproj(
                xf, mod_l, cos_tab, sin_tab, wp.astype(BF16), wr.astype(BF16), _rot_partner(wr).astype(BF16),
                plain_outs, rope_outs, (BF16, BF16, BF16), s, tm)
            r3 = lambda t: t.reshape(bsz, s, t.shape[1])
            lam_params = jnp.stack([lam_q1[j], lam_k1[j], lam_q2[j], lam_k2[j]]).astype(F32)
            vt = v_df.reshape(bsz, s // bk_diff, bk_diff, dw // LANES, LANES).transpose(0, 3, 1, 4, 2)
            vt = jnp.concatenate([vt, jnp.ones_like(vt[:, :, :, :ONES_ROWS])], axis=3)
            o_df = _diff_attention(r3(q_df), r3(k_df), vt, lam_params, subln_g[j].astype(F32), bq_diff, bk_diff,
                                   lambda_init)
            o_list = [o_df.reshape(rows, dw)]
            w_list = [w_out_odd[j].astype(BF16)]
        xf = _outproj_ln(o_list, w_list, xf, mod_l, ln_mix_g[i], ln_mix_b[i], alpha, s, tm)
        xf = _ffn_ln(xf, mod_l, w_gate[i].astype(BF16), w_up[i].astype(BF16), w_down[i].astype(BF16),
                     ln_ffn_g[i], ln_ffn_b[i], alpha, s, tm, tf)
    return xf.reshape(bsz, s, d)
```

```python
import functools
import math

import jax
import jax.numpy as jnp
from jax import lax
from jax.experimental import pallas as pl
from jax.experimental.pallas import tpu as pltpu

HEAD_DIM = 64
N_IDX_HEADS = 4
DSA_TOPK_MAX = 256
DSA_HEADS_PER_PASS = 4
DIFF_HEADS_PER_STEP = 1
ROPE_THETA = 10000.0
LN_EPS = 1e-5
LANES = 128
NEG_BIG = -1e30
INT_MIN = -2 ** 31
LOG2E = 1.4426950408889634
SB_EXP_FLOOR = 105.0
ONES_ROWS = 16
COUNT_ROWS16 = 128
I16_ROWS = 16
HALF_BIAS = 2 ** 15

F32 = jnp.float32
BF16 = jnp.bfloat16
I32 = jnp.int32
I16 = jnp.int16

VMEM_LIMIT = 56 * 1024 * 1024


def _cparams(sem):
    return pltpu.CompilerParams(dimension_semantics=sem, vmem_limit_bytes=VMEM_LIMIT)


def _nt(a, b):
    return lax.dot_general(a, b, (((1,), (1,)), ((), ())), preferred_element_type=F32)


def _nn(a, b):
    return jnp.dot(a, b, preferred_element_type=F32)


def _layer_norm(v, g, b):
    mu = jnp.mean(v, axis=-1, keepdims=True)
    d = v - mu
    var = jnp.mean(d * d, axis=-1, keepdims=True)
    return d * lax.rsqrt(var + LN_EPS) * g + b


def _mod_kernel(c_ref, w_ref, b_ref, o_ref):
    c = c_ref[...]
    a = (c * jax.nn.sigmoid(c)).astype(BF16)
    o_ref[0] = _nn(a, w_ref[0]) + b_ref[0]


def _modulation(c, w_mod_bf, b_mod):
    depth, d, n = w_mod_bf.shape
    bsz = c.shape[0]
    tn = n // 4
    return pl.pallas_call(
        _mod_kernel,
        grid=(depth, n // tn),
        in_specs=[pl.BlockSpec((bsz, d), lambda l, j: (0, 0)),
                  pl.BlockSpec((1, d, tn), lambda l, j: (l, 0, j)),
                  pl.BlockSpec((1, 1, tn), lambda l, j: (l, 0, j))],
        out_specs=pl.BlockSpec((1, bsz, tn), lambda l, j: (l, 0, j)),
        out_shape=jax.ShapeDtypeStruct((depth, bsz, n), F32),
        name="modulation",
        compiler_params=_cparams(("arbitrary", "arbitrary")),
    )(c, w_mod_bf, b_mod.reshape(depth, 1, n))


def _inproj_kernel(x_ref, mod_ref, cos_ref, sin_ref, wp_ref, wr_ref, wrr_ref, *out_refs,
                   plain_outs, rope_outs, vt_outs):
    x = x_ref[...]
    sh = mod_ref[0, 0:1, :]
    sc = mod_ref[0, 1:2, :]
    h = (x * (1.0 + sc) + sh).astype(BF16)
    plain = _nn(h, wp_ref[...])
    rope = _nn(h, wr_ref[...])
    rot = _nn(h, wrr_ref[...])
    c32, s32 = cos_ref[...], sin_ref[...]
    cos = jnp.concatenate([c32] * 4, axis=1)
    sg = jnp.concatenate([-s32, s32, -s32, s32], axis=1)
    n = 0
    for (c0, width, scale) in plain_outs:
        o = out_refs[n]
        v = plain[:, c0:c0 + width]
        if scale != 1.0:
            v = v * scale
        if o.dtype == F32:
            v = v.astype(BF16).astype(F32)
        o[...] = v.astype(o.dtype)
        n += 1
    for (c0, width, scale) in rope_outs:
        o = out_refs[n]
        for j in range(width // LANES):
            sl = slice(c0 + j * LANES, c0 + (j + 1) * LANES)
            v = rope[:, sl] * cos + rot[:, sl] * sg
            if scale != 1.0:
                v = v * scale
            o[:, j * LANES:(j + 1) * LANES] = v.astype(o.dtype)
        n += 1
    for (c0, width, v_rows, n_ones) in vt_outs:
        o = out_refs[n]
        for j in range(width // LANES):
            vt = plain[:, c0 + j * LANES:c0 + (j + 1) * LANES].T
            o[j, 0:v_rows, :] = vt[:v_rows].astype(o.dtype)
            if n_ones:
                o[j, v_rows:v_rows + n_ones, :] = jnp.ones((n_ones, vt.shape[1]), o.dtype)
        n += 1


def _inproj(xf, mod_l, cos_tab, sin_tab, wp, wr, wrr, plain_outs, rope_outs, vt_outs, out_dtypes,
            rows_per_batch, tm):
    r, d = xf.shape
    tiles_per_batch = rows_per_batch // tm
    widths = [w for (_, w, _) in plain_outs] + [w for (_, w, _) in rope_outs]
    vt_shapes = [(w // LANES, v_rows + n_ones) for (_, w, v_rows, n_ones) in vt_outs]
    row = lambda i: (i, 0)
    const = lambda i: (0, 0)
    return pl.pallas_call(
        functools.partial(_inproj_kernel, plain_outs=plain_outs, rope_outs=rope_outs, vt_outs=vt_outs),
        grid=(r // tm,),
        in_specs=[pl.BlockSpec((tm, d), row),
                  pl.BlockSpec((1, 6, d), lambda i: (i // tiles_per_batch, 0, 0)),
                  pl.BlockSpec((tm, cos_tab.shape[1]), row),
                  pl.BlockSpec((tm, sin_tab.shape[1]), row),
                  pl.BlockSpec(wp.shape, const),
                  pl.BlockSpec(wr.shape, const),
                  pl.BlockSpec(wrr.shape, const)],
        out_specs=([pl.BlockSpec((tm, w), row) for w in widths]
                   + [pl.BlockSpec((c, v, tm), lambda i: (0, 0, i)) for (c, v) in vt_shapes]),
        out_shape=([jax.ShapeDtypeStruct((r, w), dt) for w, dt in zip(widths, out_dtypes)]
                   + [jax.ShapeDtypeStruct((c, v, r), BF16) for (c, v) in vt_shapes]),
        name="inproj",
        compiler_params=_cparams(("arbitrary",)),
    )(xf, mod_l, cos_tab, sin_tab, wp, wr, wrr)


def _rot_partner(w):
    d, n = w.shape
    return w.reshape(d, n // HEAD_DIM, 2, HEAD_DIM // 2)[:, :, ::-1, :].reshape(d, n)


def _outproj_kernel(*refs, n_in, alpha):
    o_refs = refs[:n_in]
    w_refs = refs[n_in:2 * n_in]
    x_ref, mod_ref, g_ref, b_ref, out_ref = refs[2 * n_in:]
    y = _nn(o_refs[0][...], w_refs[0][...])
    for a, w in zip(o_refs[1:], w_refs[1:]):
        y = y + _nn(a[...], w[...])
    gate = mod_ref[0, 2:3, :]
    v = alpha * x_ref[...] + (1.0 + gate) * y
    out_ref[...] = _layer_norm(v, g_ref[...], b_ref[...])


def _outproj_ln(o_list, w_list, xf, mod_l, g, b, alpha, rows_per_batch, tm):
    r, d = xf.shape
    tiles_per_batch = rows_per_batch // tm
    row = lambda i: (i, 0)
    const = lambda i: (0, 0)
    n_in = len(o_list)
    return pl.pallas_call(
        functools.partial(_outproj_kernel, n_in=n_in, alpha=alpha),
        grid=(r // tm,),
        in_specs=([pl.BlockSpec((tm, o.shape[1]), row) for o in o_list]
                  + [pl.BlockSpec(w.shape, const) for w in w_list]
                  + [pl.BlockSpec((tm, d), row),
                     pl.BlockSpec((1, 6, d), lambda i: (i // tiles_per_batch, 0, 0)),
                     pl.BlockSpec((1, d), const),
                     pl.BlockSpec((1, d), const)]),
        out_specs=pl.BlockSpec((tm, d), row),
        out_shape=jax.ShapeDtypeStruct((r, d), F32),
        name="outproj_ln",
        compiler_params=_cparams(("arbitrary",)),
    )(*o_list, *w_list, xf, mod_l, g.reshape(1, d), b.reshape(1, d))


def _ffn_kernel(x_ref, mod_ref, wg_ref, wu_ref, wd_ref, g_ref, b_ref, out_ref, h_ref, acc_ref, *, alpha):
    f = pl.program_id(1)

    @pl.when(f == 0)
    def _():
        sh = mod_ref[0, 3:4, :]
        sc = mod_ref[0, 4:5, :]
        h_ref[...] = (x_ref[...] * (1.0 + sc) + sh).astype(BF16)
        acc_ref[...] = jnp.zeros_like(acc_ref)

    h = h_ref[...]
    gate = _nn(h, wg_ref[...])
    up = _nn(h, wu_ref[...])
    a = (gate * jax.nn.sigmoid(gate) * up).astype(BF16)
    acc_ref[...] += _nn(a, wd_ref[...])

    @pl.when(f == pl.num_programs(1) - 1)
    def _():
        gf = mod_ref[0, 5:6, :]
        v = alpha * x_ref[...] + (1.0 + gf) * acc_ref[...]
        out_ref[...] = _layer_norm(v, g_ref[...], b_ref[...])


def _ffn_ln(xf, mod_l, wg, wu, wd, g, b, alpha, rows_per_batch, tm, tf):
    r, d = xf.shape
    dff = wg.shape[1]
    tiles_per_batch = rows_per_batch // tm
    resident = pl.Buffered(1) if tf == dff else None
    return pl.pallas_call(
        functools.partial(_ffn_kernel, alpha=alpha),
        grid=(r // tm, dff // tf),
        in_specs=[pl.BlockSpec((tm, d), lambda i, f: (i, 0)),
                  pl.BlockSpec((1, 6, d), lambda i, f: (i // tiles_per_batch, 0, 0)),
                  pl.BlockSpec((d, tf), lambda i, f: (0, f), pipeline_mode=resident),
                  pl.BlockSpec((d, tf), lambda i, f: (0, f), pipeline_mode=resident),
                  pl.BlockSpec((tf, d), lambda i, f: (f, 0), pipeline_mode=resident),
                  pl.BlockSpec((1, d), lambda i, f: (0, 0)),
                  pl.BlockSpec((1, d), lambda i, f: (0, 0))],
        out_specs=pl.BlockSpec((tm, d), lambda i, f: (i, 0)),
        out_shape=jax.ShapeDtypeStruct((r, d), F32),
        scratch_shapes=[pltpu.VMEM((tm, d), BF16), pltpu.VMEM((tm, d), F32)],
        name="ffn_ln",
        compiler_params=_cparams(("arbitrary", "arbitrary")),
    )(xf, mod_l, wg, wu, wd, g.reshape(1, d), b.reshape(1, d))


def _lane_half_masks():
    lane = lax.broadcasted_iota(I32, (1, LANES), 1)
    return lane < HEAD_DIM, lane >= HEAD_DIM


def _tri_mask(n_keys, n_queries, reps, strict):
    kpos = lax.broadcasted_iota(I32, (n_keys, n_queries), 0)
    qpos = lax.broadcasted_iota(I32, (n_keys, n_queries), 1)
    tri = kpos < qpos if strict else kpos <= qpos
    return jnp.concatenate([tri] * reps, axis=1)


def _switch(index, branches, operand):
    def build(lo, hi):
        if hi - lo == 1:
            return branches[lo]
        mid = (lo + hi) // 2
        return lambda x: lax.cond(index < mid, build(lo, mid), build(mid, hi), x)
    return build(0, len(branches))(operand)


def _sb_kernel(q_ref, k_ref, vt_ref, o_ref, *, bq):
    qi = pl.program_id(2)
    q = q_ref[0]
    lo_half, hi_half = _lane_half_masks()
    zero = jnp.zeros_like(q)
    qq = jnp.concatenate([jnp.where(lo_half, q, zero), jnp.where(hi_half, q, zero)], axis=0)
    rs = lax.broadcasted_iota(I32, (bq, bq), 0)
    cs = lax.broadcasted_iota(I32, (bq, bq), 1)
    later = (cs > rs).astype(BF16)

    def block(kb, c, acc, diag):
        off = pl.multiple_of(kb * bq, bq)
        z = _nt(k_ref[0, pl.ds(off, bq), :], qq)
        sp = jnp.maximum(z, 0.0) + jnp.log(1.0 + jnp.exp2(jnp.abs(z) * -LOG2E))
        if diag:
            past = _tri_mask(bq, bq, 2, strict=True)
            sp = jnp.where(past, sp, 0.0)
        tail = _nn(later, sp.astype(BF16))
        w = jnp.exp(z - sp - tail - c)
        if diag:
            w = jnp.where(past, w, 0.0)
        acc = acc + _nn(vt_ref[0, :, pl.ds(off, bq)], w.astype(BF16))
        return c + tail[0:1] + sp[0:1], acc

    c, acc = block(qi, jnp.zeros((1, 2 * bq), F32), jnp.zeros((LANES, 2 * bq), F32), True)

    def more(state):
        kb, c_min, _, _ = state
        return jnp.logical_and(kb >= 0, c_min <= SB_EXP_FLOOR)

    def step(state):
        kb, _, c, acc = state
        c, acc = block(kb, c, acc, False)
        return kb - 1, jnp.min(c), c, acc

    _, _, _, acc = lax.while_loop(more, step, (qi - 1, jnp.min(c), c, acc))
    out_t = jnp.concatenate([acc[:HEAD_DIM, :bq], acc[HEAD_DIM:, bq:]], axis=0)
    o_ref[0] = out_t.T.astype(o_ref.dtype)


def _sb_attention(q, k, vt, blk):
    bsz, s, w = q.shape
    return pl.pallas_call(
        functools.partial(_sb_kernel, bq=blk),
        grid=(bsz, w // LANES, s // blk),
        in_specs=[pl.BlockSpec((1, blk, LANES), lambda b, h, i: (b, i, h)),
                  pl.BlockSpec((1, s, LANES), lambda b, h, i: (b, 0, h)),
                  pl.BlockSpec((1, vt.shape[1], s), lambda b, h, i: (h, 0, b))],
        out_specs=pl.BlockSpec((1, blk, LANES), lambda b, h, i: (b, i, h)),
        out_shape=jax.ShapeDtypeStruct((bsz, s, w), BF16),
        name="sb_attention",
        compiler_params=_cparams(("arbitrary", "arbitrary", "arbitrary")),
    )(q, k, vt)


def _fold_rows(x, target):
    while x.shape[0] > target:
        h = x.shape[0] // 2
        x = x[:h] + x[h:]
    return x


def _flash_update_t(s, m, acc, vt):
    m_new = jnp.maximum(m, jnp.max(s, axis=0, keepdims=True))
    p = jnp.exp(s - m_new).astype(BF16)
    if isinstance(vt, (list, tuple)):
        w = p.shape[1] // len(vt)
        pv = jnp.concatenate([_nn(v, p[:, g * w:(g + 1) * w]) for g, v in enumerate(vt)], axis=1)
    else:
        pv = _nn(vt, p)
    return m_new, jnp.exp(m - m_new) * acc + pv


def _flash_init(v_rows, r):
    return (jnp.full((1, r), NEG_BIG, F32), jnp.zeros((v_rows, r), F32))


def _flash_pipelined_t(qq, k_ref, vt_block, s_ref, n_blocks, bk, loop_bias, tail_bias):
    last = n_blocks - 1
    n_pairs = last // 2

    def qk(kb, slot):
        off = pl.multiple_of(kb * bk, bk)
        s_ref[slot] = _nt(k_ref[0, pl.ds(off, bk), :], qq)

    def consume(kb, slot, carry, bias_fn):
        s = s_ref[slot]
        if bias_fn is not None:
            s = s + bias_fn(kb)
        return _flash_update_t(s, *carry, vt_block(kb))

    def pair(p, carry):
        qk(2 * p + 1, 1)
        carry = consume(2 * p, 0, carry, loop_bias)
        qk(2 * p + 2, 0)
        return consume(2 * p + 1, 1, carry, loop_bias)

    qk(0, 0)
    carry = lax.fori_loop(0, n_pairs, pair, _flash_init(vt_block(0).shape[0], qq.shape[0]))
    kb = 2 * n_pairs
    qk(jnp.minimum(kb + 1, last), 1)
    carry = consume(kb, 0, carry, tail_bias)
    return lax.cond(kb < last, lambda cr: consume(kb + 1, 1, cr, tail_bias), lambda cr: cr, carry)


def _dsa_select(qx_ref, wt_ref, kx_ref, key_ref, hi_ref, lo_ref, *, qb, nkb, bq, bk, topk):
    halves = _lane_half_masks()

    qx = qx_ref[0]
    wt = wt_ref[0]
    qx_heads = []
    for hx in range(N_IDX_HEADS):
        chunk = qx[:, (hx // 2) * LANES:(hx // 2 + 1) * LANES]
        qx_heads.append(jnp.where(halves[hx % 2], chunk, jnp.zeros_like(chunk)))

    def score_block(kb, masked):
        off = pl.multiple_of(kb * bk, bk)
        kx = kx_ref[0, pl.ds(off, bk), :]
        score = jnp.zeros((bk, bq), F32)
        for hx in range(N_IDX_HEADS):
            r = jnp.maximum(_nt(kx, qx_heads[hx]).astype(BF16), 0.0).astype(F32)
            score = score + r * wt[hx:hx + 1, :]
        bits = lax.bitcast_convert_type(score, I32)
        key = jnp.where(bits < 0, INT_MIN - bits, bits)
        if masked:
            kpos = lax.broadcasted_iota(I32, (bk, bq), 0) + (kb * bk - qb * bq)
            qpos = lax.broadcasted_iota(I32, (bk, bq), 1)
            key = jnp.where(kpos <= qpos, key, INT_MIN)
        key_ref[kb] = key
        hi_ref[kb] = lax.shift_right_arithmetic(key, 16).astype(I16)
        lo_ref[kb] = ((key & (2 * HALF_BIAS - 1)) - HALF_BIAS).astype(I16)

    def score_loop_body(kb, _):
        score_block(kb, False)
        return 0

    lax.fori_loop(0, nkb - 1, score_loop_body, 0)
    score_block(nkb - 1, True)

    q_t = qb * bq + lax.broadcasted_iota(I32, (1, bq), 1)
    k_eff = jnp.minimum(topk, q_t + 1).astype(F32)

    def count16_ge(ref, thr16):
        one, zero = jnp.int16(1), jnp.int16(0)

        def body(kb, acc):
            for c in range(bk // COUNT_ROWS16):
                half = ref[kb, c * COUNT_ROWS16:(c + 1) * COUNT_ROWS16, :]
                acc = acc + _fold_rows(jnp.where(half >= thr16, one, zero), I16_ROWS)
            return acc
        acc = lax.fori_loop(0, nkb, body, jnp.zeros((I16_ROWS, bq), I16))
        return jnp.sum(acc.astype(F32), axis=0, keepdims=True)

    def kth_largest16(ref, kth):
        def bit_step(i, biased):
            cand = biased | lax.shift_left(jnp.int32(1), 15 - i)
            cnt = count16_ge(ref, (cand - HALF_BIAS).astype(I16))
            return jnp.where(cnt >= kth, cand, biased)
        return lax.fori_loop(0, 16, bit_step, jnp.zeros((1, bq), I32)) - HALF_BIAS

    t_hi = kth_largest16(hi_ref, k_eff)
    above = jnp.where(t_hi >= HALF_BIAS - 1, 0.0,
                      count16_ge(hi_ref, jnp.minimum(t_hi + 1, HALF_BIAS - 1).astype(I16)))
    t_hi16 = t_hi.astype(I16)

    def keep_group(kb, _):
        lo_ref[kb] = jnp.where(hi_ref[kb] == t_hi16, lo_ref[kb], jnp.int16(-HALF_BIAS))
        return 0

    lax.fori_loop(0, nkb, keep_group, 0)
    t_lo = kth_largest16(lo_ref, k_eff - above)
    thr = t_hi * (2 * HALF_BIAS) + (t_lo + HALF_BIAS)
    above_lo = jnp.where(t_lo >= HALF_BIAS - 1, 0.0,
                         count16_ge(lo_ref, jnp.minimum(t_lo + 1, HALF_BIAS - 1).astype(I16)))
    need = k_eff - above - above_lo
    return thr, need


def _dsa_mask(key_ref, thr, need, *, nkb, bq, bk, sub):
    rs = lax.broadcasted_iota(I32, (sub, sub), 0)
    cs = lax.broadcasted_iota(I32, (sub, sub), 1)
    earlier = (cs < rs).astype(BF16)

    def mask_block(kb, seen):
        for j in range(bk // sub):
            key = key_ref[kb, j * sub:(j + 1) * sub, :]
            eq = jnp.where(key == thr, 1.0, 0.0)
            rank = _nn(earlier, eq.astype(BF16)) + seen
            take = jnp.where(rank < need, 1, 0)
            sel = (key + take) > thr
            key_ref[kb, j * sub:(j + 1) * sub, :] = lax.bitcast_convert_type(jnp.where(sel, 0.0, NEG_BIG), I32)
            seen = seen + jnp.sum(eq, axis=0, keepdims=True)
        return seen

    lax.fori_loop(0, nkb, mask_block, jnp.zeros((1, bq), F32))


def _dsa_attend(qd_ref, kd_ref, vt_ref, o_ref, key_ref, s_ref, *, nkb, bq, bk):
    lo_half, hi_half = _lane_half_masks()
    n_heads = qd_ref.shape[2] // HEAD_DIM
    qd = qd_ref[0]
    pairs_per_pass = DSA_HEADS_PER_PASS // 2

    def stacked_bias(kb):
        bias = lax.bitcast_convert_type(key_ref[kb], F32)
        return jnp.concatenate([bias] * DSA_HEADS_PER_PASS, axis=1)

    def vt_block(kb):
        return vt_ref[0, :, pl.ds(pl.multiple_of(kb * bk, bk), bk)]

    for g in range(n_heads // DSA_HEADS_PER_PASS):
        stacked = []
        for hp in range(g * pairs_per_pass, (g + 1) * pairs_per_pass):
            chunk = qd[:, hp * LANES:(hp + 1) * LANES]
            zero = jnp.zeros_like(chunk)
            stacked += [jnp.where(lo_half, chunk, zero), jnp.where(hi_half, chunk, zero)]
        qq = jnp.concatenate(stacked, axis=0)

        _, acc = _flash_pipelined_t(qq, kd_ref, vt_block, s_ref, nkb, bk, stacked_bias, stacked_bias)
        o = acc[:HEAD_DIM] / acc[HEAD_DIM:HEAD_DIM + 1]
        for j in range(pairs_per_pass):
            hp = g * pairs_per_pass + j
            pair = jnp.concatenate([o[:, 2 * j * bq:(2 * j + 1) * bq], o[:, (2 * j + 1) * bq:(2 * j + 2) * bq]],
                                   axis=0)
            o_ref[0, :, hp * LANES:(hp + 1) * LANES] = pair.T.astype(o_ref.dtype)


def _dsa_kernel(qd_ref, qx_ref, wt_ref, kd_ref, kx_ref, vt_ref, o_ref, key_ref, *, bq, bk, sub, topk):
    qb = pl.program_id(1)
    nkb = (qb * bq) // bk + 1
    halves16 = pltpu.VMEM(key_ref.shape, I16)
    thr, need = pl.run_scoped(
        functools.partial(_dsa_select, qx_ref, wt_ref, kx_ref, key_ref, qb=qb, nkb=nkb, bq=bq, bk=bk, topk=topk),
        halves16, halves16)
    _dsa_mask(key_ref, thr, need, nkb=nkb, bq=bq, bk=bk, sub=sub)
    pl.run_scoped(functools.partial(_dsa_attend, qd_ref, kd_ref, vt_ref, o_ref, key_ref, nkb=nkb, bq=bq, bk=bk),
                  pltpu.VMEM((2, bk, DSA_HEADS_PER_PASS * bq), F32))


def _dsa_attention(qd, qx, wt, kd2, kx2, vt2, blk, bk, topk):
    bsz, s, wq = qd.shape
    kspec = pl.BlockSpec((1, s, LANES), lambda b, i: (b, 0, 0), pipeline_mode=pl.Buffered(1))
    tiles = (s // bk, bk, blk)
    return pl.pallas_call(
        functools.partial(_dsa_kernel, bq=blk, bk=bk, sub=min(bk, 256), topk=topk),
        grid=(bsz, s // blk),
        in_specs=[pl.BlockSpec((1, blk, wq), lambda b, i: (b, i, 0)),
                  pl.BlockSpec((1, blk, qx.shape[2]), lambda b, i: (b, i, 0)),
                  pl.BlockSpec((1, wt.shape[1], blk), lambda b, i: (b, 0, i)),
                  kspec, kspec,
                  pl.BlockSpec((1, vt2.shape[1], s), lambda b, i: (0, 0, b), pipeline_mode=pl.Buffered(1))],
        out_specs=pl.BlockSpec((1, blk, wq), lambda b, i: (b, i, 0)),
        out_shape=jax.ShapeDtypeStruct((bsz, s, wq), BF16),
        scratch_shapes=[pltpu.VMEM(tiles, I32)],
        name="dsa_attention",
        compiler_params=_cparams(("arbitrary", "arbitrary")),
    )(qd, qx, wt, kd2, kx2, vt2)


def _diff_kernel(q_ref, k_ref, vt_ref, lam_ref, g_ref, o_ref, s_ref, *, bq, bk, lambda_init):
    qi = pl.program_id(2)
    heads = DIFF_HEADS_PER_STEP
    width = 2 * bq
    lo_half, hi_half = _lane_half_masks()
    qqs = []
    for h in range(heads):
        q = q_ref[0, :, h * LANES:(h + 1) * LANES]
        zero = jnp.zeros_like(q)
        qqs.append(jnp.concatenate([jnp.where(lo_half, q, zero), jnp.where(hi_half, q, zero)], axis=0))
    n_full = (qi * bq) // bk

    def qk(kb, slot):
        off = pl.multiple_of(kb * bk, bk)
        for h in range(heads):
            s_ref[slot, :, h * width:(h + 1) * width] = _nt(k_ref[0, pl.ds(off, bk), h * LANES:(h + 1) * LANES], qqs[h])

    def consume(kb, slot, carry):
        off = pl.multiple_of(kb * bk, bk)
        return _flash_update_t(s_ref[slot], *carry, [vt_ref[h, :, pl.ds(off, bk)] for h in range(heads)])

    def consume_diag(n_sub, carry):
        qk(0, 0)
        nk = n_sub * bq
        s = s_ref[1, :nk, :]
        s_last = jnp.where(_tri_mask(bq, bq, 2 * heads, strict=False), s[nk - bq:], NEG_BIG)
        s = s_last if n_sub == 1 else jnp.concatenate([s[:nk - bq], s_last], axis=0)
        off = pl.multiple_of(n_full * bk, bk)
        return _flash_update_t(s, *carry, [vt_ref[h, :, pl.ds(off, nk)] for h in range(heads)])

    def pair(p, carry):
        qk(2 * p + 1, 1)
        carry = consume(2 * p, 0, carry)
        qk(2 * p + 2, 0)
        return consume(2 * p + 1, 1, carry)

    qk(n_full, 1)
    diag_sub = (qi * bq - n_full * bk) // bq
    carry = _switch(diag_sub, [functools.partial(consume_diag, v + 1) for v in range(bk // bq)],
                    _flash_init(vt_ref.shape[1], heads * width))
    carry = lax.fori_loop(0, n_full // 2, pair, carry)
    _, acc_all = lax.cond(n_full % 2 == 1, lambda cr: consume(n_full - 1, 0, cr), lambda cr: cr, carry)

    lp = lam_ref[...]
    lam = (jnp.exp(jnp.sum(lp[0:1] * lp[1:2], axis=1, keepdims=True))
           - jnp.exp(jnp.sum(lp[2:3] * lp[3:4], axis=1, keepdims=True)) + lambda_init)
    for h in range(heads):
        l = acc_all[LANES:LANES + 1, h * width:(h + 1) * width]
        acc = acc_all[:LANES, h * width:(h + 1) * width]
        o = acc[:, :bq] / l[:, :bq] - lam * (acc[:, bq:] / l[:, bq:])
        ms = jnp.mean(o * o, axis=0, keepdims=True)
        o = o * lax.rsqrt(ms + LN_EPS) * g_ref[...] * (1.0 - lambda_init)
        o_ref[0, :, h * LANES:(h + 1) * LANES] = o.T.astype(o_ref.dtype)


def _diff_attention(q, k, vt, lam_params, subln_g, blk, bk, lambda_init):
    bsz, s, w = q.shape
    heads = DIFF_HEADS_PER_STEP
    once = pl.Buffered(1)
    return pl.pallas_call(
        functools.partial(_diff_kernel, bq=blk, bk=bk, lambda_init=lambda_init),
        grid=(bsz, w // (heads * LANES), s // blk),
        in_specs=[pl.BlockSpec((1, blk, heads * LANES), lambda b, h, i: (b, i, h)),
                  pl.BlockSpec((1, s, heads * LANES), lambda b, h, i: (b, 0, h), pipeline_mode=once),
                  pl.BlockSpec((heads, vt.shape[1], s), lambda b, h, i: (h, 0, b), pipeline_mode=once),
                  pl.BlockSpec(lam_params.shape, lambda b, h, i: (0, 0)),
                  pl.BlockSpec((LANES, 1), lambda b, h, i: (0, 0))],
        out_specs=pl.BlockSpec((1, blk, heads * LANES), lambda b, h, i: (b, i, h)),
        out_shape=jax.ShapeDtypeStruct((bsz, s, w), BF16),
        scratch_shapes=[pltpu.VMEM((2, bk, heads * 2 * blk), F32)],
        name="diff_attention",
        compiler_params=_cparams(("arbitrary", "arbitrary", "arbitrary")),
    )(q, k, vt, lam_params, subln_g.reshape(LANES, 1))


def _pick(n, prefs):
    for p in prefs:
        if n % p == 0:
            return p
    return n


def kernel(x, c, positions, w_mod, b_mod, w_in_even, w_out_even, w_in_odd, lam_q1, lam_k1, lam_q2, lam_k2,
           subln_g, w_out_odd, ln_mix_g, ln_mix_b, w_gate, w_up, w_down, ln_ffn_g, ln_ffn_b):
    bsz, s, d = x.shape
    depth = w_mod.shape[0]
    alpha = (2 * depth) ** 0.25
    dff = w_gate.shape[2]
    rows = bsz * s
    tm = _pick(s, (512, 256, 128))
    tf = dff
    blk = _pick(s, (256, 128))
    bk_dsa = _pick(s, (1024, 512, 256))
    bk_diff = _pick(s, (1024, 512, 256))
    bq_diff = _pick(s, (512, 256, 128))
    assert s % tm == 0 and s % blk == 0 and bk_dsa % blk == 0 and bk_diff % bq_diff == 0, (s, tm, blk)
    assert d % (2 * LANES) == 0 and dff % LANES == 0, (d, dff)
    scale = HEAD_DIM ** -0.5

    inv = ROPE_THETA ** (-jnp.arange(0, HEAD_DIM, 2, dtype=F32) / HEAD_DIM)
    ang = positions.astype(F32)[..., None] * inv
    cos_tab, sin_tab = jnp.cos(ang).reshape(rows, -1), jnp.sin(ang).reshape(rows, -1)

    mod = _modulation(c, w_mod.astype(BF16), b_mod)
    xf = x.reshape(rows, d)

    for i in range(depth):
        mod_l = mod[i].reshape(bsz, 6, d)
        if i % 2 == 0:
            w = w_in_even[i // 2]
            n_sb = n_dsa = d // (2 * HEAD_DIM)
            sbw, dsw, ixw = n_sb * HEAD_DIM, n_dsa * HEAD_DIM, N_IDX_HEADS * HEAD_DIM
            offs = [0]
            for width in (sbw, sbw, sbw, dsw, HEAD_DIM, HEAD_DIM, ixw, HEAD_DIM, N_IDX_HEADS):
                offs.append(offs[-1] + width)
            col = lambda j: w[:, offs[j]:offs[j + 1]]
            pad = jnp.zeros((d, LANES - N_IDX_HEADS), w.dtype)
            wp = jnp.concatenate([col(0), col(1), col(2), col(5), col(5), col(8), pad], axis=1)
            wr = jnp.concatenate([col(3), col(6), col(4), col(4), col(7), col(7)], axis=1)
            plain_outs = ((0, sbw, scale), (sbw, sbw, 1.0), (3 * sbw + LANES, LANES, N_IDX_HEADS ** -0.5))
            rope_outs = ((0, dsw, scale), (dsw, ixw, scale), (dsw + ixw, LANES, 1.0),
                         (dsw + ixw + LANES, LANES, 1.0))
            vt_outs = ((2 * sbw, sbw, LANES, 0), (3 * sbw, LANES, HEAD_DIM, ONES_ROWS))
            dts = (BF16, BF16, F32, BF16, BF16, BF16, BF16)
            q_sb, k_sb, wx, qd, qx, kd2, kx2, vt_sb, vt2 = _inproj(
                xf, mod_l, cos_tab, sin_tab, wp.astype(BF16), wr.astype(BF16), _rot_partner(wr).astype(BF16),
                plain_outs, rope_outs, vt_outs, dts, s, tm)
            r3 = lambda t: t.reshape(bsz, s, t.shape[1])
            o_sb = _sb_attention(r3(q_sb), r3(k_sb), vt_sb, blk)
            wt = r3(wx)[:, :, :8].swapaxes(1, 2)
            o_dsa = _dsa_attention(r3(qd), r3(qx), wt, r3(kd2), r3(kx2), vt2, blk, bk_dsa,
                                   min(DSA_TOPK_MAX, s // 4))
            w_out = w_out_even[i // 2].astype(BF16)
            o_list = [o_sb.reshape(rows, sbw), o_dsa.reshape(rows, dsw)]
            w_list = [w_out[:sbw], w_out[sbw:]]
        else:
            j = i // 2
            w = w_in_odd[j]
            dw = w.shape[1] // 3
            lambda_init = 0.8 - 0.6 * math.exp(-0.3 * i)
            wp = w[:, 2 * dw:]
            wr = w[:, :2 * dw]
            rope_outs = ((0, dw, scale), (dw, dw, 1.0))
            vt_outs = ((0, dw, LANES, ONES_ROWS),)
            q_df, k_df, vt = _inproj(
                xf, mod_l, cos_tab, sin_tab, wp.astype(BF16), wr.astype(BF16), _rot_partner(wr).astype(BF16),
                (), rope_outs, vt_outs, (BF16, BF16), s, tm)
            r3 = lambda t: t.reshape(bsz, s, t.shape[1])
            lam_params = jnp.stack([lam_q1[j], lam_k1[j], lam_q2[j], lam_k2[j]]).astype(F32)
            o_df = _diff_attention(r3(q_df), r3(k_df), vt, lam_params, subln_g[j].astype(F32), bq_diff, bk_diff,
                                   lambda_init)
            o_list = [o_df.reshape(rows, dw)]
            w_list = [w_out_odd[j].astype(BF16)]
        xf = _outproj_ln(o_list, w_list, xf, mod_l, ln_mix_g[i], ln_mix_b[i], alpha, s, tm)
        xf = _ffn_ln(xf, mod_l, w_gate[i].astype(BF16), w_up[i].astype(BF16), w_down[i].astype(BF16),
                     ln_ffn_g[i], ln_ffn_b[i], alpha, s, tm, tf)
    return xf.reshape(bsz, s, d)
```

```python
import functools
import math

import jax
import jax.numpy as jnp
from jax import lax
from jax.experimental import pallas as pl
from jax.experimental.pallas import tpu as pltpu

HEAD_DIM = 64
N_IDX_HEADS = 4
DSA_TOPK_MAX = 256
DSA_HEADS_PER_PASS = 4
DIFF_HEADS_PER_STEP = 1
ROPE_THETA = 10000.0
LN_EPS = 1e-5
LANES = 128
NEG_BIG = -1e30
INT_MIN = -2 ** 31
LOG2E = 1.4426950408889634
SB_EXP_FLOOR = 105.0
ONES_ROWS = 16
COUNT_ROWS16 = 128
I16_ROWS = 16
HALF_BIAS = 2 ** 15

F32 = jnp.float32
BF16 = jnp.bfloat16
I32 = jnp.int32
I16 = jnp.int16

VMEM_LIMIT = 56 * 1024 * 1024


def _cparams(sem):
    return pltpu.CompilerParams(dimension_semantics=sem, vmem_limit_bytes=VMEM_LIMIT)


def _nt(a, b):
    return lax.dot_general(a, b, (((1,), (1,)), ((), ())), preferred_element_type=F32)


def _nn(a, b):
    return jnp.dot(a, b, preferred_element_type=F32)


def _layer_norm(v, g, b):
    mu = jnp.mean(v, axis=-1, keepdims=True)
    d = v - mu
    var = jnp.mean(d * d, axis=-1, keepdims=True)
    return d * lax.rsqrt(var + LN_EPS) * g + b


def _mod_kernel(c_ref, w_ref, b_ref, o_ref):
    c = c_ref[...]
    a = (c * jax.nn.sigmoid(c)).astype(BF16)
    o_ref[0] = _nn(a, w_ref[0]) + b_ref[0]


def _modulation(c, w_mod_bf, b_mod):
    depth, d, n = w_mod_bf.shape
    bsz = c.shape[0]
    tn = n // 4
    return pl.pallas_call(
        _mod_kernel,
        grid=(depth, n // tn),
        in_specs=[pl.BlockSpec((bsz, d), lambda l, j: (0, 0)),
                  pl.BlockSpec((1, d, tn), lambda l, j: (l, 0, j)),
                  pl.BlockSpec((1, 1, tn), lambda l, j: (l, 0, j))],
        out_specs=pl.BlockSpec((1, bsz, tn), lambda l, j: (l, 0, j)),
        out_shape=jax.ShapeDtypeStruct((depth, bsz, n), F32),
        name="modulation",
        compiler_params=_cparams(("arbitrary", "arbitrary")),
    )(c, w_mod_bf, b_mod.reshape(depth, 1, n))


def _inproj_kernel(x_ref, mod_ref, cos_ref, sin_ref, wp_ref, wr_ref, wrr_ref, *out_refs,
                   plain_outs, rope_outs, vt_outs):
    x = x_ref[...]
    sh = mod_ref[0, 0:1, :]
    sc = mod_ref[0, 1:2, :]
    h = (x * (1.0 + sc) + sh).astype(BF16)
    plain = _nn(h, wp_ref[...])
    rope = _nn(h, wr_ref[...])
    rot = _nn(h, wrr_ref[...])
    c32, s32 = cos_ref[...], sin_ref[...]
    cos = jnp.concatenate([c32] * 4, axis=1)
    sg = jnp.concatenate([-s32, s32, -s32, s32], axis=1)
    n = 0
    for (c0, width, scale) in plain_outs:
        o = out_refs[n]
        v = plain[:, c0:c0 + width]
        if scale != 1.0:
            v = v * scale
        if o.dtype == F32:
            v = v.astype(BF16).astype(F32)
        o[...] = v.astype(o.dtype)
        n += 1
    for (c0, width, scale) in rope_outs:
        o = out_refs[n]
        for j in range(width // LANES):
            sl = slice(c0 + j * LANES, c0 + (j + 1) * LANES)
            v = rope[:, sl] * cos + rot[:, sl] * sg
            if scale != 1.0:
                v = v * scale
            o[:, j * LANES:(j + 1) * LANES] = v.astype(o.dtype)
        n += 1
    for (c0, width, v_rows, n_ones) in vt_outs:
        o = out_refs[n]
        for j in range(width // LANES):
            vt = plain[:, c0 + j * LANES:c0 + (j + 1) * LANES].T
            o[j, 0:v_rows, :] = vt[:v_rows].astype(o.dtype)
            if n_ones:
                o[j, v_rows:v_rows + n_ones, :] = jnp.ones((n_ones, vt.shape[1]), o.dtype)
        n += 1


def _inproj(xf, mod_l, cos_tab, sin_tab, wp, wr, wrr, plain_outs, rope_outs, vt_outs, out_dtypes,
            rows_per_batch, tm):
    r, d = xf.shape
    tiles_per_batch = rows_per_batch // tm
    widths = [w for (_, w, _) in plain_outs] + [w for (_, w, _) in rope_outs]
    vt_shapes = [(w // LANES, v_rows + n_ones) for (_, w, v_rows, n_ones) in vt_outs]
    row = lambda i: (i, 0)
    const = lambda i: (0, 0)
    return pl.pallas_call(
        functools.partial(_inproj_kernel, plain_outs=plain_outs, rope_outs=rope_outs, vt_outs=vt_outs),
        grid=(r // tm,),
        in_specs=[pl.BlockSpec((tm, d), row),
                  pl.BlockSpec((1, 6, d), lambda i: (i // tiles_per_batch, 0, 0)),
                  pl.BlockSpec((tm, cos_tab.shape[1]), row),
                  pl.BlockSpec((tm, sin_tab.shape[1]), row),
                  pl.BlockSpec(wp.shape, const),
                  pl.BlockSpec(wr.shape, const),
                  pl.BlockSpec(wrr.shape, const)],
        out_specs=([pl.BlockSpec((tm, w), row) for w in widths]
                   + [pl.BlockSpec((c, v, tm), lambda i: (0, 0, i)) for (c, v) in vt_shapes]),
        out_shape=([jax.ShapeDtypeStruct((r, w), dt) for w, dt in zip(widths, out_dtypes)]
                   + [jax.ShapeDtypeStruct((c, v, r), BF16) for (c, v) in vt_shapes]),
        name="inproj",
        compiler_params=_cparams(("arbitrary",)),
    )(xf, mod_l, cos_tab, sin_tab, wp, wr, wrr)


def _rot_partner(w):
    d, n = w.shape
    return w.reshape(d, n // HEAD_DIM, 2, HEAD_DIM // 2)[:, :, ::-1, :].reshape(d, n)


def _post_kernel(*refs, n_in, alpha):
    o_refs = refs[:n_in]
    w_refs = refs[n_in:2 * n_in]
    x_ref, mod_ref, gm_ref, bm_ref, wg_ref, wu_ref, wd_ref, gf_ref, bf_ref, out_ref = refs[2 * n_in:]
    y = _nn(o_refs[0][...], w_refs[0][...])
    for a, w in zip(o_refs[1:], w_refs[1:]):
        y = y + _nn(a[...], w[...])
    x1 = _layer_norm(alpha * x_ref[...] + (1.0 + mod_ref[0, 2:3, :]) * y, gm_ref[...], bm_ref[...])
    h = (x1 * (1.0 + mod_ref[0, 4:5, :]) + mod_ref[0, 3:4, :]).astype(BF16)
    gate = _nn(h, wg_ref[...])
    up = _nn(h, wu_ref[...])
    a = (gate * jax.nn.sigmoid(gate) * up).astype(BF16)
    y = _nn(a, wd_ref[...])
    out_ref[...] = _layer_norm(alpha * x1 + (1.0 + mod_ref[0, 5:6, :]) * y, gf_ref[...], bf_ref[...])


def _post_mixer(o_list, w_list, xf, mod_l, g_mix, b_mix, wg, wu, wd, g_ffn, b_ffn, alpha, rows_per_batch, tm):
    r, d = xf.shape
    tiles_per_batch = rows_per_batch // tm
    row = lambda i: (i, 0)
    const = lambda i: (0, 0)
    resident = pl.Buffered(1)
    vec = pl.BlockSpec((1, d), const)
    n_in = len(o_list)
    return pl.pallas_call(
        functools.partial(_post_kernel, n_in=n_in, alpha=alpha),
        grid=(r // tm,),
        in_specs=([pl.BlockSpec((tm, o.shape[1]), row) for o in o_list]
                  + [pl.BlockSpec(w.shape, const, pipeline_mode=resident) for w in w_list]
                  + [pl.BlockSpec((tm, d), row),
                     pl.BlockSpec((1, 6, d), lambda i: (i // tiles_per_batch, 0, 0)),
                     vec, vec,
                     pl.BlockSpec(wg.shape, const, pipeline_mode=resident),
                     pl.BlockSpec(wu.shape, const, pipeline_mode=resident),
                     pl.BlockSpec(wd.shape, const, pipeline_mode=resident),
                     vec, vec]),
        out_specs=pl.BlockSpec((tm, d), row),
        out_shape=jax.ShapeDtypeStruct((r, d), F32),
        name="post_mixer",
        compiler_params=_cparams(("arbitrary",)),
    )(*o_list, *w_list, xf, mod_l, g_mix.reshape(1, d), b_mix.reshape(1, d), wg, wu, wd,
      g_ffn.reshape(1, d), b_ffn.reshape(1, d))


def _lane_half_masks():
    lane = lax.broadcasted_iota(I32, (1, LANES), 1)
    return lane < HEAD_DIM, lane >= HEAD_DIM


def _tri_mask(n_keys, n_queries, reps, strict):
    kpos = lax.broadcasted_iota(I32, (n_keys, n_queries), 0)
    qpos = lax.broadcasted_iota(I32, (n_keys, n_queries), 1)
    tri = kpos < qpos if strict else kpos <= qpos
    return jnp.concatenate([tri] * reps, axis=1)


def _switch(index, branches, operand):
    def build(lo, hi):
        if hi - lo == 1:
            return branches[lo]
        mid = (lo + hi) // 2
        return lambda x: lax.cond(index < mid, build(lo, mid), build(mid, hi), x)
    return build(0, len(branches))(operand)


def _sb_kernel(q_ref, k_ref, vt_ref, o_ref, *, bq):
    qi = pl.program_id(2)
    q = q_ref[0]
    lo_half, hi_half = _lane_half_masks()
    zero = jnp.zeros_like(q)
    qq = jnp.concatenate([jnp.where(lo_half, q, zero), jnp.where(hi_half, q, zero)], axis=0)
    rs = lax.broadcasted_iota(I32, (bq, bq), 0)
    cs = lax.broadcasted_iota(I32, (bq, bq), 1)
    later = (cs > rs).astype(BF16)

    def block(kb, c, acc, diag):
        off = pl.multiple_of(kb * bq, bq)
        z = _nt(k_ref[0, pl.ds(off, bq), :], qq)
        sp = jnp.maximum(z, 0.0) + jnp.log(1.0 + jnp.exp2(jnp.abs(z) * -LOG2E))
        if diag:
            past = _tri_mask(bq, bq, 2, strict=True)
            sp = jnp.where(past, sp, 0.0)
        tail = _nn(later, sp.astype(BF16))
        w = jnp.exp(z - sp - tail - c)
        if diag:
            w = jnp.where(past, w, 0.0)
        acc = acc + _nn(vt_ref[0, :, pl.ds(off, bq)], w.astype(BF16))
        return c + tail[0:1] + sp[0:1], acc

    c, acc = block(qi, jnp.zeros((1, 2 * bq), F32), jnp.zeros((LANES, 2 * bq), F32), True)

    def more(state):
        kb, c_min, _, _ = state
        return jnp.logical_and(kb >= 0, c_min <= SB_EXP_FLOOR)

    def step(state):
        kb, _, c, acc = state
        c, acc = block(kb, c, acc, False)
        return kb - 1, jnp.min(c), c, acc

    _, _, _, acc = lax.while_loop(more, step, (qi - 1, jnp.min(c), c, acc))
    out_t = jnp.concatenate([acc[:HEAD_DIM, :bq], acc[HEAD_DIM:, bq:]], axis=0)
    o_ref[0] = out_t.T.astype(o_ref.dtype)


def _sb_attention(q, k, vt, blk):
    bsz, s, w = q.shape
    return pl.pallas_call(
        functools.partial(_sb_kernel, bq=blk),
        grid=(bsz, w // LANES, s // blk),
        in_specs=[pl.BlockSpec((1, blk, LANES), lambda b, h, i: (b, i, h)),
                  pl.BlockSpec((1, s, LANES), lambda b, h, i: (b, 0, h)),
                  pl.BlockSpec((1, vt.shape[1], s), lambda b, h, i: (h, 0, b))],
        out_specs=pl.BlockSpec((1, blk, LANES), lambda b, h, i: (b, i, h)),
        out_shape=jax.ShapeDtypeStruct((bsz, s, w), BF16),
        name="sb_attention",
        compiler_params=_cparams(("arbitrary", "arbitrary", "arbitrary")),
    )(q, k, vt)


def _fold_rows(x, target):
    while x.shape[0] > target:
        h = x.shape[0] // 2
        x = x[:h] + x[h:]
    return x


def _flash_update_t(s, m, acc, vt):
    m_new = jnp.maximum(m, jnp.max(s, axis=0, keepdims=True))
    p = jnp.exp(s - m_new).astype(BF16)
    if isinstance(vt, (list, tuple)):
        w = p.shape[1] // len(vt)
        pv = jnp.concatenate([_nn(v, p[:, g * w:(g + 1) * w]) for g, v in enumerate(vt)], axis=1)
    else:
        pv = _nn(vt, p)
    return m_new, jnp.exp(m - m_new) * acc + pv


def _flash_init(v_rows, r):
    return (jnp.full((1, r), NEG_BIG, F32), jnp.zeros((v_rows, r), F32))


def _flash_pipelined_t(qq, k_ref, vt_block, s_ref, n_blocks, bk, loop_bias, tail_bias):
    last = n_blocks - 1
    n_pairs = last // 2

    def qk(kb, slot):
        off = pl.multiple_of(kb * bk, bk)
        s_ref[slot] = _nt(k_ref[0, pl.ds(off, bk), :], qq)

    def consume(kb, slot, carry, bias_fn):
        s = s_ref[slot]
        if bias_fn is not None:
            s = s + bias_fn(kb)
        return _flash_update_t(s, *carry, vt_block(kb))

    def pair(p, carry):
        qk(2 * p + 1, 1)
        carry = consume(2 * p, 0, carry, loop_bias)
        qk(2 * p + 2, 0)
        return consume(2 * p + 1, 1, carry, loop_bias)

    qk(0, 0)
    carry = lax.fori_loop(0, n_pairs, pair, _flash_init(vt_block(0).shape[0], qq.shape[0]))
    kb = 2 * n_pairs
    qk(jnp.minimum(kb + 1, last), 1)
    carry = consume(kb, 0, carry, tail_bias)
    return lax.cond(kb < last, lambda cr: consume(kb + 1, 1, cr, tail_bias), lambda cr: cr, carry)


def _dsa_select(qx_ref, wt_ref, kx_ref, key_ref, hi_ref, lo_ref, *, qb, nkb, bq, bk, topk):
    halves = _lane_half_masks()

    qx = qx_ref[0]
    wt = wt_ref[0]
    qx_heads = []
    for hx in range(N_IDX_HEADS):
        chunk = qx[:, (hx // 2) * LANES:(hx // 2 + 1) * LANES]
        qx_heads.append(jnp.where(halves[hx % 2], chunk, jnp.zeros_like(chunk)))

    def score_block(kb, masked):
        off = pl.multiple_of(kb * bk, bk)
        kx = kx_ref[0, pl.ds(off, bk), :]
        score = jnp.zeros((bk, bq), F32)
        for hx in range(N_IDX_HEADS):
            r = jnp.maximum(_nt(kx, qx_heads[hx]).astype(BF16), 0.0).astype(F32)
            score = score + r * wt[hx:hx + 1, :]
        bits = lax.bitcast_convert_type(score, I32)
        key = jnp.where(bits < 0, INT_MIN - bits, bits)
        if masked:
            kpos = lax.broadcasted_iota(I32, (bk, bq), 0) + (kb * bk - qb * bq)
            qpos = lax.broadcasted_iota(I32, (bk, bq), 1)
            key = jnp.where(kpos <= qpos, key, INT_MIN)
        key_ref[kb] = key
        hi_ref[kb] = lax.shift_right_arithmetic(key, 16).astype(I16)
        lo_ref[kb] = ((key & (2 * HALF_BIAS - 1)) - HALF_BIAS).astype(I16)

    def score_loop_body(kb, _):
        score_block(kb, False)
        return 0

    lax.fori_loop(0, nkb - 1, score_loop_body, 0)
    score_block(nkb - 1, True)

    q_t = qb * bq + lax.broadcasted_iota(I32, (1, bq), 1)
    k_eff = jnp.minimum(topk, q_t + 1).astype(F32)

    def count16_ge(ref, thr16):
        one, zero = jnp.int16(1), jnp.int16(0)

        def body(kb, acc):
            for c in range(bk // COUNT_ROWS16):
                half = ref[kb, c * COUNT_ROWS16:(c + 1) * COUNT_ROWS16, :]
                acc = acc + _fold_rows(jnp.where(half >= thr16, one, zero), I16_ROWS)
            return acc
        acc = lax.fori_loop(0, nkb, body, jnp.zeros((I16_ROWS, bq), I16))
        return jnp.sum(acc.astype(F32), axis=0, keepdims=True)

    def kth_largest16(ref, kth):
        def bit_step(i, biased):
            cand = biased | lax.shift_left(jnp.int32(1), 15 - i)
            cnt = count16_ge(ref, (cand - HALF_BIAS).astype(I16))
            return jnp.where(cnt >= kth, cand, biased)
        return lax.fori_loop(0, 16, bit_step, jnp.zeros((1, bq), I32)) - HALF_BIAS

    t_hi = kth_largest16(hi_ref, k_eff)
    above = jnp.where(t_hi >= HALF_BIAS - 1, 0.0,
                      count16_ge(hi_ref, jnp.minimum(t_hi + 1, HALF_BIAS - 1).astype(I16)))
    t_hi16 = t_hi.astype(I16)

    def keep_group(kb, _):
        lo_ref[kb] = jnp.where(hi_ref[kb] == t_hi16, lo_ref[kb], jnp.int16(-HALF_BIAS))
        return 0

    lax.fori_loop(0, nkb, keep_group, 0)
    t_lo = kth_largest16(lo_ref, k_eff - above)
    thr = t_hi * (2 * HALF_BIAS) + (t_lo + HALF_BIAS)
    above_lo = jnp.where(t_lo >= HALF_BIAS - 1, 0.0,
                         count16_ge(lo_ref, jnp.minimum(t_lo + 1, HALF_BIAS - 1).astype(I16)))
    need = k_eff - above - above_lo
    return thr, need


def _dsa_mask(key_ref, thr, need, *, nkb, bq, bk, sub):
    rs = lax.broadcasted_iota(I32, (sub, sub), 0)
    cs = lax.broadcasted_iota(I32, (sub, sub), 1)
    earlier = (cs < rs).astype(BF16)

    def mask_block(kb, seen):
        for j in range(bk // sub):
            key = key_ref[kb, j * sub:(j + 1) * sub, :]
            eq = jnp.where(key == thr, 1.0, 0.0)
            rank = _nn(earlier, eq.astype(BF16)) + seen
            take = jnp.where(rank < need, 1, 0)
            sel = (key + take) > thr
            key_ref[kb, j * sub:(j + 1) * sub, :] = lax.bitcast_convert_type(jnp.where(sel, 0.0, NEG_BIG), I32)
            seen = seen + jnp.sum(eq, axis=0, keepdims=True)
        return seen

    lax.fori_loop(0, nkb, mask_block, jnp.zeros((1, bq), F32))


def _dsa_attend(qd_ref, kd_ref, vt_ref, o_ref, key_ref, s_ref, *, nkb, bq, bk):
    lo_half, hi_half = _lane_half_masks()
    n_heads = qd_ref.shape[2] // HEAD_DIM
    qd = qd_ref[0]
    pairs_per_pass = DSA_HEADS_PER_PASS // 2

    def stacked_bias(kb):
        bias = lax.bitcast_convert_type(key_ref[kb], F32)
        return jnp.concatenate([bias] * DSA_HEADS_PER_PASS, axis=1)

    def vt_block(kb):
        return vt_ref[0, :, pl.ds(pl.multiple_of(kb * bk, bk), bk)]

    for g in range(n_heads // DSA_HEADS_PER_PASS):
        stacked = []
        for hp in range(g * pairs_per_pass, (g + 1) * pairs_per_pass):
            chunk = qd[:, hp * LANES:(hp + 1) * LANES]
            zero = jnp.zeros_like(chunk)
            stacked += [jnp.where(lo_half, chunk, zero), jnp.where(hi_half, chunk, zero)]
        qq = jnp.concatenate(stacked, axis=0)

        _, acc = _flash_pipelined_t(qq, kd_ref, vt_block, s_ref, nkb, bk, stacked_bias, stacked_bias)
        o = acc[:HEAD_DIM] / acc[HEAD_DIM:HEAD_DIM + 1]
        for j in range(pairs_per_pass):
            hp = g * pairs_per_pass + j
            pair = jnp.concatenate([o[:, 2 * j * bq:(2 * j + 1) * bq], o[:, (2 * j + 1) * bq:(2 * j + 2) * bq]],
                                   axis=0)
            o_ref[0, :, hp * LANES:(hp + 1) * LANES] = pair.T.astype(o_ref.dtype)


def _dsa_kernel(qd_ref, qx_ref, wt_ref, kd_ref, kx_ref, vt_ref, o_ref, key_ref, *, bq, bk, sub, topk):
    qb = pl.program_id(1)
    nkb = (qb * bq) // bk + 1
    halves16 = pltpu.VMEM(key_ref.shape, I16)
    thr, need = pl.run_scoped(
        functools.partial(_dsa_select, qx_ref, wt_ref, kx_ref, key_ref, qb=qb, nkb=nkb, bq=bq, bk=bk, topk=topk),
        halves16, halves16)
    _dsa_mask(key_ref, thr, need, nkb=nkb, bq=bq, bk=bk, sub=sub)
    pl.run_scoped(functools.partial(_dsa_attend, qd_ref, kd_ref, vt_ref, o_ref, key_ref, nkb=nkb, bq=bq, bk=bk),
                  pltpu.VMEM((2, bk, DSA_HEADS_PER_PASS * bq), F32))


def _dsa_attention(qd, qx, wt, kd2, kx2, vt2, blk, bk, topk):
    bsz, s, wq = qd.shape
    kspec = pl.BlockSpec((1, s, LANES), lambda b, i: (b, 0, 0), pipeline_mode=pl.Buffered(1))
    tiles = (s // bk, bk, blk)
    return pl.pallas_call(
        functools.partial(_dsa_kernel, bq=blk, bk=bk, sub=min(bk, 256), topk=topk),
        grid=(bsz, s // blk),
        in_specs=[pl.BlockSpec((1, blk, wq), lambda b, i: (b, i, 0)),
                  pl.BlockSpec((1, blk, qx.shape[2]), lambda b, i: (b, i, 0)),
                  pl.BlockSpec((1, wt.shape[1], blk), lambda b, i: (b, 0, i)),
                  kspec, kspec,
                  pl.BlockSpec((1, vt2.shape[1], s), lambda b, i: (0, 0, b), pipeline_mode=pl.Buffered(1))],
        out_specs=pl.BlockSpec((1, blk, wq), lambda b, i: (b, i, 0)),
        out_shape=jax.ShapeDtypeStruct((bsz, s, wq), BF16),
        scratch_shapes=[pltpu.VMEM(tiles, I32)],
        name="dsa_attention",
        compiler_params=_cparams(("arbitrary", "arbitrary")),
    )(qd, qx, wt, kd2, kx2, vt2)


def _diff_kernel(q_ref, k_ref, vt_ref, lam_ref, g_ref, o_ref, s_ref, *, bq, bk, lambda_init):
    qi = pl.program_id(2)
    heads = DIFF_HEADS_PER_STEP
    width = 2 * bq
    lo_half, hi_half = _lane_half_masks()
    qqs = []
    for h in range(heads):
        q = q_ref[0, :, h * LANES:(h + 1) * LANES]
        zero = jnp.zeros_like(q)
        qqs.append(jnp.concatenate([jnp.where(lo_half, q, zero), jnp.where(hi_half, q, zero)], axis=0))
    n_full = (qi * bq) // bk

    def qk(kb, slot):
        off = pl.multiple_of(kb * bk, bk)
        for h in range(heads):
            s_ref[slot, :, h * width:(h + 1) * width] = _nt(k_ref[0, pl.ds(off, bk), h * LANES:(h + 1) * LANES], qqs[h])

    def consume(kb, slot, carry):
        off = pl.multiple_of(kb * bk, bk)
        return _flash_update_t(s_ref[slot], *carry, [vt_ref[h, :, pl.ds(off, bk)] for h in range(heads)])

    def consume_diag(n_sub, carry):
        qk(0, 0)
        nk = n_sub * bq
        s = s_ref[1, :nk, :]
        s_last = jnp.where(_tri_mask(bq, bq, 2 * heads, strict=False), s[nk - bq:], NEG_BIG)
        s = s_last if n_sub == 1 else jnp.concatenate([s[:nk - bq], s_last], axis=0)
        off = pl.multiple_of(n_full * bk, bk)
        return _flash_update_t(s, *carry, [vt_ref[h, :, pl.ds(off, nk)] for h in range(heads)])

    def pair(p, carry):
        qk(2 * p + 1, 1)
        carry = consume(2 * p, 0, carry)
        qk(2 * p + 2, 0)
        return consume(2 * p + 1, 1, carry)

    qk(n_full, 1)
    diag_sub = (qi * bq - n_full * bk) // bq
    carry = _switch(diag_sub, [functools.partial(consume_diag, v + 1) for v in range(bk // bq)],
                    _flash_init(vt_ref.shape[1], heads * width))
    carry = lax.fori_loop(0, n_full // 2, pair, carry)
    _, acc_all = lax.cond(n_full % 2 == 1, lambda cr: consume(n_full - 1, 0, cr), lambda cr: cr, carry)

    lp = lam_ref[...]
    lam = (jnp.exp(jnp.sum(lp[0:1] * lp[1:2], axis=1, keepdims=True))
           - jnp.exp(jnp.sum(lp[2:3] * lp[3:4], axis=1, keepdims=True)) + lambda_init)
    for h in range(heads):
        l = acc_all[LANES:LANES + 1, h * width:(h + 1) * width]
        acc = acc_all[:LANES, h * width:(h + 1) * width]
        o = acc[:, :bq] / l[:, :bq] - lam * (acc[:, bq:] / l[:, bq:])
        ms = jnp.mean(o * o, axis=0, keepdims=True)
        o = o * lax.rsqrt(ms + LN_EPS) * g_ref[...] * (1.0 - lambda_init)
        o_ref[0, :, h * LANES:(h + 1) * LANES] = o.T.astype(o_ref.dtype)


def _diff_attention(q, k, vt, lam_params, subln_g, blk, bk, lambda_init):
    bsz, s, w = q.shape
    heads = DIFF_HEADS_PER_STEP
    once = pl.Buffered(1)
    return pl.pallas_call(
        functools.partial(_diff_kernel, bq=blk, bk=bk, lambda_init=lambda_init),
        grid=(bsz, w // (heads * LANES), s // blk),
        in_specs=[pl.BlockSpec((1, blk, heads * LANES), lambda b, h, i: (b, i, h)),
                  pl.BlockSpec((1, s, heads * LANES), lambda b, h, i: (b, 0, h), pipeline_mode=once),
                  pl.BlockSpec((heads, vt.shape[1], s), lambda b, h, i: (h, 0, b), pipeline_mode=once),
                  pl.BlockSpec(lam_params.shape, lambda b, h, i: (0, 0)),
                  pl.BlockSpec((LANES, 1), lambda b, h, i: (0, 0))],
        out_specs=pl.BlockSpec((1, blk, heads * LANES), lambda b, h, i: (b, i, h)),
        out_shape=jax.ShapeDtypeStruct((bsz, s, w), BF16),
        scratch_shapes=[pltpu.VMEM((2, bk, heads * 2 * blk), F32)],
        name="diff_attention",
        compiler_params=_cparams(("arbitrary", "arbitrary", "arbitrary")),
    )(q, k, vt, lam_params, subln_g.reshape(LANES, 1))


def _pick(n, prefs):
    for p in prefs:
        if n % p == 0:
            return p
    return n


def kernel(x, c, positions, w_mod, b_mod, w_in_even, w_out_even, w_in_odd, lam_q1, lam_k1, lam_q2, lam_k2,
           subln_g, w_out_odd, ln_mix_g, ln_mix_b, w_gate, w_up, w_down, ln_ffn_g, ln_ffn_b):
    bsz, s, d = x.shape
    depth = w_mod.shape[0]
    alpha = (2 * depth) ** 0.25
    dff = w_gate.shape[2]
    rows = bsz * s
    tm = _pick(s, (512, 256, 128))
    blk = _pick(s, (256, 128))
    bk_dsa = _pick(s, (1024, 512, 256))
    bk_diff = _pick(s, (1024, 512, 256))
    bq_diff = _pick(s, (512, 256, 128))
    assert s % tm == 0 and s % blk == 0 and bk_dsa % blk == 0 and bk_diff % bq_diff == 0, (s, tm, blk)
    assert d % (2 * LANES) == 0 and dff % LANES == 0, (d, dff)
    scale = HEAD_DIM ** -0.5

    inv = ROPE_THETA ** (-jnp.arange(0, HEAD_DIM, 2, dtype=F32) / HEAD_DIM)
    ang = positions.astype(F32)[..., None] * inv
    cos_tab, sin_tab = jnp.cos(ang).reshape(rows, -1), jnp.sin(ang).reshape(rows, -1)

    mod = _modulation(c, w_mod.astype(BF16), b_mod)
    xf = x.reshape(rows, d)

    for i in range(depth):
        mod_l = mod[i].reshape(bsz, 6, d)
        if i % 2 == 0:
            w = w_in_even[i // 2]
            n_sb = n_dsa = d // (2 * HEAD_DIM)
            sbw, dsw, ixw = n_sb * HEAD_DIM, n_dsa * HEAD_DIM, N_IDX_HEADS * HEAD_DIM
            offs = [0]
            for width in (sbw, sbw, sbw, dsw, HEAD_DIM, HEAD_DIM, ixw, HEAD_DIM, N_IDX_HEADS):
                offs.append(offs[-1] + width)
            col = lambda j: w[:, offs[j]:offs[j + 1]]
            pad = jnp.zeros((d, LANES - N_IDX_HEADS), w.dtype)
            wp = jnp.concatenate([col(0), col(1), col(2), col(5), col(5), col(8), pad], axis=1)
            wr = jnp.concatenate([col(3), col(6), col(4), col(4), col(7), col(7)], axis=1)
            plain_outs = ((0, sbw, scale), (sbw, sbw, 1.0), (3 * sbw + LANES, LANES, N_IDX_HEADS ** -0.5))
            rope_outs = ((0, dsw, scale), (dsw, ixw, scale), (dsw + ixw, LANES, 1.0),
                         (dsw + ixw + LANES, LANES, 1.0))
            vt_outs = ((2 * sbw, sbw, LANES, 0), (3 * sbw, LANES, HEAD_DIM, ONES_ROWS))
            dts = (BF16, BF16, F32, BF16, BF16, BF16, BF16)
            q_sb, k_sb, wx, qd, qx, kd2, kx2, vt_sb, vt2 = _inproj(
                xf, mod_l, cos_tab, sin_tab, wp.astype(BF16), wr.astype(BF16), _rot_partner(wr).astype(BF16),
                plain_outs, rope_outs, vt_outs, dts, s, tm)
            r3 = lambda t: t.reshape(bsz, s, t.shape[1])
            o_sb = _sb_attention(r3(q_sb), r3(k_sb), vt_sb, blk)
            wt = r3(wx)[:, :, :8].swapaxes(1, 2)
            o_dsa = _dsa_attention(r3(qd), r3(qx), wt, r3(kd2), r3(kx2), vt2, blk, bk_dsa,
                                   min(DSA_TOPK_MAX, s // 4))
            w_out = w_out_even[i // 2].astype(BF16)
            o_list = [o_sb.reshape(rows, sbw), o_dsa.reshape(rows, dsw)]
            w_list = [w_out[:sbw], w_out[sbw:]]
        else:
            j = i // 2
            w = w_in_odd[j]
            dw = w.shape[1] // 3
            lambda_init = 0.8 - 0.6 * math.exp(-0.3 * i)
            wp = w[:, 2 * dw:]
            wr = w[:, :2 * dw]
            rope_outs = ((0, dw, scale), (dw, dw, 1.0))
            vt_outs = ((0, dw, LANES, ONES_ROWS),)
            q_df, k_df, vt = _inproj(
                xf, mod_l, cos_tab, sin_tab, wp.astype(BF16), wr.astype(BF16), _rot_partner(wr).astype(BF16),
                (), rope_outs, vt_outs, (BF16, BF16), s, tm)
            r3 = lambda t: t.reshape(bsz, s, t.shape[1])
            lam_params = jnp.stack([lam_q1[j], lam_k1[j], lam_q2[j], lam_k2[j]]).astype(F32)
            o_df = _diff_attention(r3(q_df), r3(k_df), vt, lam_params, subln_g[j].astype(F32), bq_diff, bk_diff,
                                   lambda_init)
            o_list = [o_df.reshape(rows, dw)]
            w_list = [w_out_odd[j].astype(BF16)]
        xf = _post_mixer(o_list, w_list, xf, mod_l, ln_mix_g[i], ln_mix_b[i], w_gate[i].astype(BF16),
                         w_up[i].astype(BF16), w_down[i].astype(BF16), ln_ffn_g[i], ln_ffn_b[i], alpha, s, tm)
    return xf.reshape(bsz, s, d)
```

```python
import functools
import math

import jax
import jax.numpy as jnp
from jax import lax
from jax.experimental import pallas as pl
from jax.experimental.pallas import tpu as pltpu

HEAD_DIM = 64
N_IDX_HEADS = 4
DSA_TOPK_MAX = 256
DSA_HEADS_PER_PASS = 4
DIFF_HEADS_PER_STEP = 1
ROPE_THETA = 10000.0
LN_EPS = 1e-5
LANES = 128
NEG_BIG = -1e30
INT_MIN = -2 ** 31
LOG2E = 1.4426950408889634
SB_EXP_FLOOR = 105.0
ONES_ROWS = 16
COUNT_ROWS16 = 128
I16_ROWS = 16
HALF_BIAS = 2 ** 15

F32 = jnp.float32
BF16 = jnp.bfloat16
I32 = jnp.int32
I16 = jnp.int16

VMEM_LIMIT = 56 * 1024 * 1024


def _cparams(sem):
    return pltpu.CompilerParams(dimension_semantics=sem, vmem_limit_bytes=VMEM_LIMIT)


def _nt(a, b):
    return lax.dot_general(a, b, (((1,), (1,)), ((), ())), preferred_element_type=F32)


def _nn(a, b):
    return jnp.dot(a, b, preferred_element_type=F32)


def _layer_norm(v, g, b):
    mu = jnp.mean(v, axis=-1, keepdims=True)
    d = v - mu
    var = jnp.mean(d * d, axis=-1, keepdims=True)
    return d * lax.rsqrt(var + LN_EPS) * g + b


def _mod_kernel(c_ref, w_ref, b_ref, o_ref):
    c = c_ref[...]
    a = (c * jax.nn.sigmoid(c)).astype(BF16)
    o_ref[0] = _nn(a, w_ref[0]) + b_ref[0]


def _modulation(c, w_mod_bf, b_mod):
    depth, d, n = w_mod_bf.shape
    bsz = c.shape[0]
    tn = n // 4
    return pl.pallas_call(
        _mod_kernel,
        grid=(depth, n // tn),
        in_specs=[pl.BlockSpec((bsz, d), lambda l, j: (0, 0)),
                  pl.BlockSpec((1, d, tn), lambda l, j: (l, 0, j)),
                  pl.BlockSpec((1, 1, tn), lambda l, j: (l, 0, j))],
        out_specs=pl.BlockSpec((1, bsz, tn), lambda l, j: (l, 0, j)),
        out_shape=jax.ShapeDtypeStruct((depth, bsz, n), F32),
        name="modulation",
        compiler_params=_cparams(("arbitrary", "arbitrary")),
    )(c, w_mod_bf, b_mod.reshape(depth, 1, n))


def _inproj_kernel(x_ref, mod_ref, cos_ref, sin_ref, wp_ref, wr_ref, wrr_ref, *out_refs,
                   plain_outs, rope_outs, vt_outs):
    x = x_ref[...]
    sh = mod_ref[0, 0:1, :]
    sc = mod_ref[0, 1:2, :]
    h = (x * (1.0 + sc) + sh).astype(BF16)
    plain = _nn(h, wp_ref[...])
    rope = _nn(h, wr_ref[...])
    rot = _nn(h, wrr_ref[...])
    c32, s32 = cos_ref[...], sin_ref[...]
    cos = jnp.concatenate([c32] * 4, axis=1)
    sg = jnp.concatenate([-s32, s32, -s32, s32], axis=1)
    n = 0
    for (c0, width, scale) in plain_outs:
        o = out_refs[n]
        v = plain[:, c0:c0 + width]
        if scale != 1.0:
            v = v * scale
        if o.dtype == F32:
            v = v.astype(BF16).astype(F32)
        o[...] = v.astype(o.dtype)
        n += 1
    for (c0, width, scale) in rope_outs:
        o = out_refs[n]
        for j in range(width // LANES):
            sl = slice(c0 + j * LANES, c0 + (j + 1) * LANES)
            v = rope[:, sl] * cos + rot[:, sl] * sg
            if scale != 1.0:
                v = v * scale
            o[:, j * LANES:(j + 1) * LANES] = v.astype(o.dtype)
        n += 1
    for (c0, width, v_rows, n_ones) in vt_outs:
        o = out_refs[n]
        for j in range(width // LANES):
            vt = plain[:, c0 + j * LANES:c0 + (j + 1) * LANES].T
            o[j, 0:v_rows, :] = vt[:v_rows].astype(o.dtype)
            if n_ones:
                o[j, v_rows:v_rows + n_ones, :] = jnp.ones((n_ones, vt.shape[1]), o.dtype)
        n += 1


def _inproj(xf, mod_l, cos_tab, sin_tab, wp, wr, wrr, plain_outs, rope_outs, vt_outs, out_dtypes,
            rows_per_batch, tm):
    r, d = xf.shape
    tiles_per_batch = rows_per_batch // tm
    widths = [w for (_, w, _) in plain_outs] + [w for (_, w, _) in rope_outs]
    vt_shapes = [(w // LANES, v_rows + n_ones) for (_, w, v_rows, n_ones) in vt_outs]
    row = lambda i: (i, 0)
    const = lambda i: (0, 0)
    return pl.pallas_call(
        functools.partial(_inproj_kernel, plain_outs=plain_outs, rope_outs=rope_outs, vt_outs=vt_outs),
        grid=(r // tm,),
        in_specs=[pl.BlockSpec((tm, d), row),
                  pl.BlockSpec((1, 6, d), lambda i: (i // tiles_per_batch, 0, 0)),
                  pl.BlockSpec((tm, cos_tab.shape[1]), row),
                  pl.BlockSpec((tm, sin_tab.shape[1]), row),
                  pl.BlockSpec(wp.shape, const),
                  pl.BlockSpec(wr.shape, const),
                  pl.BlockSpec(wrr.shape, const)],
        out_specs=([pl.BlockSpec((tm, w), row) for w in widths]
                   + [pl.BlockSpec((c, v, tm), lambda i: (0, 0, i)) for (c, v) in vt_shapes]),
        out_shape=([jax.ShapeDtypeStruct((r, w), dt) for w, dt in zip(widths, out_dtypes)]
                   + [jax.ShapeDtypeStruct((c, v, r), BF16) for (c, v) in vt_shapes]),
        name="inproj",
        compiler_params=_cparams(("arbitrary",)),
    )(xf, mod_l, cos_tab, sin_tab, wp, wr, wrr)


def _rot_partner(w):
    d, n = w.shape
    return w.reshape(d, n // HEAD_DIM, 2, HEAD_DIM // 2)[:, :, ::-1, :].reshape(d, n)


def _post_kernel(*refs, n_in, alpha):
    o_refs = refs[:n_in]
    w_refs = refs[n_in:2 * n_in]
    x_ref, mod_ref, gm_ref, bm_ref, wg_ref, wu_ref, wd_ref, gf_ref, bf_ref, out_ref = refs[2 * n_in:]
    y = _nn(o_refs[0][...], w_refs[0][...])
    for a, w in zip(o_refs[1:], w_refs[1:]):
        y = y + _nn(a[...], w[...])
    x1 = _layer_norm(alpha * x_ref[...] + (1.0 + mod_ref[0, 2:3, :]) * y, gm_ref[...], bm_ref[...])
    h = (x1 * (1.0 + mod_ref[0, 4:5, :]) + mod_ref[0, 3:4, :]).astype(BF16)
    gate = _nn(h, wg_ref[...])
    up = _nn(h, wu_ref[...])
    a = (gate * jax.nn.sigmoid(gate) * up).astype(BF16)
    y = _nn(a, wd_ref[...])
    out_ref[...] = _layer_norm(alpha * x1 + (1.0 + mod_ref[0, 5:6, :]) * y, gf_ref[...], bf_ref[...])


def _post_mixer(o_list, w_list, xf, mod_l, g_mix, b_mix, wg, wu, wd, g_ffn, b_ffn, alpha, rows_per_batch, tm):
    r, d = xf.shape
    tiles_per_batch = rows_per_batch // tm
    row = lambda i: (i, 0)
    const = lambda i: (0, 0)
    resident = pl.Buffered(1)
    vec = pl.BlockSpec((1, d), const)
    n_in = len(o_list)
    return pl.pallas_call(
        functools.partial(_post_kernel, n_in=n_in, alpha=alpha),
        grid=(r // tm,),
        in_specs=([pl.BlockSpec((tm, o.shape[1]), row) for o in o_list]
                  + [pl.BlockSpec(w.shape, const, pipeline_mode=resident) for w in w_list]
                  + [pl.BlockSpec((tm, d), row),
                     pl.BlockSpec((1, 6, d), lambda i: (i // tiles_per_batch, 0, 0)),
                     vec, vec,
                     pl.BlockSpec(wg.shape, const, pipeline_mode=resident),
                     pl.BlockSpec(wu.shape, const, pipeline_mode=resident),
                     pl.BlockSpec(wd.shape, const, pipeline_mode=resident),
                     vec, vec]),
        out_specs=pl.BlockSpec((tm, d), row),
        out_shape=jax.ShapeDtypeStruct((r, d), F32),
        name="post_mixer",
        compiler_params=_cparams(("arbitrary",)),
    )(*o_list, *w_list, xf, mod_l, g_mix.reshape(1, d), b_mix.reshape(1, d), wg, wu, wd,
      g_ffn.reshape(1, d), b_ffn.reshape(1, d))


def _lane_half_masks():
    lane = lax.broadcasted_iota(I32, (1, LANES), 1)
    return lane < HEAD_DIM, lane >= HEAD_DIM


def _tri_mask(n_keys, n_queries, reps, strict):
    kpos = lax.broadcasted_iota(I32, (n_keys, n_queries), 0)
    qpos = lax.broadcasted_iota(I32, (n_keys, n_queries), 1)
    tri = kpos < qpos if strict else kpos <= qpos
    return jnp.concatenate([tri] * reps, axis=1)


def _switch(index, branches, operand):
    def build(lo, hi):
        if hi - lo == 1:
            return branches[lo]
        mid = (lo + hi) // 2
        return lambda x: lax.cond(index < mid, build(lo, mid), build(mid, hi), x)
    return build(0, len(branches))(operand)


def _sb_kernel(q_ref, k_ref, vt_ref, o_ref, *, bq):
    qi = pl.program_id(2)
    q = q_ref[0]
    lo_half, hi_half = _lane_half_masks()
    zero = jnp.zeros_like(q)
    qq = jnp.concatenate([jnp.where(lo_half, q, zero), jnp.where(hi_half, q, zero)], axis=0)
    rs = lax.broadcasted_iota(I32, (bq, bq), 0)
    cs = lax.broadcasted_iota(I32, (bq, bq), 1)
    later = (cs > rs).astype(BF16)

    def block(kb, c, acc, diag):
        off = pl.multiple_of(kb * bq, bq)
        z = _nt(k_ref[0, pl.ds(off, bq), :], qq)
        sp = jnp.maximum(z, 0.0) + jnp.log(1.0 + jnp.exp2(jnp.abs(z) * -LOG2E))
        if diag:
            past = _tri_mask(bq, bq, 2, strict=True)
            sp = jnp.where(past, sp, 0.0)
        tail = _nn(later, sp.astype(BF16))
        w = jnp.exp(z - sp - tail - c)
        if diag:
            w = jnp.where(past, w, 0.0)
        acc = acc + _nn(vt_ref[0, :, pl.ds(off, bq)], w.astype(BF16))
        return c + tail[0:1] + sp[0:1], acc

    c, acc = block(qi, jnp.zeros((1, 2 * bq), F32), jnp.zeros((LANES, 2 * bq), F32), True)

    def more(state):
        kb, c_min, _, _ = state
        return jnp.logical_and(kb >= 0, c_min <= SB_EXP_FLOOR)

    def step(state):
        kb, _, c, acc = state
        c, acc = block(kb, c, acc, False)
        return kb - 1, jnp.min(c), c, acc

    _, _, _, acc = lax.while_loop(more, step, (qi - 1, jnp.min(c), c, acc))
    out_t = jnp.concatenate([acc[:HEAD_DIM, :bq], acc[HEAD_DIM:, bq:]], axis=0)
    o_ref[0] = out_t.T.astype(o_ref.dtype)


def _sb_attention(q, k, vt, blk):
    bsz, s, w = q.shape
    return pl.pallas_call(
        functools.partial(_sb_kernel, bq=blk),
        grid=(bsz, w // LANES, s // blk),
        in_specs=[pl.BlockSpec((1, blk, LANES), lambda b, h, i: (b, i, h)),
                  pl.BlockSpec((1, s, LANES), lambda b, h, i: (b, 0, h)),
                  pl.BlockSpec((1, vt.shape[1], s), lambda b, h, i: (h, 0, b))],
        out_specs=pl.BlockSpec((1, blk, LANES), lambda b, h, i: (b, i, h)),
        out_shape=jax.ShapeDtypeStruct((bsz, s, w), BF16),
        name="sb_attention",
        compiler_params=_cparams(("arbitrary", "arbitrary", "arbitrary")),
    )(q, k, vt)


def _fold_rows(x, target):
    while x.shape[0] > target:
        h = x.shape[0] // 2
        x = x[:h] + x[h:]
    return x


def _flash_update_t(s, m, acc, vt):
    m_new = jnp.maximum(m, jnp.max(s, axis=0, keepdims=True))
    p = jnp.exp(s - m_new).astype(BF16)
    if isinstance(vt, (list, tuple)):
        w = p.shape[1] // len(vt)
        pv = jnp.concatenate([_nn(v, p[:, g * w:(g + 1) * w]) for g, v in enumerate(vt)], axis=1)
    else:
        pv = _nn(vt, p)
    return m_new, jnp.exp(m - m_new) * acc + pv


def _flash_init(v_rows, r):
    return (jnp.full((1, r), NEG_BIG, F32), jnp.zeros((v_rows, r), F32))


def _flash_pipelined_t(qq, k_ref, vt_block, s_ref, n_blocks, bk, loop_bias, tail_bias):
    last = n_blocks - 1
    n_pairs = last // 2

    def qk(kb, slot):
        off = pl.multiple_of(kb * bk, bk)
        s_ref[slot] = _nt(k_ref[0, pl.ds(off, bk), :], qq)

    def consume(kb, slot, carry, bias_fn):
        s = s_ref[slot]
        if bias_fn is not None:
            s = s + bias_fn(kb)
        return _flash_update_t(s, *carry, vt_block(kb))

    def pair(p, carry):
        qk(2 * p + 1, 1)
        carry = consume(2 * p, 0, carry, loop_bias)
        qk(2 * p + 2, 0)
        return consume(2 * p + 1, 1, carry, loop_bias)

    qk(0, 0)
    carry = lax.fori_loop(0, n_pairs, pair, _flash_init(vt_block(0).shape[0], qq.shape[0]))
    kb = 2 * n_pairs
    qk(jnp.minimum(kb + 1, last), 1)
    carry = consume(kb, 0, carry, tail_bias)
    return lax.cond(kb < last, lambda cr: consume(kb + 1, 1, cr, tail_bias), lambda cr: cr, carry)


def _dsa_select(qx_ref, wt_ref, kx_ref, key_ref, hi_ref, lo_ref, *, qb, nkb, bq, bk, topk):
    halves = _lane_half_masks()

    qx = qx_ref[0]
    wt = wt_ref[0]
    qx_heads = []
    for hx in range(N_IDX_HEADS):
        chunk = qx[:, (hx // 2) * LANES:(hx // 2 + 1) * LANES]
        qx_heads.append(jnp.where(halves[hx % 2], chunk, jnp.zeros_like(chunk)))

    def score_block(kb, masked):
        off = pl.multiple_of(kb * bk, bk)
        kx = kx_ref[0, pl.ds(off, bk), :]
        score = jnp.zeros((bk, bq), F32)
        for hx in range(N_IDX_HEADS):
            r = jnp.maximum(_nt(kx, qx_heads[hx]).astype(BF16), 0.0).astype(F32)
            score = score + r * wt[hx:hx + 1, :]
        bits = lax.bitcast_convert_type(score, I32)
        key = jnp.where(bits < 0, INT_MIN - bits, bits)
        if masked:
            kpos = lax.broadcasted_iota(I32, (bk, bq), 0) + (kb * bk - qb * bq)
            qpos = lax.broadcasted_iota(I32, (bk, bq), 1)
            key = jnp.where(kpos <= qpos, key, INT_MIN)
        key_ref[kb] = key
        hi_ref[kb] = lax.shift_right_arithmetic(key, 16).astype(I16)
        lo_ref[kb] = ((key & (2 * HALF_BIAS - 1)) - HALF_BIAS).astype(I16)

    def score_loop_body(kb, _):
        score_block(kb, False)
        return 0

    lax.fori_loop(0, nkb - 1, score_loop_body, 0)
    score_block(nkb - 1, True)

    q_t = qb * bq + lax.broadcasted_iota(I32, (1, bq), 1)
    k_eff = jnp.minimum(topk, q_t + 1).astype(F32)

    def count16_ge(ref, thr16):
        one, zero = jnp.int16(1), jnp.int16(0)

        def body(kb, acc):
            for c in range(bk // COUNT_ROWS16):
                half = ref[kb, c * COUNT_ROWS16:(c + 1) * COUNT_ROWS16, :]
                acc = acc + _fold_rows(jnp.where(half >= thr16, one, zero), I16_ROWS)
            return acc
        acc = lax.fori_loop(0, nkb, body, jnp.zeros((I16_ROWS, bq), I16))
        return jnp.sum(acc.astype(F32), axis=0, keepdims=True)

    def kth_largest16(ref, kth):
        def bit_step(i, biased):
            cand = biased | lax.shift_left(jnp.int32(1), 15 - i)
            cnt = count16_ge(ref, (cand - HALF_BIAS).astype(I16))
            return jnp.where(cnt >= kth, cand, biased)
        return lax.fori_loop(0, 16, bit_step, jnp.zeros((1, bq), I32)) - HALF_BIAS

    t_hi = kth_largest16(hi_ref, k_eff)
    above = jnp.where(t_hi >= HALF_BIAS - 1, 0.0,
                      count16_ge(hi_ref, jnp.minimum(t_hi + 1, HALF_BIAS - 1).astype(I16)))
    t_hi16 = t_hi.astype(I16)

    def keep_group(kb, _):
        lo_ref[kb] = jnp.where(hi_ref[kb] == t_hi16, lo_ref[kb], jnp.int16(-HALF_BIAS))
        return 0

    lax.fori_loop(0, nkb, keep_group, 0)
    t_lo = kth_largest16(lo_ref, k_eff - above)
    thr = t_hi * (2 * HALF_BIAS) + (t_lo + HALF_BIAS)
    above_lo = jnp.where(t_lo >= HALF_BIAS - 1, 0.0,
                         count16_ge(lo_ref, jnp.minimum(t_lo + 1, HALF_BIAS - 1).astype(I16)))
    need = k_eff - above - above_lo
    return thr, need


def _dsa_mask(key_ref, thr, need, *, nkb, bq, bk, sub):
    rs = lax.broadcasted_iota(I32, (sub, sub), 0)
    cs = lax.broadcasted_iota(I32, (sub, sub), 1)
    earlier = (cs < rs).astype(BF16)

    def mask_block(kb, seen):
        for j in range(bk // sub):
            key = key_ref[kb, j * sub:(j + 1) * sub, :]
            eq = jnp.where(key == thr, 1.0, 0.0)
            rank = _nn(earlier, eq.astype(BF16)) + seen
            take = jnp.where(rank < need, 1, 0)
            sel = (key + take) > thr
            key_ref[kb, j * sub:(j + 1) * sub, :] = lax.bitcast_convert_type(jnp.where(sel, 0.0, NEG_BIG), I32)
            seen = seen + jnp.sum(eq, axis=0, keepdims=True)
        return seen

    lax.fori_loop(0, nkb, mask_block, jnp.zeros((1, bq), F32))


def _dsa_attend(qd_ref, kd_ref, vt_ref, o_ref, key_ref, s_ref, *, nkb, bq, bk):
    lo_half, hi_half = _lane_half_masks()
    n_heads = qd_ref.shape[2] // HEAD_DIM
    qd = qd_ref[0]
    pairs_per_pass = DSA_HEADS_PER_PASS // 2

    def stacked_bias(kb):
        bias = lax.bitcast_convert_type(key_ref[kb], F32)
        return jnp.concatenate([bias] * DSA_HEADS_PER_PASS, axis=1)

    def vt_block(kb):
        return vt_ref[0, :, pl.ds(pl.multiple_of(kb * bk, bk), bk)]

    for g in range(n_heads // DSA_HEADS_PER_PASS):
        stacked = []
        for hp in range(g * pairs_per_pass, (g + 1) * pairs_per_pass):
            chunk = qd[:, hp * LANES:(hp + 1) * LANES]
            zero = jnp.zeros_like(chunk)
            stacked += [jnp.where(lo_half, chunk, zero), jnp.where(hi_half, chunk, zero)]
        qq = jnp.concatenate(stacked, axis=0)

        _, acc = _flash_pipelined_t(qq, kd_ref, vt_block, s_ref, nkb, bk, stacked_bias, stacked_bias)
        o = acc[:HEAD_DIM] / acc[HEAD_DIM:HEAD_DIM + 1]
        for j in range(pairs_per_pass):
            hp = g * pairs_per_pass + j
            pair = jnp.concatenate([o[:, 2 * j * bq:(2 * j + 1) * bq], o[:, (2 * j + 1) * bq:(2 * j + 2) * bq]],
                                   axis=0)
            o_ref[0, :, hp * LANES:(hp + 1) * LANES] = pair.T.astype(o_ref.dtype)


def _dsa_kernel(qd_ref, qx_ref, wt_ref, kd_ref, kx_ref, vt_ref, o_ref, key_ref, *, bq, bk, sub, topk):
    qb = pl.program_id(1)
    nkb = (qb * bq) // bk + 1
    halves16 = pltpu.VMEM(key_ref.shape, I16)
    thr, need = pl.run_scoped(
        functools.partial(_dsa_select, qx_ref, wt_ref, kx_ref, key_ref, qb=qb, nkb=nkb, bq=bq, bk=bk, topk=topk),
        halves16, halves16)
    _dsa_mask(key_ref, thr, need, nkb=nkb, bq=bq, bk=bk, sub=sub)
    pl.run_scoped(functools.partial(_dsa_attend, qd_ref, kd_ref, vt_ref, o_ref, key_ref, nkb=nkb, bq=bq, bk=bk),
                  pltpu.VMEM((2, bk, DSA_HEADS_PER_PASS * bq), F32))


def _dsa_attention(qd, qx, wt, kd2, kx2, vt2, blk, bk, topk):
    bsz, s, wq = qd.shape
    kspec = pl.BlockSpec((1, s, LANES), lambda b, i: (b, 0, 0), pipeline_mode=pl.Buffered(1))
    tiles = (s // bk, bk, blk)
    return pl.pallas_call(
        functools.partial(_dsa_kernel, bq=blk, bk=bk, sub=min(bk, 256), topk=topk),
        grid=(bsz, s // blk),
        in_specs=[pl.BlockSpec((1, blk, wq), lambda b, i: (b, i, 0)),
                  pl.BlockSpec((1, blk, qx.shape[2]), lambda b, i: (b, i, 0)),
                  pl.BlockSpec((1, wt.shape[1], blk), lambda b, i: (b, 0, i)),
                  kspec, kspec,
                  pl.BlockSpec((1, vt2.shape[1], s), lambda b, i: (0, 0, b), pipeline_mode=pl.Buffered(1))],
        out_specs=pl.BlockSpec((1, blk, wq), lambda b, i: (b, i, 0)),
        out_shape=jax.ShapeDtypeStruct((bsz, s, wq), BF16),
        scratch_shapes=[pltpu.VMEM(tiles, I32)],
        name="dsa_attention",
        compiler_params=_cparams(("arbitrary", "arbitrary")),
    )(qd, qx, wt, kd2, kx2, vt2)


def _diff_kernel(q_ref, k_ref, vt_ref, lam_ref, g_ref, o_ref, s_ref, *, bq, bk, lambda_init):
    qi = pl.program_id(2)
    heads = DIFF_HEADS_PER_STEP
    width = 2 * bq
    lo_half, hi_half = _lane_half_masks()
    qqs = []
    for h in range(heads):
        q = q_ref[0, :, h * LANES:(h + 1) * LANES]
        zero = jnp.zeros_like(q)
        qqs.append(jnp.concatenate([jnp.where(lo_half, q, zero), jnp.where(hi_half, q, zero)], axis=0))
    n_full = (qi * bq) // bk

    def qk(kb, slot):
        off = pl.multiple_of(kb * bk, bk)
        for h in range(heads):
            s_ref[slot, :, h * width:(h + 1) * width] = _nt(k_ref[0, pl.ds(off, bk), h * LANES:(h + 1) * LANES], qqs[h])

    def consume(kb, slot, carry):
        off = pl.multiple_of(kb * bk, bk)
        return _flash_update_t(s_ref[slot], *carry, [vt_ref[h, :, pl.ds(off, bk)] for h in range(heads)])

    def consume_diag(n_sub, carry):
        qk(0, 0)
        nk = n_sub * bq
        s = s_ref[1, :nk, :]
        s_last = jnp.where(_tri_mask(bq, bq, 2 * heads, strict=False), s[nk - bq:], NEG_BIG)
        s = s_last if n_sub == 1 else jnp.concatenate([s[:nk - bq], s_last], axis=0)
        off = pl.multiple_of(n_full * bk, bk)
        return _flash_update_t(s, *carry, [vt_ref[h, :, pl.ds(off, nk)] for h in range(heads)])

    def pair(p, carry):
        qk(2 * p + 1, 1)
        carry = consume(2 * p, 0, carry)
        qk(2 * p + 2, 0)
        return consume(2 * p + 1, 1, carry)

    qk(n_full, 1)
    diag_sub = (qi * bq - n_full * bk) // bq
    carry = _switch(diag_sub, [functools.partial(consume_diag, v + 1) for v in range(bk // bq)],
                    _flash_init(vt_ref.shape[1], heads * width))
    carry = lax.fori_loop(0, n_full // 2, pair, carry)
    _, acc_all = lax.cond(n_full % 2 == 1, lambda cr: consume(n_full - 1, 0, cr), lambda cr: cr, carry)

    lp = lam_ref[...]
    lam = (jnp.exp(jnp.sum(lp[0:1] * lp[1:2], axis=1, keepdims=True))
           - jnp.exp(jnp.sum(lp[2:3] * lp[3:4], axis=1, keepdims=True)) + lambda_init)
    for h in range(heads):
        l = acc_all[LANES:LANES + 1, h * width:(h + 1) * width]
        acc = acc_all[:LANES, h * width:(h + 1) * width]
        o = acc[:, :bq] / l[:, :bq] - lam * (acc[:, bq:] / l[:, bq:])
        ms = jnp.mean(o * o, axis=0, keepdims=True)
        o = o * lax.rsqrt(ms + LN_EPS) * g_ref[...] * (1.0 - lambda_init)
        o_ref[0, :, h * LANES:(h + 1) * LANES] = o.T.astype(o_ref.dtype)


def _diff_attention(q, k, vt, lam_params, subln_g, blk, bk, lambda_init):
    bsz, s, w = q.shape
    heads = DIFF_HEADS_PER_STEP
    once = pl.Buffered(1)
    return pl.pallas_call(
        functools.partial(_diff_kernel, bq=blk, bk=bk, lambda_init=lambda_init),
        grid=(bsz, w // (heads * LANES), s // blk),
        in_specs=[pl.BlockSpec((1, blk, heads * LANES), lambda b, h, i: (b, i, h)),
                  pl.BlockSpec((1, s, heads * LANES), lambda b, h, i: (b, 0, h), pipeline_mode=once),
                  pl.BlockSpec((heads, vt.shape[1], s), lambda b, h, i: (h, 0, b), pipeline_mode=once),
                  pl.BlockSpec(lam_params.shape, lambda b, h, i: (0, 0)),
                  pl.BlockSpec((LANES, 1), lambda b, h, i: (0, 0))],
        out_specs=pl.BlockSpec((1, blk, heads * LANES), lambda b, h, i: (b, i, h)),
        out_shape=jax.ShapeDtypeStruct((bsz, s, w), BF16),
        scratch_shapes=[pltpu.VMEM((2, bk, heads * 2 * blk), F32)],
        name="diff_attention",
        compiler_params=_cparams(("arbitrary", "arbitrary", "arbitrary")),
    )(q, k, vt, lam_params, subln_g.reshape(LANES, 1))


def _pick(n, prefs):
    for p in prefs:
        if n % p == 0:
            return p
    return n


def kernel(x, c, positions, w_mod, b_mod, w_in_even, w_out_even, w_in_odd, lam_q1, lam_k1, lam_q2, lam_k2,
           subln_g, w_out_odd, ln_mix_g, ln_mix_b, w_gate, w_up, w_down, ln_ffn_g, ln_ffn_b):
    bsz, s, d = x.shape
    depth = w_mod.shape[0]
    alpha = (2 * depth) ** 0.25
    dff = w_gate.shape[2]
    rows = bsz * s
    tm = _pick(s, (512, 256, 128))
    blk = _pick(s, (256, 128))
    bk_dsa = _pick(s, (1024, 512, 256))
    bk_diff = _pick(s, (1024, 512, 256))
    bq_diff = _pick(s, (1024, 512, 256, 128))
    assert s % tm == 0 and s % blk == 0 and bk_dsa % blk == 0 and bk_diff % bq_diff == 0, (s, tm, blk)
    assert d % (2 * LANES) == 0 and dff % LANES == 0, (d, dff)
    scale = HEAD_DIM ** -0.5

    inv = ROPE_THETA ** (-jnp.arange(0, HEAD_DIM, 2, dtype=F32) / HEAD_DIM)
    ang = positions.astype(F32)[..., None] * inv
    cos_tab, sin_tab = jnp.cos(ang).reshape(rows, -1), jnp.sin(ang).reshape(rows, -1)

    mod = _modulation(c, w_mod.astype(BF16), b_mod)
    xf = x.reshape(rows, d)

    for i in range(depth):
        mod_l = mod[i].reshape(bsz, 6, d)
        if i % 2 == 0:
            w = w_in_even[i // 2]
            n_sb = n_dsa = d // (2 * HEAD_DIM)
            sbw, dsw, ixw = n_sb * HEAD_DIM, n_dsa * HEAD_DIM, N_IDX_HEADS * HEAD_DIM
            offs = [0]
            for width in (sbw, sbw, sbw, dsw, HEAD_DIM, HEAD_DIM, ixw, HEAD_DIM, N_IDX_HEADS):
                offs.append(offs[-1] + width)
            col = lambda j: w[:, offs[j]:offs[j + 1]]
            pad = jnp.zeros((d, LANES - N_IDX_HEADS), w.dtype)
            wp = jnp.concatenate([col(0), col(1), col(2), col(5), col(5), col(8), pad], axis=1)
            wr = jnp.concatenate([col(3), col(6), col(4), col(4), col(7), col(7)], axis=1)
            plain_outs = ((0, sbw, scale), (sbw, sbw, 1.0), (3 * sbw + LANES, LANES, N_IDX_HEADS ** -0.5))
            rope_outs = ((0, dsw, scale), (dsw, ixw, scale), (dsw + ixw, LANES, 1.0),
                         (dsw + ixw + LANES, LANES, 1.0))
            vt_outs = ((2 * sbw, sbw, LANES, 0), (3 * sbw, LANES, HEAD_DIM, ONES_ROWS))
            dts = (BF16, BF16, F32, BF16, BF16, BF16, BF16)
            q_sb, k_sb, wx, qd, qx, kd2, kx2, vt_sb, vt2 = _inproj(
                xf, mod_l, cos_tab, sin_tab, wp.astype(BF16), wr.astype(BF16), _rot_partner(wr).astype(BF16),
                plain_outs, rope_outs, vt_outs, dts, s, tm)
            r3 = lambda t: t.reshape(bsz, s, t.shape[1])
            o_sb = _sb_attention(r3(q_sb), r3(k_sb), vt_sb, blk)
            wt = r3(wx)[:, :, :8].swapaxes(1, 2)
            o_dsa = _dsa_attention(r3(qd), r3(qx), wt, r3(kd2), r3(kx2), vt2, blk, bk_dsa,
                                   min(DSA_TOPK_MAX, s // 4))
            w_out = w_out_even[i // 2].astype(BF16)
            o_list = [o_sb.reshape(rows, sbw), o_dsa.reshape(rows, dsw)]
            w_list = [w_out[:sbw], w_out[sbw:]]
        else:
            j = i // 2
            w = w_in_odd[j]
            dw = w.shape[1] // 3
            lambda_init = 0.8 - 0.6 * math.exp(-0.3 * i)
            wp = w[:, 2 * dw:]
            wr = w[:, :2 * dw]
            rope_outs = ((0, dw, scale), (dw, dw, 1.0))
            vt_outs = ((0, dw, LANES, ONES_ROWS),)
            q_df, k_df, vt = _inproj(
                xf, mod_l, cos_tab, sin_tab, wp.astype(BF16), wr.astype(BF16), _rot_partner(wr).astype(BF16),
                (), rope_outs, vt_outs, (BF16, BF16), s, tm)
            r3 = lambda t: t.reshape(bsz, s, t.shape[1])
            lam_params = jnp.stack([lam_q1[j], lam_k1[j], lam_q2[j], lam_k2[j]]).astype(F32)
            o_df = _diff_attention(r3(q_df), r3(k_df), vt, lam_params, subln_g[j].astype(F32), bq_diff, bk_diff,
                                   lambda_init)
            o_list = [o_df.reshape(rows, dw)]
            w_list = [w_out_odd[j].astype(BF16)]
        xf = _post_mixer(o_list, w_list, xf, mod_l, ln_mix_g[i], ln_mix_b[i], w_gate[i].astype(BF16),
                         w_up[i].astype(BF16), w_down[i].astype(BF16), ln_ffn_g[i], ln_ffn_b[i], alpha, s, tm)
    return xf.reshape(bsz, s, d)
```

```python
import functools
import math

import jax
import jax.numpy as jnp
from jax import lax
from jax.experimental import pallas as pl
from jax.experimental.pallas import tpu as pltpu

HEAD_DIM = 64
N_IDX_HEADS = 4
DSA_TOPK_MAX = 256
DSA_HEADS_PER_PASS = 4
DIFF_HEADS_PER_STEP = 1
SB_PAIRS_PER_STEP = 4
ROPE_THETA = 10000.0
LN_EPS = 1e-5
LANES = 128
NEG_BIG = -1e30
INT_MIN = -2 ** 31
LOG2E = 1.4426950408889634
SB_EXP_FLOOR = 105.0
ONES_ROWS = 16
COUNT_ROWS16 = 128
I16_ROWS = 16
HALF_BIAS = 2 ** 15

F32 = jnp.float32
BF16 = jnp.bfloat16
I32 = jnp.int32
I16 = jnp.int16

VMEM_LIMIT = 56 * 1024 * 1024


def _cparams(sem):
    return pltpu.CompilerParams(dimension_semantics=sem, vmem_limit_bytes=VMEM_LIMIT)


def _nt(a, b):
    return lax.dot_general(a, b, (((1,), (1,)), ((), ())), preferred_element_type=F32)


def _nn(a, b):
    return jnp.dot(a, b, preferred_element_type=F32)


def _layer_norm(v, g, b):
    mu = jnp.mean(v, axis=-1, keepdims=True)
    d = v - mu
    var = jnp.mean(d * d, axis=-1, keepdims=True)
    return d * lax.rsqrt(var + LN_EPS) * g + b


def _mod_kernel(c_ref, w_ref, b_ref, o_ref):
    c = c_ref[...]
    a = (c * jax.nn.sigmoid(c)).astype(BF16)
    o_ref[0] = _nn(a, w_ref[0]) + b_ref[0]


def _modulation(c, w_mod_bf, b_mod):
    depth, d, n = w_mod_bf.shape
    bsz = c.shape[0]
    tn = n // 4
    return pl.pallas_call(
        _mod_kernel,
        grid=(depth, n // tn),
        in_specs=[pl.BlockSpec((bsz, d), lambda l, j: (0, 0)),
                  pl.BlockSpec((1, d, tn), lambda l, j: (l, 0, j)),
                  pl.BlockSpec((1, 1, tn), lambda l, j: (l, 0, j))],
        out_specs=pl.BlockSpec((1, bsz, tn), lambda l, j: (l, 0, j)),
        out_shape=jax.ShapeDtypeStruct((depth, bsz, n), F32),
        name="modulation",
        compiler_params=_cparams(("arbitrary", "arbitrary")),
    )(c, w_mod_bf, b_mod.reshape(depth, 1, n))


def _inproj_kernel(x_ref, mod_ref, cos_ref, sin_ref, wp_ref, wr_ref, wrr_ref, *out_refs,
                   plain_outs, rope_outs, vt_outs):
    x = x_ref[...]
    sh = mod_ref[0, 0:1, :]
    sc = mod_ref[0, 1:2, :]
    h = (x * (1.0 + sc) + sh).astype(BF16)
    plain = _nn(h, wp_ref[...])
    rope = _nn(h, wr_ref[...])
    rot = _nn(h, wrr_ref[...])
    c32, s32 = cos_ref[...], sin_ref[...]
    cos = jnp.concatenate([c32] * 4, axis=1)
    sg = jnp.concatenate([-s32, s32, -s32, s32], axis=1)
    n = 0
    for (c0, width, scale) in plain_outs:
        o = out_refs[n]
        v = plain[:, c0:c0 + width]
        if scale != 1.0:
            v = v * scale
        if o.dtype == F32:
            v = v.astype(BF16).astype(F32)
        o[...] = v.astype(o.dtype)
        n += 1
    for (c0, width, scale) in rope_outs:
        o = out_refs[n]
        for j in range(width // LANES):
            sl = slice(c0 + j * LANES, c0 + (j + 1) * LANES)
            v = rope[:, sl] * cos + rot[:, sl] * sg
            if scale != 1.0:
                v = v * scale
            o[:, j * LANES:(j + 1) * LANES] = v.astype(o.dtype)
        n += 1
    for (c0, width, v_rows, n_ones) in vt_outs:
        o = out_refs[n]
        for j in range(width // LANES):
            vt = plain[:, c0 + j * LANES:c0 + (j + 1) * LANES].T
            o[j, 0:v_rows, :] = vt[:v_rows].astype(o.dtype)
            if n_ones:
                o[j, v_rows:v_rows + n_ones, :] = jnp.ones((n_ones, vt.shape[1]), o.dtype)
        n += 1


def _inproj(xf, mod_l, cos_tab, sin_tab, wp, wr, wrr, plain_outs, rope_outs, vt_outs, out_dtypes,
            rows_per_batch, tm):
    r, d = xf.shape
    tiles_per_batch = rows_per_batch // tm
    widths = [w for (_, w, _) in plain_outs] + [w for (_, w, _) in rope_outs]
    vt_shapes = [(w // LANES, v_rows + n_ones) for (_, w, v_rows, n_ones) in vt_outs]
    row = lambda i: (i, 0)
    const = lambda i: (0, 0)
    return pl.pallas_call(
        functools.partial(_inproj_kernel, plain_outs=plain_outs, rope_outs=rope_outs, vt_outs=vt_outs),
        grid=(r // tm,),
        in_specs=[pl.BlockSpec((tm, d), row),
                  pl.BlockSpec((1, 6, d), lambda i: (i // tiles_per_batch, 0, 0)),
                  pl.BlockSpec((tm, cos_tab.shape[1]), row),
                  pl.BlockSpec((tm, sin_tab.shape[1]), row),
                  pl.BlockSpec(wp.shape, const),
                  pl.BlockSpec(wr.shape, const),
                  pl.BlockSpec(wrr.shape, const)],
        out_specs=([pl.BlockSpec((tm, w), row) for w in widths]
                   + [pl.BlockSpec((c, v, tm), lambda i: (0, 0, i)) for (c, v) in vt_shapes]),
        out_shape=([jax.ShapeDtypeStruct((r, w), dt) for w, dt in zip(widths, out_dtypes)]
                   + [jax.ShapeDtypeStruct((c, v, r), BF16) for (c, v) in vt_shapes]),
        name="inproj",
        compiler_params=_cparams(("arbitrary",)),
    )(xf, mod_l, cos_tab, sin_tab, wp, wr, wrr)


def _rot_partner(w):
    d, n = w.shape
    return w.reshape(d, n // HEAD_DIM, 2, HEAD_DIM // 2)[:, :, ::-1, :].reshape(d, n)


def _post_kernel(*refs, n_in, alpha):
    o_refs = refs[:n_in]
    w_refs = refs[n_in:2 * n_in]
    x_ref, mod_ref, gm_ref, bm_ref, wg_ref, wu_ref, wd_ref, gf_ref, bf_ref, out_ref = refs[2 * n_in:]
    y = _nn(o_refs[0][...], w_refs[0][...])
    for a, w in zip(o_refs[1:], w_refs[1:]):
        y = y + _nn(a[...], w[...])
    x1 = _layer_norm(alpha * x_ref[...] + (1.0 + mod_ref[0, 2:3, :]) * y, gm_ref[...], bm_ref[...])
    h = (x1 * (1.0 + mod_ref[0, 4:5, :]) + mod_ref[0, 3:4, :]).astype(BF16)
    gate = _nn(h, wg_ref[...])
    up = _nn(h, wu_ref[...])
    a = (gate * jax.nn.sigmoid(gate) * up).astype(BF16)
    y = _nn(a, wd_ref[...])
    out_ref[...] = _layer_norm(alpha * x1 + (1.0 + mod_ref[0, 5:6, :]) * y, gf_ref[...], bf_ref[...])


def _post_mixer(o_list, w_list, xf, mod_l, g_mix, b_mix, wg, wu, wd, g_ffn, b_ffn, alpha, rows_per_batch, tm):
    r, d = xf.shape
    tiles_per_batch = rows_per_batch // tm
    row = lambda i: (i, 0)
    const = lambda i: (0, 0)
    resident = pl.Buffered(1)
    vec = pl.BlockSpec((1, d), const)
    n_in = len(o_list)
    return pl.pallas_call(
        functools.partial(_post_kernel, n_in=n_in, alpha=alpha),
        grid=(r // tm,),
        in_specs=([pl.BlockSpec((tm, o.shape[1]), row) for o in o_list]
                  + [pl.BlockSpec(w.shape, const, pipeline_mode=resident) for w in w_list]
                  + [pl.BlockSpec((tm, d), row),
                     pl.BlockSpec((1, 6, d), lambda i: (i // tiles_per_batch, 0, 0)),
                     vec, vec,
                     pl.BlockSpec(wg.shape, const, pipeline_mode=resident),
                     pl.BlockSpec(wu.shape, const, pipeline_mode=resident),
                     pl.BlockSpec(wd.shape, const, pipeline_mode=resident),
                     vec, vec]),
        out_specs=pl.BlockSpec((tm, d), row),
        out_shape=jax.ShapeDtypeStruct((r, d), F32),
        name="post_mixer",
        compiler_params=_cparams(("arbitrary",)),
    )(*o_list, *w_list, xf, mod_l, g_mix.reshape(1, d), b_mix.reshape(1, d), wg, wu, wd,
      g_ffn.reshape(1, d), b_ffn.reshape(1, d))


def _lane_half_masks():
    lane = lax.broadcasted_iota(I32, (1, LANES), 1)
    return lane < HEAD_DIM, lane >= HEAD_DIM


def _tri_mask(n_keys, n_queries, reps, strict):
    kpos = lax.broadcasted_iota(I32, (n_keys, n_queries), 0)
    qpos = lax.broadcasted_iota(I32, (n_keys, n_queries), 1)
    tri = kpos < qpos if strict else kpos <= qpos
    return jnp.concatenate([tri] * reps, axis=1)


def _switch(index, branches, operand):
    def build(lo, hi):
        if hi - lo == 1:
            return branches[lo]
        mid = (lo + hi) // 2
        return lambda x: lax.cond(index < mid, build(lo, mid), build(mid, hi), x)
    return build(0, len(branches))(operand)


def _sb_kernel(q_ref, k_ref, vt_ref, o_ref, *, bq):
    qi = pl.program_id(2)
    pairs = SB_PAIRS_PER_STEP
    width = 2 * bq
    lo_half, hi_half = _lane_half_masks()
    qqs = []
    for g in range(pairs):
        q = q_ref[0, :, g * LANES:(g + 1) * LANES]
        zero = jnp.zeros_like(q)
        qqs.append(jnp.concatenate([jnp.where(lo_half, q, zero), jnp.where(hi_half, q, zero)], axis=0))
    rs = lax.broadcasted_iota(I32, (bq, bq), 0)
    cs = lax.broadcasted_iota(I32, (bq, bq), 1)
    later = (cs > rs).astype(BF16)

    def block(kb, c, acc, diag):
        off = pl.multiple_of(kb * bq, bq)
        z = jnp.concatenate([_nt(k_ref[0, pl.ds(off, bq), g * LANES:(g + 1) * LANES], qqs[g])
                             for g in range(pairs)], axis=1)
        sp = jnp.maximum(z, 0.0) + jnp.log(1.0 + jnp.exp2(jnp.abs(z) * -LOG2E))
        if diag:
            past = _tri_mask(bq, bq, 2 * pairs, strict=True)
            sp = jnp.where(past, sp, 0.0)
        tail = _nn(later, sp.astype(BF16))
        w = jnp.exp(z - sp - tail - c)
        if diag:
            w = jnp.where(past, w, 0.0)
        w = w.astype(BF16)
        pv = jnp.concatenate([_nn(vt_ref[g, :, pl.ds(off, bq)], w[:, g * width:(g + 1) * width])
                              for g in range(pairs)], axis=1)
        return c + tail[0:1] + sp[0:1], acc + pv

    c, acc = block(qi, jnp.zeros((1, pairs * width), F32), jnp.zeros((LANES, pairs * width), F32), True)

    def more(state):
        kb, c_min, _, _ = state
        return jnp.logical_and(kb >= 0, c_min <= SB_EXP_FLOOR)

    def step(state):
        kb, _, c, acc = state
        c, acc = block(kb, c, acc, False)
        return kb - 1, jnp.min(c), c, acc

    _, _, _, acc = lax.while_loop(more, step, (qi - 1, jnp.min(c), c, acc))
    for g in range(pairs):
        out_t = jnp.concatenate([acc[:HEAD_DIM, g * width:g * width + bq],
                                 acc[HEAD_DIM:, g * width + bq:(g + 1) * width]], axis=0)
        o_ref[0, :, g * LANES:(g + 1) * LANES] = out_t.T.astype(o_ref.dtype)


def _sb_attention(q, k, vt, blk):
    bsz, s, w = q.shape
    lanes = SB_PAIRS_PER_STEP * LANES
    once = pl.Buffered(1)
    return pl.pallas_call(
        functools.partial(_sb_kernel, bq=blk),
        grid=(bsz, w // lanes, s // blk),
        in_specs=[pl.BlockSpec((1, blk, lanes), lambda b, h, i: (b, i, h)),
                  pl.BlockSpec((1, s, lanes), lambda b, h, i: (b, 0, h), pipeline_mode=once),
                  pl.BlockSpec((SB_PAIRS_PER_STEP, vt.shape[1], s), lambda b, h, i: (h, 0, b), pipeline_mode=once)],
        out_specs=pl.BlockSpec((1, blk, lanes), lambda b, h, i: (b, i, h)),
        out_shape=jax.ShapeDtypeStruct((bsz, s, w), BF16),
        name="sb_attention",
        compiler_params=_cparams(("arbitrary", "arbitrary", "arbitrary")),
    )(q, k, vt)


def _fold_rows(x, target):
    while x.shape[0] > target:
        h = x.shape[0] // 2
        x = x[:h] + x[h:]
    return x


def _flash_update_t(s, m, acc, vt):
    m_new = jnp.maximum(m, jnp.max(s, axis=0, keepdims=True))
    p = jnp.exp(s - m_new).astype(BF16)
    if isinstance(vt, (list, tuple)):
        w = p.shape[1] // len(vt)
        pv = jnp.concatenate([_nn(v, p[:, g * w:(g + 1) * w]) for g, v in enumerate(vt)], axis=1)
    else:
        pv = _nn(vt, p)
    return m_new, jnp.exp(m - m_new) * acc + pv


def _flash_init(v_rows, r):
    return (jnp.full((1, r), NEG_BIG, F32), jnp.zeros((v_rows, r), F32))


def _flash_pipelined_t(qq, k_ref, vt_block, s_ref, n_blocks, bk, loop_bias, tail_bias):
    last = n_blocks - 1
    n_pairs = last // 2

    def qk(kb, slot):
        off = pl.multiple_of(kb * bk, bk)
        s_ref[slot] = _nt(k_ref[0, pl.ds(off, bk), :], qq)

    def consume(kb, slot, carry, bias_fn):
        s = s_ref[slot]
        if bias_fn is not None:
            s = s + bias_fn(kb)
        return _flash_update_t(s, *carry, vt_block(kb))

    def pair(p, carry):
        qk(2 * p + 1, 1)
        carry = consume(2 * p, 0, carry, loop_bias)
        qk(2 * p + 2, 0)
        return consume(2 * p + 1, 1, carry, loop_bias)

    qk(0, 0)
    carry = lax.fori_loop(0, n_pairs, pair, _flash_init(vt_block(0).shape[0], qq.shape[0]))
    kb = 2 * n_pairs
    qk(jnp.minimum(kb + 1, last), 1)
    carry = consume(kb, 0, carry, tail_bias)
    return lax.cond(kb < last, lambda cr: consume(kb + 1, 1, cr, tail_bias), lambda cr: cr, carry)


def _dsa_select(qx_ref, wt_ref, kx_ref, key_ref, hi_ref, lo_ref, *, qb, nkb, bq, bk, topk):
    halves = _lane_half_masks()

    qx = qx_ref[0]
    wt = wt_ref[0]
    qx_heads = []
    for hx in range(N_IDX_HEADS):
        chunk = qx[:, (hx // 2) * LANES:(hx // 2 + 1) * LANES]
        qx_heads.append(jnp.where(halves[hx % 2], chunk, jnp.zeros_like(chunk)))

    def score_block(kb, masked):
        off = pl.multiple_of(kb * bk, bk)
        kx = kx_ref[0, pl.ds(off, bk), :]
        score = jnp.zeros((bk, bq), F32)
        for hx in range(N_IDX_HEADS):
            r = jnp.maximum(_nt(kx, qx_heads[hx]).astype(BF16), 0.0).astype(F32)
            score = score + r * wt[hx:hx + 1, :]
        bits = lax.bitcast_convert_type(score, I32)
        key = jnp.where(bits < 0, INT_MIN - bits, bits)
        if masked:
            kpos = lax.broadcasted_iota(I32, (bk, bq), 0) + (kb * bk - qb * bq)
            qpos = lax.broadcasted_iota(I32, (bk, bq), 1)
            key = jnp.where(kpos <= qpos, key, INT_MIN)
        key_ref[kb] = key
        hi_ref[kb] = lax.shift_right_arithmetic(key, 16).astype(I16)
        lo_ref[kb] = ((key & (2 * HALF_BIAS - 1)) - HALF_BIAS).astype(I16)

    def score_loop_body(kb, _):
        score_block(kb, False)
        return 0

    lax.fori_loop(0, nkb - 1, score_loop_body, 0)
    score_block(nkb - 1, True)

    q_t = qb * bq + lax.broadcasted_iota(I32, (1, bq), 1)
    k_eff = jnp.minimum(topk, q_t + 1).astype(F32)

    def count16_ge(ref, thr16):
        one, zero = jnp.int16(1), jnp.int16(0)

        def body(kb, acc):
            for c in range(bk // COUNT_ROWS16):
                half = ref[kb, c * COUNT_ROWS16:(c + 1) * COUNT_ROWS16, :]
                acc = acc + _fold_rows(jnp.where(half >= thr16, one, zero), I16_ROWS)
            return acc
        acc = lax.fori_loop(0, nkb, body, jnp.zeros((I16_ROWS, bq), I16))
        return jnp.sum(acc.astype(F32), axis=0, keepdims=True)

    def kth_largest16(ref, kth):
        def bit_step(i, biased):
            cand = biased | lax.shift_left(jnp.int32(1), 15 - i)
            cnt = count16_ge(ref, (cand - HALF_BIAS).astype(I16))
            return jnp.where(cnt >= kth, cand, biased)
        return lax.fori_loop(0, 16, bit_step, jnp.zeros((1, bq), I32)) - HALF_BIAS

    t_hi = kth_largest16(hi_ref, k_eff)
    above = jnp.where(t_hi >= HALF_BIAS - 1, 0.0,
                      count16_ge(hi_ref, jnp.minimum(t_hi + 1, HALF_BIAS - 1).astype(I16)))
    t_hi16 = t_hi.astype(I16)

    def keep_group(kb, _):
        lo_ref[kb] = jnp.where(hi_ref[kb] == t_hi16, lo_ref[kb], jnp.int16(-HALF_BIAS))
        return 0

    lax.fori_loop(0, nkb, keep_group, 0)
    t_lo = kth_largest16(lo_ref, k_eff - above)
    thr = t_hi * (2 * HALF_BIAS) + (t_lo + HALF_BIAS)
    above_lo = jnp.where(t_lo >= HALF_BIAS - 1, 0.0,
                         count16_ge(lo_ref, jnp.minimum(t_lo + 1, HALF_BIAS - 1).astype(I16)))
    need = k_eff - above - above_lo
    return thr, need


def _dsa_mask(key_ref, thr, need, *, nkb, bq, bk, sub):
    rs = lax.broadcasted_iota(I32, (sub, sub), 0)
    cs = lax.broadcasted_iota(I32, (sub, sub), 1)
    earlier = (cs < rs).astype(BF16)

    def mask_block(kb, seen):
        for j in range(bk // sub):
            key = key_ref[kb, j * sub:(j + 1) * sub, :]
            eq = jnp.where(key == thr, 1.0, 0.0)
            rank = _nn(earlier, eq.astype(BF16)) + seen
            take = jnp.where(rank < need, 1, 0)
            sel = (key + take) > thr
            key_ref[kb, j * sub:(j + 1) * sub, :] = lax.bitcast_convert_type(jnp.where(sel, 0.0, NEG_BIG), I32)
            seen = seen + jnp.sum(eq, axis=0, keepdims=True)
        return seen

    lax.fori_loop(0, nkb, mask_block, jnp.zeros((1, bq), F32))


def _dsa_attend(qd_ref, kd_ref, vt_ref, o_ref, key_ref, s_ref, *, nkb, bq, bk):
    lo_half, hi_half = _lane_half_masks()
    n_heads = qd_ref.shape[2] // HEAD_DIM
    qd = qd_ref[0]
    pairs_per_pass = DSA_HEADS_PER_PASS // 2

    def stacked_bias(kb):
        bias = lax.bitcast_convert_type(key_ref[kb], F32)
        return jnp.concatenate([bias] * DSA_HEADS_PER_PASS, axis=1)

    def vt_block(kb):
        return vt_ref[0, :, pl.ds(pl.multiple_of(kb * bk, bk), bk)]

    for g in range(n_heads // DSA_HEADS_PER_PASS):
        stacked = []
        for hp in range(g * pairs_per_pass, (g + 1) * pairs_per_pass):
            chunk = qd[:, hp * LANES:(hp + 1) * LANES]
            zero = jnp.zeros_like(chunk)
            stacked += [jnp.where(lo_half, chunk, zero), jnp.where(hi_half, chunk, zero)]
        qq = jnp.concatenate(stacked, axis=0)

        _, acc = _flash_pipelined_t(qq, kd_ref, vt_block, s_ref, nkb, bk, stacked_bias, stacked_bias)
        o = acc[:HEAD_DIM] / acc[HEAD_DIM:HEAD_DIM + 1]
        for j in range(pairs_per_pass):
            hp = g * pairs_per_pass + j
            pair = jnp.concatenate([o[:, 2 * j * bq:(2 * j + 1) * bq], o[:, (2 * j + 1) * bq:(2 * j + 2) * bq]],
                                   axis=0)
            o_ref[0, :, hp * LANES:(hp + 1) * LANES] = pair.T.astype(o_ref.dtype)


def _dsa_kernel(qd_ref, qx_ref, wt_ref, kd_ref, kx_ref, vt_ref, o_ref, key_ref, *, bq, bk, sub, topk):
    qb = pl.program_id(1)
    nkb = (qb * bq) // bk + 1
    halves16 = pltpu.VMEM(key_ref.shape, I16)
    thr, need = pl.run_scoped(
        functools.partial(_dsa_select, qx_ref, wt_ref, kx_ref, key_ref, qb=qb, nkb=nkb, bq=bq, bk=bk, topk=topk),
        halves16, halves16)
    _dsa_mask(key_ref, thr, need, nkb=nkb, bq=bq, bk=bk, sub=sub)
    pl.run_scoped(functools.partial(_dsa_attend, qd_ref, kd_ref, vt_ref, o_ref, key_ref, nkb=nkb, bq=bq, bk=bk),
                  pltpu.VMEM((2, bk, DSA_HEADS_PER_PASS * bq), F32))


def _dsa_attention(qd, qx, wt, kd2, kx2, vt2, blk, bk, topk):
    bsz, s, wq = qd.shape
    kspec = pl.BlockSpec((1, s, LANES), lambda b, i: (b, 0, 0), pipeline_mode=pl.Buffered(1))
    tiles = (s // bk, bk, blk)
    return pl.pallas_call(
        functools.partial(_dsa_kernel, bq=blk, bk=bk, sub=min(bk, 256), topk=topk),
        grid=(bsz, s // blk),
        in_specs=[pl.BlockSpec((1, blk, wq), lambda b, i: (b, i, 0)),
                  pl.BlockSpec((1, blk, qx.shape[2]), lambda b, i: (b, i, 0)),
                  pl.BlockSpec((1, wt.shape[1], blk), lambda b, i: (b, 0, i)),
                  kspec, kspec,
                  pl.BlockSpec((1, vt2.shape[1], s), lambda b, i: (0, 0, b), pipeline_mode=pl.Buffered(1))],
        out_specs=pl.BlockSpec((1, blk, wq), lambda b, i: (b, i, 0)),
        out_shape=jax.ShapeDtypeStruct((bsz, s, wq), BF16),
        scratch_shapes=[pltpu.VMEM(tiles, I32)],
        name="dsa_attention",
        compiler_params=_cparams(("arbitrary", "arbitrary")),
    )(qd, qx, wt, kd2, kx2, vt2)


def _diff_kernel(q_ref, k_ref, vt_ref, lam_ref, g_ref, o_ref, s_ref, *, bq, bk, lambda_init):
    qi = pl.program_id(2)
    heads = DIFF_HEADS_PER_STEP
    width = 2 * bq
    lo_half, hi_half = _lane_half_masks()
    qqs = []
    for h in range(heads):
        q = q_ref[0, :, h * LANES:(h + 1) * LANES]
        zero = jnp.zeros_like(q)
        qqs.append(jnp.concatenate([jnp.where(lo_half, q, zero), jnp.where(hi_half, q, zero)], axis=0))
    n_full = (qi * bq) // bk

    def qk(kb, slot):
        off = pl.multiple_of(kb * bk, bk)
        for h in range(heads):
            s_ref[slot, :, h * width:(h + 1) * width] = _nt(k_ref[0, pl.ds(off, bk), h * LANES:(h + 1) * LANES], qqs[h])

    def consume(kb, slot, carry):
        off = pl.multiple_of(kb * bk, bk)
        return _flash_update_t(s_ref[slot], *carry, [vt_ref[h, :, pl.ds(off, bk)] for h in range(heads)])

    def consume_diag(n_sub, carry):
        qk(0, 0)
        nk = n_sub * bq
        s = s_ref[1, :nk, :]
        s_last = jnp.where(_tri_mask(bq, bq, 2 * heads, strict=False), s[nk - bq:], NEG_BIG)
        s = s_last if n_sub == 1 else jnp.concatenate([s[:nk - bq], s_last], axis=0)
        off = pl.multiple_of(n_full * bk, bk)
        return _flash_update_t(s, *carry, [vt_ref[h, :, pl.ds(off, nk)] for h in range(heads)])

    def pair(p, carry):
        qk(2 * p + 1, 1)
        carry = consume(2 * p, 0, carry)
        qk(2 * p + 2, 0)
        return consume(2 * p + 1, 1, carry)

    qk(n_full, 1)
    diag_sub = (qi * bq - n_full * bk) // bq
    carry = _switch(diag_sub, [functools.partial(consume_diag, v + 1) for v in range(bk // bq)],
                    _flash_init(vt_ref.shape[1], heads * width))
    carry = lax.fori_loop(0, n_full // 2, pair, carry)
    _, acc_all = lax.cond(n_full % 2 == 1, lambda cr: consume(n_full - 1, 0, cr), lambda cr: cr, carry)

    lp = lam_ref[...]
    lam = (jnp.exp(jnp.sum(lp[0:1] * lp[1:2], axis=1, keepdims=True))
           - jnp.exp(jnp.sum(lp[2:3] * lp[3:4], axis=1, keepdims=True)) + lambda_init)
    for h in range(heads):
        l = acc_all[LANES:LANES + 1, h * width:(h + 1) * width]
        acc = acc_all[:LANES, h * width:(h + 1) * width]
        o = acc[:, :bq] / l[:, :bq] - lam * (acc[:, bq:] / l[:, bq:])
        ms = jnp.mean(o * o, axis=0, keepdims=True)
        o = o * lax.rsqrt(ms + LN_EPS) * g_ref[...] * (1.0 - lambda_init)
        o_ref[0, :, h * LANES:(h + 1) * LANES] = o.T.astype(o_ref.dtype)


def _diff_attention(q, k, vt, lam_params, subln_g, blk, bk, lambda_init):
    bsz, s, w = q.shape
    heads = DIFF_HEADS_PER_STEP
    once = pl.Buffered(1)
    return pl.pallas_call(
        functools.partial(_diff_kernel, bq=blk, bk=bk, lambda_init=lambda_init),
        grid=(bsz, w // (heads * LANES), s // blk),
        in_specs=[pl.BlockSpec((1, blk, heads * LANES), lambda b, h, i: (b, i, h)),
                  pl.BlockSpec((1, s, heads * LANES), lambda b, h, i: (b, 0, h), pipeline_mode=once),
                  pl.BlockSpec((heads, vt.shape[1], s), lambda b, h, i: (h, 0, b), pipeline_mode=once),
                  pl.BlockSpec(lam_params.shape, lambda b, h, i: (0, 0)),
                  pl.BlockSpec((LANES, 1), lambda b, h, i: (0, 0))],
        out_specs=pl.BlockSpec((1, blk, heads * LANES), lambda b, h, i: (b, i, h)),
        out_shape=jax.ShapeDtypeStruct((bsz, s, w), BF16),
        scratch_shapes=[pltpu.VMEM((2, bk, heads * 2 * blk), F32)],
        name="diff_attention",
        compiler_params=_cparams(("arbitrary", "arbitrary", "arbitrary")),
    )(q, k, vt, lam_params, subln_g.reshape(LANES, 1))


def _pick(n, prefs):
    for p in prefs:
        if n % p == 0:
            return p
    return n


def kernel(x, c, positions, w_mod, b_mod, w_in_even, w_out_even, w_in_odd, lam_q1, lam_k1, lam_q2, lam_k2,
           subln_g, w_out_odd, ln_mix_g, ln_mix_b, w_gate, w_up, w_down, ln_ffn_g, ln_ffn_b):
    bsz, s, d = x.shape
    depth = w_mod.shape[0]
    alpha = (2 * depth) ** 0.25
    dff = w_gate.shape[2]
    rows = bsz * s
    tm = _pick(s, (512, 256, 128))
    blk = _pick(s, (256, 128))
    bk_dsa = _pick(s, (1024, 512, 256))
    bk_diff = _pick(s, (1024, 512, 256))
    bq_diff = _pick(s, (1024, 512, 256, 128))
    assert s % tm == 0 and s % blk == 0 and bk_dsa % blk == 0 and bk_diff % bq_diff == 0, (s, tm, blk)
    assert d % (2 * LANES) == 0 and dff % LANES == 0, (d, dff)
    scale = HEAD_DIM ** -0.5

    inv = ROPE_THETA ** (-jnp.arange(0, HEAD_DIM, 2, dtype=F32) / HEAD_DIM)
    ang = positions.astype(F32)[..., None] * inv
    cos_tab, sin_tab = jnp.cos(ang).reshape(rows, -1), jnp.sin(ang).reshape(rows, -1)

    mod = _modulation(c, w_mod.astype(BF16), b_mod)
    xf = x.reshape(rows, d)

    for i in range(depth):
        mod_l = mod[i].reshape(bsz, 6, d)
        if i % 2 == 0:
            w = w_in_even[i // 2]
            n_sb = n_dsa = d // (2 * HEAD_DIM)
            sbw, dsw, ixw = n_sb * HEAD_DIM, n_dsa * HEAD_DIM, N_IDX_HEADS * HEAD_DIM
            offs = [0]
            for width in (sbw, sbw, sbw, dsw, HEAD_DIM, HEAD_DIM, ixw, HEAD_DIM, N_IDX_HEADS):
                offs.append(offs[-1] + width)
            col = lambda j: w[:, offs[j]:offs[j + 1]]
            pad = jnp.zeros((d, LANES - N_IDX_HEADS), w.dtype)
            wp = jnp.concatenate([col(0), col(1), col(2), col(5), col(5), col(8), pad], axis=1)
            wr = jnp.concatenate([col(3), col(6), col(4), col(4), col(7), col(7)], axis=1)
            plain_outs = ((0, sbw, scale), (sbw, sbw, 1.0), (3 * sbw + LANES, LANES, N_IDX_HEADS ** -0.5))
            rope_outs = ((0, dsw, scale), (dsw, ixw, scale), (dsw + ixw, LANES, 1.0),
                         (dsw + ixw + LANES, LANES, 1.0))
            vt_outs = ((2 * sbw, sbw, LANES, 0), (3 * sbw, LANES, HEAD_DIM, ONES_ROWS))
            dts = (BF16, BF16, F32, BF16, BF16, BF16, BF16)
            q_sb, k_sb, wx, qd, qx, kd2, kx2, vt_sb, vt2 = _inproj(
                xf, mod_l, cos_tab, sin_tab, wp.astype(BF16), wr.astype(BF16), _rot_partner(wr).astype(BF16),
                plain_outs, rope_outs, vt_outs, dts, s, tm)
            r3 = lambda t: t.reshape(bsz, s, t.shape[1])
            o_sb = _sb_attention(r3(q_sb), r3(k_sb), vt_sb, blk)
            wt = r3(wx)[:, :, :8].swapaxes(1, 2)
            o_dsa = _dsa_attention(r3(qd), r3(qx), wt, r3(kd2), r3(kx2), vt2, blk, bk_dsa,
                                   min(DSA_TOPK_MAX, s // 4))
            w_out = w_out_even[i // 2].astype(BF16)
            o_list = [o_sb.reshape(rows, sbw), o_dsa.reshape(rows, dsw)]
            w_list = [w_out[:sbw], w_out[sbw:]]
        else:
            j = i // 2
            w = w_in_odd[j]
            dw = w.shape[1] // 3
            lambda_init = 0.8 - 0.6 * math.exp(-0.3 * i)
            wp = w[:, 2 * dw:]
            wr = w[:, :2 * dw]
            rope_outs = ((0, dw, scale), (dw, dw, 1.0))
            vt_outs = ((0, dw, LANES, ONES_ROWS),)
            q_df, k_df, vt = _inproj(
                xf, mod_l, cos_tab, sin_tab, wp.astype(BF16), wr.astype(BF16), _rot_partner(wr).astype(BF16),
                (), rope_outs, vt_outs, (BF16, BF16), s, tm)
            r3 = lambda t: t.reshape(bsz, s, t.shape[1])
            lam_params = jnp.stack([lam_q1[j], lam_k1[j], lam_q2[j], lam_k2[j]]).astype(F32)
            o_df = _diff_attention(r3(q_df), r3(k_df), vt, lam_params, subln_g[j].astype(F32), bq_diff, bk_diff,
                                   lambda_init)
            o_list = [o_df.reshape(rows, dw)]
            w_list = [w_out_odd[j].astype(BF16)]
        xf = _post_mixer(o_list, w_list, xf, mod_l, ln_mix_g[i], ln_mix_b[i], w_gate[i].astype(BF16),
                         w_up[i].astype(BF16), w_down[i].astype(BF16), ln_ffn_g[i], ln_ffn_b[i], alpha, s, tm)
    return xf.reshape(bsz, s, d)
```

```python
import functools
import math

import jax
import jax.numpy as jnp
from jax import lax
from jax.experimental import pallas as pl
from jax.experimental.pallas import tpu as pltpu

HEAD_DIM = 64
N_IDX_HEADS = 4
DSA_TOPK_MAX = 256
DSA_HEADS_PER_PASS = 4
DIFF_HEADS_PER_STEP = 1
SB_PAIRS_PER_STEP = 4
ROPE_THETA = 10000.0
LN_EPS = 1e-5
LANES = 128
NEG_BIG = -1e30
INT_MIN = -2 ** 31
LOG2E = 1.4426950408889634
SB_EXP_FLOOR = 105.0
FLASH_CHUNK = 512
ONES_ROWS = 16
COUNT_ROWS16 = 128
I16_ROWS = 16
HALF_BIAS = 2 ** 15

F32 = jnp.float32
BF16 = jnp.bfloat16
I32 = jnp.int32
I16 = jnp.int16

VMEM_LIMIT = 56 * 1024 * 1024


def _cparams(sem):
    return pltpu.CompilerParams(dimension_semantics=sem, vmem_limit_bytes=VMEM_LIMIT)


def _nt(a, b):
    return lax.dot_general(a, b, (((1,), (1,)), ((), ())), preferred_element_type=F32)


def _nn(a, b):
    return jnp.dot(a, b, preferred_element_type=F32)


def _layer_norm(v, g, b):
    mu = jnp.mean(v, axis=-1, keepdims=True)
    d = v - mu
    var = jnp.mean(d * d, axis=-1, keepdims=True)
    return d * lax.rsqrt(var + LN_EPS) * g + b


def _mod_kernel(c_ref, w_ref, b_ref, o_ref):
    c = c_ref[...]
    a = (c * jax.nn.sigmoid(c)).astype(BF16)
    o_ref[0] = _nn(a, w_ref[0]) + b_ref[0]


def _modulation(c, w_mod_bf, b_mod):
    depth, d, n = w_mod_bf.shape
    bsz = c.shape[0]
    tn = n // 4
    return pl.pallas_call(
        _mod_kernel,
        grid=(depth, n // tn),
        in_specs=[pl.BlockSpec((bsz, d), lambda l, j: (0, 0)),
                  pl.BlockSpec((1, d, tn), lambda l, j: (l, 0, j)),
                  pl.BlockSpec((1, 1, tn), lambda l, j: (l, 0, j))],
        out_specs=pl.BlockSpec((1, bsz, tn), lambda l, j: (l, 0, j)),
        out_shape=jax.ShapeDtypeStruct((depth, bsz, n), F32),
        name="modulation",
        compiler_params=_cparams(("arbitrary", "arbitrary")),
    )(c, w_mod_bf, b_mod.reshape(depth, 1, n))


def _inproj_kernel(x_ref, mod_ref, cos_ref, sin_ref, wp_ref, wr_ref, wrr_ref, *out_refs,
                   plain_outs, rope_outs, vt_outs):
    x = x_ref[...]
    sh = mod_ref[0, 0:1, :]
    sc = mod_ref[0, 1:2, :]
    h = (x * (1.0 + sc) + sh).astype(BF16)
    plain = _nn(h, wp_ref[...])
    rope = _nn(h, wr_ref[...])
    rot = _nn(h, wrr_ref[...])
    c32, s32 = cos_ref[...], sin_ref[...]
    cos = jnp.concatenate([c32] * 4, axis=1)
    sg = jnp.concatenate([-s32, s32, -s32, s32], axis=1)
    n = 0
    for (c0, width, scale) in plain_outs:
        o = out_refs[n]
        v = plain[:, c0:c0 + width]
        if scale != 1.0:
            v = v * scale
        if o.dtype == F32:
            v = v.astype(BF16).astype(F32)
        o[...] = v.astype(o.dtype)
        n += 1
    for (c0, width, scale) in rope_outs:
        o = out_refs[n]
        for j in range(width // LANES):
            sl = slice(c0 + j * LANES, c0 + (j + 1) * LANES)
            v = rope[:, sl] * cos + rot[:, sl] * sg
            if scale != 1.0:
                v = v * scale
            o[:, j * LANES:(j + 1) * LANES] = v.astype(o.dtype)
        n += 1
    for (c0, width, v_rows, n_ones) in vt_outs:
        o = out_refs[n]
        for j in range(width // LANES):
            vt = plain[:, c0 + j * LANES:c0 + (j + 1) * LANES].T
            o[j, 0:v_rows, :] = vt[:v_rows].astype(o.dtype)
            if n_ones:
                o[j, v_rows:v_rows + n_ones, :] = jnp.ones((n_ones, vt.shape[1]), o.dtype)
        n += 1


def _inproj(xf, mod_l, cos_tab, sin_tab, wp, wr, wrr, plain_outs, rope_outs, vt_outs, out_dtypes,
            rows_per_batch, tm):
    r, d = xf.shape
    tiles_per_batch = rows_per_batch // tm
    widths = [w for (_, w, _) in plain_outs] + [w for (_, w, _) in rope_outs]
    vt_shapes = [(w // LANES, v_rows + n_ones) for (_, w, v_rows, n_ones) in vt_outs]
    row = lambda i: (i, 0)
    const = lambda i: (0, 0)
    return pl.pallas_call(
        functools.partial(_inproj_kernel, plain_outs=plain_outs, rope_outs=rope_outs, vt_outs=vt_outs),
        grid=(r // tm,),
        in_specs=[pl.BlockSpec((tm, d), row),
                  pl.BlockSpec((1, 6, d), lambda i: (i // tiles_per_batch, 0, 0)),
                  pl.BlockSpec((tm, cos_tab.shape[1]), row),
                  pl.BlockSpec((tm, sin_tab.shape[1]), row),
                  pl.BlockSpec(wp.shape, const),
                  pl.BlockSpec(wr.shape, const),
                  pl.BlockSpec(wrr.shape, const)],
        out_specs=([pl.BlockSpec((tm, w), row) for w in widths]
                   + [pl.BlockSpec((c, v, tm), lambda i: (0, 0, i)) for (c, v) in vt_shapes]),
        out_shape=([jax.ShapeDtypeStruct((r, w), dt) for w, dt in zip(widths, out_dtypes)]
                   + [jax.ShapeDtypeStruct((c, v, r), BF16) for (c, v) in vt_shapes]),
        name="inproj",
        compiler_params=_cparams(("arbitrary",)),
    )(xf, mod_l, cos_tab, sin_tab, wp, wr, wrr)


def _rot_partner(w):
    d, n = w.shape
    return w.reshape(d, n // HEAD_DIM, 2, HEAD_DIM // 2)[:, :, ::-1, :].reshape(d, n)


def _post_kernel(*refs, n_in, alpha):
    o_refs = refs[:n_in]
    w_refs = refs[n_in:2 * n_in]
    x_ref, mod_ref, gm_ref, bm_ref, wg_ref, wu_ref, wd_ref, gf_ref, bf_ref, out_ref = refs[2 * n_in:]
    y = _nn(o_refs[0][...], w_refs[0][...])
    for a, w in zip(o_refs[1:], w_refs[1:]):
        y = y + _nn(a[...], w[...])
    x1 = _layer_norm(alpha * x_ref[...] + (1.0 + mod_ref[0, 2:3, :]) * y, gm_ref[...], bm_ref[...])
    h = (x1 * (1.0 + mod_ref[0, 4:5, :]) + mod_ref[0, 3:4, :]).astype(BF16)
    gate = _nn(h, wg_ref[...])
    up = _nn(h, wu_ref[...])
    a = (gate * jax.nn.sigmoid(gate) * up).astype(BF16)
    y = _nn(a, wd_ref[...])
    out_ref[...] = _layer_norm(alpha * x1 + (1.0 + mod_ref[0, 5:6, :]) * y, gf_ref[...], bf_ref[...])


def _post_mixer(o_list, w_list, xf, mod_l, g_mix, b_mix, wg, wu, wd, g_ffn, b_ffn, alpha, rows_per_batch, tm):
    r, d = xf.shape
    tiles_per_batch = rows_per_batch // tm
    row = lambda i: (i, 0)
    const = lambda i: (0, 0)
    resident = pl.Buffered(1)
    vec = pl.BlockSpec((1, d), const)
    n_in = len(o_list)
    return pl.pallas_call(
        functools.partial(_post_kernel, n_in=n_in, alpha=alpha),
        grid=(r // tm,),
        in_specs=([pl.BlockSpec((tm, o.shape[1]), row) for o in o_list]
                  + [pl.BlockSpec(w.shape, const, pipeline_mode=resident) for w in w_list]
                  + [pl.BlockSpec((tm, d), row),
                     pl.BlockSpec((1, 6, d), lambda i: (i // tiles_per_batch, 0, 0)),
                     vec, vec,
                     pl.BlockSpec(wg.shape, const, pipeline_mode=resident),
                     pl.BlockSpec(wu.shape, const, pipeline_mode=resident),
                     pl.BlockSpec(wd.shape, const, pipeline_mode=resident),
                     vec, vec]),
        out_specs=pl.BlockSpec((tm, d), row),
        out_shape=jax.ShapeDtypeStruct((r, d), F32),
        name="post_mixer",
        compiler_params=_cparams(("arbitrary",)),
    )(*o_list, *w_list, xf, mod_l, g_mix.reshape(1, d), b_mix.reshape(1, d), wg, wu, wd,
      g_ffn.reshape(1, d), b_ffn.reshape(1, d))


def _lane_half_masks():
    lane = lax.broadcasted_iota(I32, (1, LANES), 1)
    return lane < HEAD_DIM, lane >= HEAD_DIM


def _tri_mask(n_keys, n_queries, reps, strict):
    kpos = lax.broadcasted_iota(I32, (n_keys, n_queries), 0)
    qpos = lax.broadcasted_iota(I32, (n_keys, n_queries), 1)
    tri = kpos < qpos if strict else kpos <= qpos
    return jnp.concatenate([tri] * reps, axis=1)


def _switch(index, branches, operand):
    def build(lo, hi):
        if hi - lo == 1:
            return branches[lo]
        mid = (lo + hi) // 2
        return lambda x: lax.cond(index < mid, build(lo, mid), build(mid, hi), x)
    return build(0, len(branches))(operand)


def _sb_kernel(q_ref, k_ref, vt_ref, o_ref, *, bq):
    qi = pl.program_id(2)
    pairs = SB_PAIRS_PER_STEP
    width = 2 * bq
    lo_half, hi_half = _lane_half_masks()
    qqs = []
    for g in range(pairs):
        q = q_ref[0, :, g * LANES:(g + 1) * LANES]
        zero = jnp.zeros_like(q)
        qqs.append(jnp.concatenate([jnp.where(lo_half, q, zero), jnp.where(hi_half, q, zero)], axis=0))
    rs = lax.broadcasted_iota(I32, (bq, bq), 0)
    cs = lax.broadcasted_iota(I32, (bq, bq), 1)
    later = (cs > rs).astype(BF16)

    def block(kb, c, acc, diag):
        off = pl.multiple_of(kb * bq, bq)
        z = jnp.concatenate([_nt(k_ref[0, pl.ds(off, bq), g * LANES:(g + 1) * LANES], qqs[g])
                             for g in range(pairs)], axis=1)
        sp = jnp.maximum(z, 0.0) + jnp.log(1.0 + jnp.exp2(jnp.abs(z) * -LOG2E))
        if diag:
            past = _tri_mask(bq, bq, 2 * pairs, strict=True)
            sp = jnp.where(past, sp, 0.0)
        tail = _nn(later, sp.astype(BF16))
        w = jnp.exp(z - sp - tail - c)
        if diag:
            w = jnp.where(past, w, 0.0)
        w = w.astype(BF16)
        pv = jnp.concatenate([_nn(vt_ref[g, :, pl.ds(off, bq)], w[:, g * width:(g + 1) * width])
                              for g in range(pairs)], axis=1)
        return c + tail[0:1] + sp[0:1], acc + pv

    c, acc = block(qi, jnp.zeros((1, pairs * width), F32), jnp.zeros((LANES, pairs * width), F32), True)

    def more(state):
        kb, c_min, _, _ = state
        return jnp.logical_and(kb >= 0, c_min <= SB_EXP_FLOOR)

    def step(state):
        kb, _, c, acc = state
        c, acc = block(kb, c, acc, False)
        return kb - 1, jnp.min(c), c, acc

    _, _, _, acc = lax.while_loop(more, step, (qi - 1, jnp.min(c), c, acc))
    for g in range(pairs):
        out_t = jnp.concatenate([acc[:HEAD_DIM, g * width:g * width + bq],
                                 acc[HEAD_DIM:, g * width + bq:(g + 1) * width]], axis=0)
        o_ref[0, :, g * LANES:(g + 1) * LANES] = out_t.T.astype(o_ref.dtype)


def _sb_attention(q, k, vt, blk):
    bsz, s, w = q.shape
    lanes = SB_PAIRS_PER_STEP * LANES
    once = pl.Buffered(1)
    return pl.pallas_call(
        functools.partial(_sb_kernel, bq=blk),
        grid=(bsz, w // lanes, s // blk),
        in_specs=[pl.BlockSpec((1, blk, lanes), lambda b, h, i: (b, i, h)),
                  pl.BlockSpec((1, s, lanes), lambda b, h, i: (b, 0, h), pipeline_mode=once),
                  pl.BlockSpec((SB_PAIRS_PER_STEP, vt.shape[1], s), lambda b, h, i: (h, 0, b), pipeline_mode=once)],
        out_specs=pl.BlockSpec((1, blk, lanes), lambda b, h, i: (b, i, h)),
        out_shape=jax.ShapeDtypeStruct((bsz, s, w), BF16),
        name="sb_attention",
        compiler_params=_cparams(("arbitrary", "arbitrary", "arbitrary")),
    )(q, k, vt)


def _fold_rows(x, target):
    while x.shape[0] > target:
        h = x.shape[0] // 2
        x = x[:h] + x[h:]
    return x


def _flash_update_t(s, m, acc, vt):
    m_new = jnp.maximum(m, jnp.max(s, axis=0, keepdims=True))
    p = jnp.exp(s - m_new).astype(BF16)
    if isinstance(vt, (list, tuple)):
        w = p.shape[1] // len(vt)
        pv = jnp.concatenate([_nn(v, p[:, g * w:(g + 1) * w]) for g, v in enumerate(vt)], axis=1)
    else:
        pv = _nn(vt, p)
    return m_new, jnp.exp(m - m_new) * acc + pv


def _flash_update_chunked(s_ref, slot, m, acc, vt_chunk, chunk):
    n = s_ref.shape[1] // chunk
    m_new = m
    for c in range(n):
        m_new = jnp.maximum(m_new, jnp.max(s_ref[slot, c * chunk:(c + 1) * chunk, :], axis=0, keepdims=True))
    acc = jnp.exp(m - m_new) * acc
    for c in range(n):
        p = jnp.exp(s_ref[slot, c * chunk:(c + 1) * chunk, :] - m_new).astype(BF16)
        acc = acc + _nn(vt_chunk(c), p)
    return m_new, acc


def _flash_init(v_rows, r):
    return (jnp.full((1, r), NEG_BIG, F32), jnp.zeros((v_rows, r), F32))


def _flash_pipelined_t(qq, k_ref, vt_chunk, v_rows, s_ref, n_blocks, bk, bias_chunk):
    last = n_blocks - 1
    n_pairs = last // 2
    n_chunks = bk // FLASH_CHUNK

    def qk(kb, slot):
        off = pl.multiple_of(kb * bk, bk)
        s_ref[slot] = _nt(k_ref[0, pl.ds(off, bk), :], qq)

    def consume(kb, slot, carry):
        m, acc = carry
        m_new = m
        for c in range(n_chunks):
            rows = slice(c * FLASH_CHUNK, (c + 1) * FLASH_CHUNK)
            masked = s_ref[slot, rows, :] + bias_chunk(kb, c)
            s_ref[slot, rows, :] = masked
            m_new = jnp.maximum(m_new, jnp.max(masked, axis=0, keepdims=True))
        acc = jnp.exp(m - m_new) * acc
        for c in range(n_chunks):
            p = jnp.exp(s_ref[slot, c * FLASH_CHUNK:(c + 1) * FLASH_CHUNK, :] - m_new).astype(BF16)
            acc = acc + _nn(vt_chunk(kb, c), p)
        return m_new, acc

    def pair(p, carry):
        qk(2 * p + 1, 1)
        carry = consume(2 * p, 0, carry)
        qk(2 * p + 2, 0)
        return consume(2 * p + 1, 1, carry)

    qk(0, 0)
    carry = lax.fori_loop(0, n_pairs, pair, _flash_init(v_rows, qq.shape[0]))
    kb = 2 * n_pairs
    qk(jnp.minimum(kb + 1, last), 1)
    carry = consume(kb, 0, carry)
    return lax.cond(kb < last, lambda cr: consume(kb + 1, 1, cr), lambda cr: cr, carry)


def _dsa_select(qx_ref, wt_ref, kx_ref, key_ref, hi_ref, lo_ref, *, qb, nkb, bq, bk, topk):
    halves = _lane_half_masks()

    qx = qx_ref[0]
    wt = wt_ref[0]
    qx_heads = []
    for hx in range(N_IDX_HEADS):
        chunk = qx[:, (hx // 2) * LANES:(hx // 2 + 1) * LANES]
        qx_heads.append(jnp.where(halves[hx % 2], chunk, jnp.zeros_like(chunk)))

    def score_block(kb, masked):
        off = pl.multiple_of(kb * bk, bk)
        kx = kx_ref[0, pl.ds(off, bk), :]
        score = jnp.zeros((bk, bq), F32)
        for hx in range(N_IDX_HEADS):
            r = jnp.maximum(_nt(kx, qx_heads[hx]).astype(BF16), 0.0).astype(F32)
            score = score + r * wt[hx:hx + 1, :]
        bits = lax.bitcast_convert_type(score, I32)
        key = jnp.where(bits < 0, INT_MIN - bits, bits)
        if masked:
            kpos = lax.broadcasted_iota(I32, (bk, bq), 0) + (kb * bk - qb * bq)
            qpos = lax.broadcasted_iota(I32, (bk, bq), 1)
            key = jnp.where(kpos <= qpos, key, INT_MIN)
        key_ref[kb] = key
        hi_ref[kb] = lax.shift_right_arithmetic(key, 16).astype(I16)
        lo_ref[kb] = ((key & (2 * HALF_BIAS - 1)) - HALF_BIAS).astype(I16)

    def score_loop_body(kb, _):
        score_block(kb, False)
        return 0

    lax.fori_loop(0, nkb - 1, score_loop_body, 0)
    score_block(nkb - 1, True)

    q_t = qb * bq + lax.broadcasted_iota(I32, (1, bq), 1)
    k_eff = jnp.minimum(topk, q_t + 1).astype(F32)

    def count16_ge(ref, thr16):
        one, zero = jnp.int16(1), jnp.int16(0)

        def body(kb, acc):
            for c in range(bk // COUNT_ROWS16):
                half = ref[kb, c * COUNT_ROWS16:(c + 1) * COUNT_ROWS16, :]
                acc = acc + _fold_rows(jnp.where(half >= thr16, one, zero), I16_ROWS)
            return acc
        acc = lax.fori_loop(0, nkb, body, jnp.zeros((I16_ROWS, bq), I16))
        return jnp.sum(acc.astype(F32), axis=0, keepdims=True)

    def kth_largest16(ref, kth):
        def bit_step(i, biased):
            cand = biased | lax.shift_left(jnp.int32(1), 15 - i)
            cnt = count16_ge(ref, (cand - HALF_BIAS).astype(I16))
            return jnp.where(cnt >= kth, cand, biased)
        return lax.fori_loop(0, 16, bit_step, jnp.zeros((1, bq), I32)) - HALF_BIAS

    t_hi = kth_largest16(hi_ref, k_eff)
    above = jnp.where(t_hi >= HALF_BIAS - 1, 0.0,
                      count16_ge(hi_ref, jnp.minimum(t_hi + 1, HALF_BIAS - 1).astype(I16)))
    t_hi16 = t_hi.astype(I16)

    def keep_group(kb, _):
        lo_ref[kb] = jnp.where(hi_ref[kb] == t_hi16, lo_ref[kb], jnp.int16(-HALF_BIAS))
        return 0

    lax.fori_loop(0, nkb, keep_group, 0)
    t_lo = kth_largest16(lo_ref, k_eff - above)
    thr = t_hi * (2 * HALF_BIAS) + (t_lo + HALF_BIAS)
    above_lo = jnp.where(t_lo >= HALF_BIAS - 1, 0.0,
                         count16_ge(lo_ref, jnp.minimum(t_lo + 1, HALF_BIAS - 1).astype(I16)))
    need = k_eff - above - above_lo
    return thr, need


def _dsa_mask(key_ref, thr, need, *, nkb, bq, bk, sub):
    rs = lax.broadcasted_iota(I32, (sub, sub), 0)
    cs = lax.broadcasted_iota(I32, (sub, sub), 1)
    earlier = (cs < rs).astype(BF16)

    def mask_block(kb, seen):
        for j in range(bk // sub):
            key = key_ref[kb, j * sub:(j + 1) * sub, :]
            eq = jnp.where(key == thr, 1.0, 0.0)
            rank = _nn(earlier, eq.astype(BF16)) + seen
            take = jnp.where(rank < need, 1, 0)
            sel = (key + take) > thr
            key_ref[kb, j * sub:(j + 1) * sub, :] = lax.bitcast_convert_type(jnp.where(sel, 0.0, NEG_BIG), I32)
            seen = seen + jnp.sum(eq, axis=0, keepdims=True)
        return seen

    lax.fori_loop(0, nkb, mask_block, jnp.zeros((1, bq), F32))


def _dsa_attend(qd_ref, kd_ref, vt_ref, o_ref, key_ref, s_ref, *, nkb, bq, bk):
    lo_half, hi_half = _lane_half_masks()
    n_heads = qd_ref.shape[2] // HEAD_DIM
    qd = qd_ref[0]
    pairs_per_pass = DSA_HEADS_PER_PASS // 2

    def stacked_bias(kb, c):
        bias = lax.bitcast_convert_type(key_ref[kb, c * FLASH_CHUNK:(c + 1) * FLASH_CHUNK, :], F32)
        return jnp.concatenate([bias] * DSA_HEADS_PER_PASS, axis=1)

    def vt_chunk(kb, c):
        return vt_ref[0, :, pl.ds(pl.multiple_of(kb * bk + c * FLASH_CHUNK, FLASH_CHUNK), FLASH_CHUNK)]

    for g in range(n_heads // DSA_HEADS_PER_PASS):
        stacked = []
        for hp in range(g * pairs_per_pass, (g + 1) * pairs_per_pass):
            chunk = qd[:, hp * LANES:(hp + 1) * LANES]
            zero = jnp.zeros_like(chunk)
            stacked += [jnp.where(lo_half, chunk, zero), jnp.where(hi_half, chunk, zero)]
        qq = jnp.concatenate(stacked, axis=0)

        _, acc = _flash_pipelined_t(qq, kd_ref, vt_chunk, vt_ref.shape[1], s_ref, nkb, bk, stacked_bias)
        o = acc[:HEAD_DIM] / acc[HEAD_DIM:HEAD_DIM + 1]
        for j in range(pairs_per_pass):
            hp = g * pairs_per_pass + j
            pair = jnp.concatenate([o[:, 2 * j * bq:(2 * j + 1) * bq], o[:, (2 * j + 1) * bq:(2 * j + 2) * bq]],
                                   axis=0)
            o_ref[0, :, hp * LANES:(hp + 1) * LANES] = pair.T.astype(o_ref.dtype)


def _dsa_kernel(qd_ref, qx_ref, wt_ref, kd_ref, kx_ref, vt_ref, o_ref, key_ref, *, bq, bk, sub, topk):
    qb = pl.program_id(1)
    nkb = (qb * bq) // bk + 1
    halves16 = pltpu.VMEM(key_ref.shape, I16)
    thr, need = pl.run_scoped(
        functools.partial(_dsa_select, qx_ref, wt_ref, kx_ref, key_ref, qb=qb, nkb=nkb, bq=bq, bk=bk, topk=topk),
        halves16, halves16)
    _dsa_mask(key_ref, thr, need, nkb=nkb, bq=bq, bk=bk, sub=sub)
    pl.run_scoped(functools.partial(_dsa_attend, qd_ref, kd_ref, vt_ref, o_ref, key_ref, nkb=nkb, bq=bq, bk=bk),
                  pltpu.VMEM((2, bk, DSA_HEADS_PER_PASS * bq), F32))


def _dsa_attention(qd, qx, wt, kd2, kx2, vt2, blk, bk, topk):
    bsz, s, wq = qd.shape
    kspec = pl.BlockSpec((1, s, LANES), lambda b, i: (b, 0, 0), pipeline_mode=pl.Buffered(1))
    tiles = (s // bk, bk, blk)
    return pl.pallas_call(
        functools.partial(_dsa_kernel, bq=blk, bk=bk, sub=min(bk, 256), topk=topk),
        grid=(bsz, s // blk),
        in_specs=[pl.BlockSpec((1, blk, wq), lambda b, i: (b, i, 0)),
                  pl.BlockSpec((1, blk, qx.shape[2]), lambda b, i: (b, i, 0)),
                  pl.BlockSpec((1, wt.shape[1], blk), lambda b, i: (b, 0, i)),
                  kspec, kspec,
                  pl.BlockSpec((1, vt2.shape[1], s), lambda b, i: (0, 0, b), pipeline_mode=pl.Buffered(1))],
        out_specs=pl.BlockSpec((1, blk, wq), lambda b, i: (b, i, 0)),
        out_shape=jax.ShapeDtypeStruct((bsz, s, wq), BF16),
        scratch_shapes=[pltpu.VMEM(tiles, I32)],
        name="dsa_attention",
        compiler_params=_cparams(("arbitrary", "arbitrary")),
    )(qd, qx, wt, kd2, kx2, vt2)


def _diff_kernel(q_ref, k_ref, vt_ref, lam_ref, g_ref, o_ref, s_ref, *, bq, bk, lambda_init):
    qi = pl.program_id(2)
    heads = DIFF_HEADS_PER_STEP
    width = 2 * bq
    lo_half, hi_half = _lane_half_masks()
    qqs = []
    for h in range(heads):
        q = q_ref[0, :, h * LANES:(h + 1) * LANES]
        zero = jnp.zeros_like(q)
        qqs.append(jnp.concatenate([jnp.where(lo_half, q, zero), jnp.where(hi_half, q, zero)], axis=0))
    n_full = (qi * bq) // bk

    def qk(kb, slot):
        off = pl.multiple_of(kb * bk, bk)
        for h in range(heads):
            s_ref[slot, :, h * width:(h + 1) * width] = _nt(k_ref[0, pl.ds(off, bk), h * LANES:(h + 1) * LANES], qqs[h])

    def consume(kb, slot, carry):
        def vt_chunk(c):
            return vt_ref[0, :, pl.ds(pl.multiple_of(kb * bk + c * FLASH_CHUNK, FLASH_CHUNK), FLASH_CHUNK)]
        return _flash_update_chunked(s_ref, slot, *carry, vt_chunk, FLASH_CHUNK)

    def consume_diag(n_sub, carry):
        qk(0, 0)
        nk = n_sub * bq
        s = s_ref[1, :nk, :]
        s_last = jnp.where(_tri_mask(bq, bq, 2 * heads, strict=False), s[nk - bq:], NEG_BIG)
        s = s_last if n_sub == 1 else jnp.concatenate([s[:nk - bq], s_last], axis=0)
        off = pl.multiple_of(n_full * bk, bk)
        return _flash_update_t(s, *carry, [vt_ref[h, :, pl.ds(off, nk)] for h in range(heads)])

    def pair(p, carry):
        qk(2 * p + 1, 1)
        carry = consume(2 * p, 0, carry)
        qk(2 * p + 2, 0)
        return consume(2 * p + 1, 1, carry)

    qk(n_full, 1)
    diag_sub = (qi * bq - n_full * bk) // bq
    carry = _switch(diag_sub, [functools.partial(consume_diag, v + 1) for v in range(bk // bq)],
                    _flash_init(vt_ref.shape[1], heads * width))
    carry = lax.fori_loop(0, n_full // 2, pair, carry)
    _, acc_all = lax.cond(n_full % 2 == 1, lambda cr: consume(n_full - 1, 0, cr), lambda cr: cr, carry)

    lp = lam_ref[...]
    lam = (jnp.exp(jnp.sum(lp[0:1] * lp[1:2], axis=1, keepdims=True))
           - jnp.exp(jnp.sum(lp[2:3] * lp[3:4], axis=1, keepdims=True)) + lambda_init)
    for h in range(heads):
        l = acc_all[LANES:LANES + 1, h * width:(h + 1) * width]
        acc = acc_all[:LANES, h * width:(h + 1) * width]
        o = acc[:, :bq] / l[:, :bq] - lam * (acc[:, bq:] / l[:, bq:])
        ms = jnp.mean(o * o, axis=0, keepdims=True)
        o = o * lax.rsqrt(ms + LN_EPS) * g_ref[...] * (1.0 - lambda_init)
        o_ref[0, :, h * LANES:(h + 1) * LANES] = o.T.astype(o_ref.dtype)


def _diff_attention(q, k, vt, lam_params, subln_g, blk, bk, lambda_init):
    bsz, s, w = q.shape
    heads = DIFF_HEADS_PER_STEP
    once = pl.Buffered(1)
    return pl.pallas_call(
        functools.partial(_diff_kernel, bq=blk, bk=bk, lambda_init=lambda_init),
        grid=(bsz, w // (heads * LANES), s // blk),
        in_specs=[pl.BlockSpec((1, blk, heads * LANES), lambda b, h, i: (b, i, h)),
                  pl.BlockSpec((1, s, heads * LANES), lambda b, h, i: (b, 0, h), pipeline_mode=once),
                  pl.BlockSpec((heads, vt.shape[1], s), lambda b, h, i: (h, 0, b), pipeline_mode=once),
                  pl.BlockSpec(lam_params.shape, lambda b, h, i: (0, 0)),
                  pl.BlockSpec((LANES, 1), lambda b, h, i: (0, 0))],
        out_specs=pl.BlockSpec((1, blk, heads * LANES), lambda b, h, i: (b, i, h)),
        out_shape=jax.ShapeDtypeStruct((bsz, s, w), BF16),
        scratch_shapes=[pltpu.VMEM((2, bk, heads * 2 * blk), F32)],
        name="diff_attention",
        compiler_params=_cparams(("arbitrary", "arbitrary", "arbitrary")),
    )(q, k, vt, lam_params, subln_g.reshape(LANES, 1))


def _pick(n, prefs):
    for p in prefs:
        if n % p == 0:
            return p
    return n


def kernel(x, c, positions, w_mod, b_mod, w_in_even, w_out_even, w_in_odd, lam_q1, lam_k1, lam_q2, lam_k2,
           subln_g, w_out_odd, ln_mix_g, ln_mix_b, w_gate, w_up, w_down, ln_ffn_g, ln_ffn_b):
    bsz, s, d = x.shape
    depth = w_mod.shape[0]
    alpha = (2 * depth) ** 0.25
    dff = w_gate.shape[2]
    rows = bsz * s
    tm = _pick(s, (512, 256, 128))
    blk = _pick(s, (256, 128))
    bk_dsa = _pick(s, (1024, 512, 256))
    bk_diff = _pick(s, (1024, 512, 256))
    bq_diff = _pick(s, (1024, 512, 256, 128))
    assert s % tm == 0 and s % blk == 0 and bk_dsa % blk == 0 and bk_diff % bq_diff == 0, (s, tm, blk)
    assert d % (2 * LANES) == 0 and dff % LANES == 0, (d, dff)
    scale = HEAD_DIM ** -0.5

    inv = ROPE_THETA ** (-jnp.arange(0, HEAD_DIM, 2, dtype=F32) / HEAD_DIM)
    ang = positions.astype(F32)[..., None] * inv
    cos_tab, sin_tab = jnp.cos(ang).reshape(rows, -1), jnp.sin(ang).reshape(rows, -1)

    mod = _modulation(c, w_mod.astype(BF16), b_mod)
    xf = x.reshape(rows, d)

    for i in range(depth):
        mod_l = mod[i].reshape(bsz, 6, d)
        if i % 2 == 0:
            w = w_in_even[i // 2]
            n_sb = n_dsa = d // (2 * HEAD_DIM)
            sbw, dsw, ixw = n_sb * HEAD_DIM, n_dsa * HEAD_DIM, N_IDX_HEADS * HEAD_DIM
            offs = [0]
            for width in (sbw, sbw, sbw, dsw, HEAD_DIM, HEAD_DIM, ixw, HEAD_DIM, N_IDX_HEADS):
                offs.append(offs[-1] + width)
            col = lambda j: w[:, offs[j]:offs[j + 1]]
            pad = jnp.zeros((d, LANES - N_IDX_HEADS), w.dtype)
            wp = jnp.concatenate([col(0), col(1), col(2), col(5), col(5), col(8), pad], axis=1)
            wr = jnp.concatenate([col(3), col(6), col(4), col(4), col(7), col(7)], axis=1)
            plain_outs = ((0, sbw, scale), (sbw, sbw, 1.0), (3 * sbw + LANES, LANES, N_IDX_HEADS ** -0.5))
            rope_outs = ((0, dsw, scale), (dsw, ixw, scale), (dsw + ixw, LANES, 1.0),
                         (dsw + ixw + LANES, LANES, 1.0))
            vt_outs = ((2 * sbw, sbw, LANES, 0), (3 * sbw, LANES, HEAD_DIM, ONES_ROWS))
            dts = (BF16, BF16, F32, BF16, BF16, BF16, BF16)
            q_sb, k_sb, wx, qd, qx, kd2, kx2, vt_sb, vt2 = _inproj(
                xf, mod_l, cos_tab, sin_tab, wp.astype(BF16), wr.astype(BF16), _rot_partner(wr).astype(BF16),
                plain_outs, rope_outs, vt_outs, dts, s, tm)
            r3 = lambda t: t.reshape(bsz, s, t.shape[1])
            o_sb = _sb_attention(r3(q_sb), r3(k_sb), vt_sb, blk)
            wt = r3(wx)[:, :, :8].swapaxes(1, 2)
            o_dsa = _dsa_attention(r3(qd), r3(qx), wt, r3(kd2), r3(kx2), vt2, blk, bk_dsa,
                                   min(DSA_TOPK_MAX, s // 4))
            w_out = w_out_even[i // 2].astype(BF16)
            o_list = [o_sb.reshape(rows, sbw), o_dsa.reshape(rows, dsw)]
            w_list = [w_out[:sbw], w_out[sbw:]]
        else:
            j = i // 2
            w = w_in_odd[j]
            dw = w.shape[1] // 3
            lambda_init = 0.8 - 0.6 * math.exp(-0.3 * i)
            wp = w[:, 2 * dw:]
            wr = w[:, :2 * dw]
            rope_outs = ((0, dw, scale), (dw, dw, 1.0))
            vt_outs = ((0, dw, LANES, ONES_ROWS),)
            q_df, k_df, vt = _inproj(
                xf, mod_l, cos_tab, sin_tab, wp.astype(BF16), wr.astype(BF16), _rot_partner(wr).astype(BF16),
                (), rope_outs, vt_outs, (BF16, BF16), s, tm)
            r3 = lambda t: t.reshape(bsz, s, t.shape[1])
            lam_params = jnp.stack([lam_q1[j], lam_k1[j], lam_q2[j], lam_k2[j]]).astype(F32)
            o_df = _diff_attention(r3(q_df), r3(k_df), vt, lam_params, subln_g[j].astype(F32), bq_diff, bk_diff,
                                   lambda_init)
            o_list = [o_df.reshape(rows, dw)]
            w_list = [w_out_odd[j].astype(BF16)]
        xf = _post_mixer(o_list, w_list, xf, mod_l, ln_mix_g[i], ln_mix_b[i], w_gate[i].astype(BF16),
                         w_up[i].astype(BF16), w_down[i].astype(BF16), ln_ffn_g[i], ln_ffn_b[i], alpha, s, tm)
    return xf.reshape(bsz, s, d)
```

```python
import functools
import math

import jax
import jax.numpy as jnp
from jax import lax
from jax.experimental import pallas as pl
from jax.experimental.pallas import tpu as pltpu

HEAD_DIM = 64
N_IDX_HEADS = 4
DSA_TOPK_MAX = 256
DSA_HEADS_PER_PASS = 4
DIFF_HEADS_PER_STEP = 1
SB_PAIRS_PER_STEP = 4
ROPE_THETA = 10000.0
LN_EPS = 1e-5
LANES = 128
NEG_BIG = -1e30
INT_MIN = -2 ** 31
LOG2E = 1.4426950408889634
SB_EXP_FLOOR = 105.0
FLASH_CHUNK = 512
ONES_ROWS = 16
COUNT_ROWS16 = 128
I16_ROWS = 16
HALF_BIAS = 2 ** 15

F32 = jnp.float32
BF16 = jnp.bfloat16
I32 = jnp.int32
I16 = jnp.int16

VMEM_LIMIT = 56 * 1024 * 1024


def _cparams(sem):
    return pltpu.CompilerParams(dimension_semantics=sem, vmem_limit_bytes=VMEM_LIMIT)


def _nt(a, b):
    return lax.dot_general(a, b, (((1,), (1,)), ((), ())), preferred_element_type=F32)


def _nn(a, b):
    return jnp.dot(a, b, preferred_element_type=F32)


def _layer_norm(v, g, b):
    mu = jnp.mean(v, axis=-1, keepdims=True)
    d = v - mu
    var = jnp.mean(d * d, axis=-1, keepdims=True)
    return d * lax.rsqrt(var + LN_EPS) * g + b


def _mod_kernel(c_ref, w_ref, b_ref, o_ref):
    c = c_ref[...]
    a = (c * jax.nn.sigmoid(c)).astype(BF16)
    o_ref[0] = _nn(a, w_ref[0]) + b_ref[0]


def _modulation(c, w_mod_bf, b_mod):
    depth, d, n = w_mod_bf.shape
    bsz = c.shape[0]
    tn = n // 4
    return pl.pallas_call(
        _mod_kernel,
        grid=(depth, n // tn),
        in_specs=[pl.BlockSpec((bsz, d), lambda l, j: (0, 0)),
                  pl.BlockSpec((1, d, tn), lambda l, j: (l, 0, j)),
                  pl.BlockSpec((1, 1, tn), lambda l, j: (l, 0, j))],
        out_specs=pl.BlockSpec((1, bsz, tn), lambda l, j: (l, 0, j)),
        out_shape=jax.ShapeDtypeStruct((depth, bsz, n), F32),
        name="modulation",
        compiler_params=_cparams(("arbitrary", "arbitrary")),
    )(c, w_mod_bf, b_mod.reshape(depth, 1, n))


def _inproj_kernel(x_ref, mod_ref, cos_ref, sin_ref, wp_ref, wr_ref, wrr_ref, *out_refs,
                   plain_outs, rope_outs, vt_outs):
    x = x_ref[...]
    sh = mod_ref[0, 0:1, :]
    sc = mod_ref[0, 1:2, :]
    h = (x * (1.0 + sc) + sh).astype(BF16)
    plain = _nn(h, wp_ref[...])
    rope = _nn(h, wr_ref[...])
    rot = _nn(h, wrr_ref[...])
    c32, s32 = cos_ref[...], sin_ref[...]
    cos = jnp.concatenate([c32] * 4, axis=1)
    sg = jnp.concatenate([-s32, s32, -s32, s32], axis=1)
    n = 0
    for (c0, width, scale) in plain_outs:
        o = out_refs[n]
        v = plain[:, c0:c0 + width]
        if scale != 1.0:
            v = v * scale
        if o.dtype == F32:
            v = v.astype(BF16).astype(F32)
        o[...] = v.astype(o.dtype)
        n += 1
    for (c0, width, scale) in rope_outs:
        o = out_refs[n]
        for j in range(width // LANES):
            sl = slice(c0 + j * LANES, c0 + (j + 1) * LANES)
            v = rope[:, sl] * cos + rot[:, sl] * sg
            if scale != 1.0:
                v = v * scale
            o[:, j * LANES:(j + 1) * LANES] = v.astype(o.dtype)
        n += 1
    for (c0, width, v_rows, n_ones) in vt_outs:
        o = out_refs[n]
        for j in range(width // LANES):
            vt = plain[:, c0 + j * LANES:c0 + (j + 1) * LANES].T
            o[j, 0:v_rows, :] = vt[:v_rows].astype(o.dtype)
            if n_ones:
                o[j, v_rows:v_rows + n_ones, :] = jnp.ones((n_ones, vt.shape[1]), o.dtype)
        n += 1


def _inproj(xf, mod_l, cos_tab, sin_tab, wp, wr, wrr, plain_outs, rope_outs, vt_outs, out_dtypes,
            rows_per_batch, tm):
    r, d = xf.shape
    tiles_per_batch = rows_per_batch // tm
    widths = [w for (_, w, _) in plain_outs] + [w for (_, w, _) in rope_outs]
    vt_shapes = [(w // LANES, v_rows + n_ones) for (_, w, v_rows, n_ones) in vt_outs]
    row = lambda i: (i, 0)
    const = lambda i: (0, 0)
    return pl.pallas_call(
        functools.partial(_inproj_kernel, plain_outs=plain_outs, rope_outs=rope_outs, vt_outs=vt_outs),
        grid=(r // tm,),
        in_specs=[pl.BlockSpec((tm, d), row),
                  pl.BlockSpec((1, 6, d), lambda i: (i // tiles_per_batch, 0, 0)),
                  pl.BlockSpec((tm, cos_tab.shape[1]), row),
                  pl.BlockSpec((tm, sin_tab.shape[1]), row),
                  pl.BlockSpec(wp.shape, const),
                  pl.BlockSpec(wr.shape, const),
                  pl.BlockSpec(wrr.shape, const)],
        out_specs=([pl.BlockSpec((tm, w), row) for w in widths]
                   + [pl.BlockSpec((c, v, tm), lambda i: (0, 0, i)) for (c, v) in vt_shapes]),
        out_shape=([jax.ShapeDtypeStruct((r, w), dt) for w, dt in zip(widths, out_dtypes)]
                   + [jax.ShapeDtypeStruct((c, v, r), BF16) for (c, v) in vt_shapes]),
        name="inproj",
        compiler_params=_cparams(("arbitrary",)),
    )(xf, mod_l, cos_tab, sin_tab, wp, wr, wrr)


def _rot_partner(w):
    d, n = w.shape
    return w.reshape(d, n // HEAD_DIM, 2, HEAD_DIM // 2)[:, :, ::-1, :].reshape(d, n)


def _post_kernel(*refs, n_in, alpha):
    o_refs = refs[:n_in]
    w_refs = refs[n_in:2 * n_in]
    x_ref, mod_ref, gm_ref, bm_ref, wg_ref, wu_ref, wd_ref, gf_ref, bf_ref, out_ref = refs[2 * n_in:]
    y = _nn(o_refs[0][...], w_refs[0][...])
    for a, w in zip(o_refs[1:], w_refs[1:]):
        y = y + _nn(a[...], w[...])
    x1 = _layer_norm(alpha * x_ref[...] + (1.0 + mod_ref[0, 2:3, :]) * y, gm_ref[...], bm_ref[...])
    h = (x1 * (1.0 + mod_ref[0, 4:5, :]) + mod_ref[0, 3:4, :]).astype(BF16)
    gate = _nn(h, wg_ref[...])
    up = _nn(h, wu_ref[...])
    a = (gate * jax.nn.sigmoid(gate) * up).astype(BF16)
    y = _nn(a, wd_ref[...])
    out_ref[...] = _layer_norm(alpha * x1 + (1.0 + mod_ref[0, 5:6, :]) * y, gf_ref[...], bf_ref[...])


def _post_mixer(o_list, w_list, xf, mod_l, g_mix, b_mix, wg, wu, wd, g_ffn, b_ffn, alpha, rows_per_batch, tm):
    r, d = xf.shape
    tiles_per_batch = rows_per_batch // tm
    row = lambda i: (i, 0)
    const = lambda i: (0, 0)
    resident = pl.Buffered(1)
    vec = pl.BlockSpec((1, d), const)
    n_in = len(o_list)
    return pl.pallas_call(
        functools.partial(_post_kernel, n_in=n_in, alpha=alpha),
        grid=(r // tm,),
        in_specs=([pl.BlockSpec((tm, o.shape[1]), row) for o in o_list]
                  + [pl.BlockSpec(w.shape, const, pipeline_mode=resident) for w in w_list]
                  + [pl.BlockSpec((tm, d), row),
                     pl.BlockSpec((1, 6, d), lambda i: (i // tiles_per_batch, 0, 0)),
                     vec, vec,
                     pl.BlockSpec(wg.shape, const, pipeline_mode=resident),
                     pl.BlockSpec(wu.shape, const, pipeline_mode=resident),
                     pl.BlockSpec(wd.shape, const, pipeline_mode=resident),
                     vec, vec]),
        out_specs=pl.BlockSpec((tm, d), row),
        out_shape=jax.ShapeDtypeStruct((r, d), F32),
        name="post_mixer",
        compiler_params=_cparams(("arbitrary",)),
    )(*o_list, *w_list, xf, mod_l, g_mix.reshape(1, d), b_mix.reshape(1, d), wg, wu, wd,
      g_ffn.reshape(1, d), b_ffn.reshape(1, d))


def _lane_half_masks():
    lane = lax.broadcasted_iota(I32, (1, LANES), 1)
    return lane < HEAD_DIM, lane >= HEAD_DIM


def _tri_mask(n_keys, n_queries, reps, strict):
    kpos = lax.broadcasted_iota(I32, (n_keys, n_queries), 0)
    qpos = lax.broadcasted_iota(I32, (n_keys, n_queries), 1)
    tri = kpos < qpos if strict else kpos <= qpos
    return jnp.concatenate([tri] * reps, axis=1)


def _switch(index, branches, operand):
    def build(lo, hi):
        if hi - lo == 1:
            return branches[lo]
        mid = (lo + hi) // 2
        return lambda x: lax.cond(index < mid, build(lo, mid), build(mid, hi), x)
    return build(0, len(branches))(operand)


def _sb_kernel(q_ref, k_ref, vt_ref, o_ref, *, bq):
    qi = pl.program_id(2)
    pairs = SB_PAIRS_PER_STEP
    width = 2 * bq
    lo_half, hi_half = _lane_half_masks()
    qqs = []
    for g in range(pairs):
        q = q_ref[0, :, g * LANES:(g + 1) * LANES]
        zero = jnp.zeros_like(q)
        qqs.append(jnp.concatenate([jnp.where(lo_half, q, zero), jnp.where(hi_half, q, zero)], axis=0))
    rs = lax.broadcasted_iota(I32, (bq, bq), 0)
    cs = lax.broadcasted_iota(I32, (bq, bq), 1)
    later = (cs > rs).astype(BF16)

    def block(kb, c, acc, diag):
        off = pl.multiple_of(kb * bq, bq)
        z = jnp.concatenate([_nt(k_ref[0, pl.ds(off, bq), g * LANES:(g + 1) * LANES], qqs[g])
                             for g in range(pairs)], axis=1)
        sp = jnp.maximum(z, 0.0) + jnp.log(1.0 + jnp.exp2(jnp.abs(z) * -LOG2E))
        if diag:
            past = _tri_mask(bq, bq, 2 * pairs, strict=True)
            sp = jnp.where(past, sp, 0.0)
        tail = _nn(later, sp.astype(BF16))
        w = jnp.exp(z - sp - tail - c)
        if diag:
            w = jnp.where(past, w, 0.0)
        w = w.astype(BF16)
        pv = jnp.concatenate([_nn(vt_ref[g, :, pl.ds(off, bq)], w[:, g * width:(g + 1) * width])
                              for g in range(pairs)], axis=1)
        return c + tail[0:1] + sp[0:1], acc + pv

    c, acc = block(qi, jnp.zeros((1, pairs * width), F32), jnp.zeros((LANES, pairs * width), F32), True)

    def more(state):
        kb, c_min, _, _ = state
        return jnp.logical_and(kb >= 0, c_min <= SB_EXP_FLOOR)

    def step(state):
        kb, _, c, acc = state
        c, acc = block(kb, c, acc, False)
        return kb - 1, jnp.min(c), c, acc

    _, _, _, acc = lax.while_loop(more, step, (qi - 1, jnp.min(c), c, acc))
    for g in range(pairs):
        out_t = jnp.concatenate([acc[:HEAD_DIM, g * width:g * width + bq],
                                 acc[HEAD_DIM:, g * width + bq:(g + 1) * width]], axis=0)
        o_ref[0, :, g * LANES:(g + 1) * LANES] = out_t.T.astype(o_ref.dtype)


def _sb_attention(q, k, vt, blk):
    bsz, s, w = q.shape
    lanes = SB_PAIRS_PER_STEP * LANES
    once = pl.Buffered(1)
    return pl.pallas_call(
        functools.partial(_sb_kernel, bq=blk),
        grid=(bsz, w // lanes, s // blk),
        in_specs=[pl.BlockSpec((1, blk, lanes), lambda b, h, i: (b, i, h)),
                  pl.BlockSpec((1, s, lanes), lambda b, h, i: (b, 0, h), pipeline_mode=once),
                  pl.BlockSpec((SB_PAIRS_PER_STEP, vt.shape[1], s), lambda b, h, i: (h, 0, b), pipeline_mode=once)],
        out_specs=pl.BlockSpec((1, blk, lanes), lambda b, h, i: (b, i, h)),
        out_shape=jax.ShapeDtypeStruct((bsz, s, w), BF16),
        name="sb_attention",
        compiler_params=_cparams(("arbitrary", "arbitrary", "arbitrary")),
    )(q, k, vt)


def _fold_rows(x, target):
    while x.shape[0] > target:
        h = x.shape[0] // 2
        x = x[:h] + x[h:]
    return x


def _flash_update_t(s, m, acc, vt):
    m_new = jnp.maximum(m, jnp.max(s, axis=0, keepdims=True))
    p = jnp.exp(s - m_new).astype(BF16)
    if isinstance(vt, (list, tuple)):
        w = p.shape[1] // len(vt)
        pv = jnp.concatenate([_nn(v, p[:, g * w:(g + 1) * w]) for g, v in enumerate(vt)], axis=1)
    else:
        pv = _nn(vt, p)
    return m_new, jnp.exp(m - m_new) * acc + pv


def _flash_update_chunked(s_ref, slot, m, acc, vt_chunk, chunk):
    n = s_ref.shape[1] // chunk
    m_new = m
    for c in range(n):
        m_new = jnp.maximum(m_new, jnp.max(s_ref[slot, c * chunk:(c + 1) * chunk, :], axis=0, keepdims=True))
    acc = jnp.exp(m - m_new) * acc
    for c in range(n):
        p = jnp.exp(s_ref[slot, c * chunk:(c + 1) * chunk, :] - m_new).astype(BF16)
        acc = acc + _nn(vt_chunk(c), p)
    return m_new, acc


def _flash_init(v_rows, r):
    return (jnp.full((1, r), NEG_BIG, F32), jnp.zeros((v_rows, r), F32))


def _flash_pipelined_t(qq, k_ref, vt_block, s_ref, n_blocks, bk, loop_bias, tail_bias):
    last = n_blocks - 1
    n_pairs = last // 2

    def qk(kb, slot):
        off = pl.multiple_of(kb * bk, bk)
        s_ref[slot] = _nt(k_ref[0, pl.ds(off, bk), :], qq)

    def consume(kb, slot, carry, bias_fn):
        s = s_ref[slot]
        if bias_fn is not None:
            s = s + bias_fn(kb)
        return _flash_update_t(s, *carry, vt_block(kb))

    def pair(p, carry):
        qk(2 * p + 1, 1)
        carry = consume(2 * p, 0, carry, loop_bias)
        qk(2 * p + 2, 0)
        return consume(2 * p + 1, 1, carry, loop_bias)

    qk(0, 0)
    carry = lax.fori_loop(0, n_pairs, pair, _flash_init(vt_block(0).shape[0], qq.shape[0]))
    kb = 2 * n_pairs
    qk(jnp.minimum(kb + 1, last), 1)
    carry = consume(kb, 0, carry, tail_bias)
    return lax.cond(kb < last, lambda cr: consume(kb + 1, 1, cr, tail_bias), lambda cr: cr, carry)


def _dsa_select(qx_ref, wt_ref, kx_ref, key_ref, hi_ref, lo_ref, *, qb, nkb, bq, bk, topk):
    halves = _lane_half_masks()

    qx = qx_ref[0]
    wt = wt_ref[0]
    qx_heads = []
    for hx in range(N_IDX_HEADS):
        chunk = qx[:, (hx // 2) * LANES:(hx // 2 + 1) * LANES]
        qx_heads.append(jnp.where(halves[hx % 2], chunk, jnp.zeros_like(chunk)))

    def score_block(kb, masked):
        off = pl.multiple_of(kb * bk, bk)
        kx = kx_ref[0, pl.ds(off, bk), :]
        score = jnp.zeros((bk, bq), F32)
        for hx in range(N_IDX_HEADS):
            r = jnp.maximum(_nt(kx, qx_heads[hx]).astype(BF16), 0.0).astype(F32)
            score = score + r * wt[hx:hx + 1, :]
        bits = lax.bitcast_convert_type(score, I32)
        key = jnp.where(bits < 0, INT_MIN - bits, bits)
        if masked:
            kpos = lax.broadcasted_iota(I32, (bk, bq), 0) + (kb * bk - qb * bq)
            qpos = lax.broadcasted_iota(I32, (bk, bq), 1)
            key = jnp.where(kpos <= qpos, key, INT_MIN)
        key_ref[kb] = key
        hi_ref[kb] = lax.shift_right_arithmetic(key, 16).astype(I16)
        lo_ref[kb] = ((key & (2 * HALF_BIAS - 1)) - HALF_BIAS).astype(I16)

    def score_loop_body(kb, _):
        score_block(kb, False)
        return 0

    lax.fori_loop(0, nkb - 1, score_loop_body, 0)
    score_block(nkb - 1, True)

    q_t = qb * bq + lax.broadcasted_iota(I32, (1, bq), 1)
    k_eff = jnp.minimum(topk, q_t + 1).astype(F32)

    def count16_ge(ref, thr16):
        one, zero = jnp.int16(1), jnp.int16(0)

        def body(kb, acc):
            for c in range(bk // COUNT_ROWS16):
                half = ref[kb, c * COUNT_ROWS16:(c + 1) * COUNT_ROWS16, :]
                acc = acc + _fold_rows(jnp.where(half >= thr16, one, zero), I16_ROWS)
            return acc
        acc = lax.fori_loop(0, nkb, body, jnp.zeros((I16_ROWS, bq), I16))
        return jnp.sum(acc.astype(F32), axis=0, keepdims=True)

    def kth_largest16(ref, kth):
        def bit_step(i, biased):
            cand = biased | lax.shift_left(jnp.int32(1), 15 - i)
            cnt = count16_ge(ref, (cand - HALF_BIAS).astype(I16))
            return jnp.where(cnt >= kth, cand, biased)
        return lax.fori_loop(0, 16, bit_step, jnp.zeros((1, bq), I32)) - HALF_BIAS

    t_hi = kth_largest16(hi_ref, k_eff)
    above = jnp.where(t_hi >= HALF_BIAS - 1, 0.0,
                      count16_ge(hi_ref, jnp.minimum(t_hi + 1, HALF_BIAS - 1).astype(I16)))
    t_hi16 = t_hi.astype(I16)

    def keep_group(kb, _):
        lo_ref[kb] = jnp.where(hi_ref[kb] == t_hi16, lo_ref[kb], jnp.int16(-HALF_BIAS))
        return 0

    lax.fori_loop(0, nkb, keep_group, 0)
    t_lo = kth_largest16(lo_ref, k_eff - above)
    thr = t_hi * (2 * HALF_BIAS) + (t_lo + HALF_BIAS)
    above_lo = jnp.where(t_lo >= HALF_BIAS - 1, 0.0,
                         count16_ge(lo_ref, jnp.minimum(t_lo + 1, HALF_BIAS - 1).astype(I16)))
    need = k_eff - above - above_lo
    return thr, need


def _dsa_mask(key_ref, thr, need, *, nkb, bq, bk, sub):
    rs = lax.broadcasted_iota(I32, (sub, sub), 0)
    cs = lax.broadcasted_iota(I32, (sub, sub), 1)
    earlier = (cs < rs).astype(BF16)

    def mask_block(kb, seen):
        for j in range(bk // sub):
            key = key_ref[kb, j * sub:(j + 1) * sub, :]
            eq = jnp.where(key == thr, 1.0, 0.0)
            rank = _nn(earlier, eq.astype(BF16)) + seen
            take = jnp.where(rank < need, 1, 0)
            sel = (key + take) > thr
            key_ref[kb, j * sub:(j + 1) * sub, :] = lax.bitcast_convert_type(jnp.where(sel, 0.0, NEG_BIG), I32)
            seen = seen + jnp.sum(eq, axis=0, keepdims=True)
        return seen

    lax.fori_loop(0, nkb, mask_block, jnp.zeros((1, bq), F32))


def _dsa_attend(qd_ref, kd_ref, vt_ref, o_ref, key_ref, s_ref, *, nkb, bq, bk):
    lo_half, hi_half = _lane_half_masks()
    n_heads = qd_ref.shape[2] // HEAD_DIM
    qd = qd_ref[0]
    pairs_per_pass = DSA_HEADS_PER_PASS // 2

    def stacked_bias(kb):
        bias = lax.bitcast_convert_type(key_ref[kb], F32)
        return jnp.concatenate([bias] * DSA_HEADS_PER_PASS, axis=1)

    def vt_block(kb):
        return vt_ref[0, :, pl.ds(pl.multiple_of(kb * bk, bk), bk)]

    for g in range(n_heads // DSA_HEADS_PER_PASS):
        stacked = []
        for hp in range(g * pairs_per_pass, (g + 1) * pairs_per_pass):
            chunk = qd[:, hp * LANES:(hp + 1) * LANES]
            zero = jnp.zeros_like(chunk)
            stacked += [jnp.where(lo_half, chunk, zero), jnp.where(hi_half, chunk, zero)]
        qq = jnp.concatenate(stacked, axis=0)

        _, acc = _flash_pipelined_t(qq, kd_ref, vt_block, s_ref, nkb, bk, stacked_bias, stacked_bias)
        o = acc[:HEAD_DIM] / acc[HEAD_DIM:HEAD_DIM + 1]
        for j in range(pairs_per_pass):
            hp = g * pairs_per_pass + j
            pair = jnp.concatenate([o[:, 2 * j * bq:(2 * j + 1) * bq], o[:, (2 * j + 1) * bq:(2 * j + 2) * bq]],
                                   axis=0)
            o_ref[0, :, hp * LANES:(hp + 1) * LANES] = pair.T.astype(o_ref.dtype)


def _dsa_kernel(qd_ref, qx_ref, wt_ref, kd_ref, kx_ref, vt_ref, o_ref, key_ref, *, bq, bk, sub, topk):
    qb = pl.program_id(1)
    nkb = (qb * bq) // bk + 1
    halves16 = pltpu.VMEM(key_ref.shape, I16)
    thr, need = pl.run_scoped(
        functools.partial(_dsa_select, qx_ref, wt_ref, kx_ref, key_ref, qb=qb, nkb=nkb, bq=bq, bk=bk, topk=topk),
        halves16, halves16)
    _dsa_mask(key_ref, thr, need, nkb=nkb, bq=bq, bk=bk, sub=sub)
    pl.run_scoped(functools.partial(_dsa_attend, qd_ref, kd_ref, vt_ref, o_ref, key_ref, nkb=nkb, bq=bq, bk=bk),
                  pltpu.VMEM((2, bk, DSA_HEADS_PER_PASS * bq), F32))


def _dsa_attention(qd, qx, wt, kd2, kx2, vt2, blk, bk, topk):
    bsz, s, wq = qd.shape
    kspec = pl.BlockSpec((1, s, LANES), lambda b, i: (b, 0, 0), pipeline_mode=pl.Buffered(1))
    tiles = (s // bk, bk, blk)
    return pl.pallas_call(
        functools.partial(_dsa_kernel, bq=blk, bk=bk, sub=min(bk, 256), topk=topk),
        grid=(bsz, s // blk),
        in_specs=[pl.BlockSpec((1, blk, wq), lambda b, i: (b, i, 0)),
                  pl.BlockSpec((1, blk, qx.shape[2]), lambda b, i: (b, i, 0)),
                  pl.BlockSpec((1, wt.shape[1], blk), lambda b, i: (b, 0, i)),
                  kspec, kspec,
                  pl.BlockSpec((1, vt2.shape[1], s), lambda b, i: (0, 0, b), pipeline_mode=pl.Buffered(1))],
        out_specs=pl.BlockSpec((1, blk, wq), lambda b, i: (b, i, 0)),
        out_shape=jax.ShapeDtypeStruct((bsz, s, wq), BF16),
        scratch_shapes=[pltpu.VMEM(tiles, I32)],
        name="dsa_attention",
        compiler_params=_cparams(("arbitrary", "arbitrary")),
    )(qd, qx, wt, kd2, kx2, vt2)


def _diff_kernel(q_ref, k_ref, vt_ref, lam_ref, g_ref, o_ref, s_ref, *, bq, bk, lambda_init):
    qi = pl.program_id(2)
    heads = DIFF_HEADS_PER_STEP
    width = 2 * bq
    lo_half, hi_half = _lane_half_masks()
    qqs = []
    for h in range(heads):
        q = q_ref[0, :, h * LANES:(h + 1) * LANES]
        zero = jnp.zeros_like(q)
        qqs.append(jnp.concatenate([jnp.where(lo_half, q, zero), jnp.where(hi_half, q, zero)], axis=0))
    n_full = (qi * bq) // bk

    def qk(kb, slot):
        off = pl.multiple_of(kb * bk, bk)
        for h in range(heads):
            s_ref[slot, :, h * width:(h + 1) * width] = _nt(k_ref[0, pl.ds(off, bk), h * LANES:(h + 1) * LANES], qqs[h])

    def consume(kb, slot, carry):
        def vt_chunk(c):
            return vt_ref[0, :, pl.ds(pl.multiple_of(kb * bk + c * FLASH_CHUNK, FLASH_CHUNK), FLASH_CHUNK)]
        return _flash_update_chunked(s_ref, slot, *carry, vt_chunk, FLASH_CHUNK)

    def consume_diag(n_sub, carry):
        qk(0, 0)
        nk = n_sub * bq
        s = s_ref[1, :nk, :]
        s_last = jnp.where(_tri_mask(bq, bq, 2 * heads, strict=False), s[nk - bq:], NEG_BIG)
        s = s_last if n_sub == 1 else jnp.concatenate([s[:nk - bq], s_last], axis=0)
        off = pl.multiple_of(n_full * bk, bk)
        return _flash_update_t(s, *carry, [vt_ref[h, :, pl.ds(off, nk)] for h in range(heads)])

    def pair(p, carry):
        qk(2 * p + 1, 1)
        carry = consume(2 * p, 0, carry)
        qk(2 * p + 2, 0)
        return consume(2 * p + 1, 1, carry)

    qk(n_full, 1)
    diag_sub = (qi * bq - n_full * bk) // bq
    carry = _switch(diag_sub, [functools.partial(consume_diag, v + 1) for v in range(bk // bq)],
                    _flash_init(vt_ref.shape[1], heads * width))
    carry = lax.fori_loop(0, n_full // 2, pair, carry)
    _, acc_all = lax.cond(n_full % 2 == 1, lambda cr: consume(n_full - 1, 0, cr), lambda cr: cr, carry)

    lp = lam_ref[...]
    lam = (jnp.exp(jnp.sum(lp[0:1] * lp[1:2], axis=1, keepdims=True))
           - jnp.exp(jnp.sum(lp[2:3] * lp[3:4], axis=1, keepdims=True)) + lambda_init)
    for h in range(heads):
        l = acc_all[LANES:LANES + 1, h * width:(h + 1) * width]
        acc = acc_all[:LANES, h * width:(h + 1) * width]
        o = acc[:, :bq] / l[:, :bq] - lam * (acc[:, bq:] / l[:, bq:])
        ms = jnp.mean(o * o, axis=0, keepdims=True)
        o = o * lax.rsqrt(ms + LN_EPS) * g_ref[...] * (1.0 - lambda_init)
        o_ref[0, :, h * LANES:(h + 1) * LANES] = o.T.astype(o_ref.dtype)


def _diff_attention(q, k, vt, lam_params, subln_g, blk, bk, lambda_init):
    bsz, s, w = q.shape
    heads = DIFF_HEADS_PER_STEP
    once = pl.Buffered(1)
    return pl.pallas_call(
        functools.partial(_diff_kernel, bq=blk, bk=bk, lambda_init=lambda_init),
        grid=(bsz, w // (heads * LANES), s // blk),
        in_specs=[pl.BlockSpec((1, blk, heads * LANES), lambda b, h, i: (b, i, h)),
                  pl.BlockSpec((1, s, heads * LANES), lambda b, h, i: (b, 0, h), pipeline_mode=once),
                  pl.BlockSpec((heads, vt.shape[1], s), lambda b, h, i: (h, 0, b), pipeline_mode=once),
                  pl.BlockSpec(lam_params.shape, lambda b, h, i: (0, 0)),
                  pl.BlockSpec((LANES, 1), lambda b, h, i: (0, 0))],
        out_specs=pl.BlockSpec((1, blk, heads * LANES), lambda b, h, i: (b, i, h)),
        out_shape=jax.ShapeDtypeStruct((bsz, s, w), BF16),
        scratch_shapes=[pltpu.VMEM((2, bk, heads * 2 * blk), F32)],
        name="diff_attention",
        compiler_params=_cparams(("arbitrary", "arbitrary", "arbitrary")),
    )(q, k, vt, lam_params, subln_g.reshape(LANES, 1))


def _pick(n, prefs):
    for p in prefs:
        if n % p == 0:
            return p
    return n


def kernel(x, c, positions, w_mod, b_mod, w_in_even, w_out_even, w_in_odd, lam_q1, lam_k1, lam_q2, lam_k2,
           subln_g, w_out_odd, ln_mix_g, ln_mix_b, w_gate, w_up, w_down, ln_ffn_g, ln_ffn_b):
    bsz, s, d = x.shape
    depth = w_mod.shape[0]
    alpha = (2 * depth) ** 0.25
    dff = w_gate.shape[2]
    rows = bsz * s
    tm = _pick(s, (512, 256, 128))
    blk = _pick(s, (256, 128))
    bk_dsa = _pick(s, (1024, 512, 256))
    bk_diff = _pick(s, (1024, 512, 256))
    bq_diff = _pick(s, (1024, 512, 256, 128))
    assert s % tm == 0 and s % blk == 0 and bk_dsa % blk == 0 and bk_diff % bq_diff == 0, (s, tm, blk)
    assert d % (2 * LANES) == 0 and dff % LANES == 0, (d, dff)
    scale = HEAD_DIM ** -0.5

    inv = ROPE_THETA ** (-jnp.arange(0, HEAD_DIM, 2, dtype=F32) / HEAD_DIM)
    ang = positions.astype(F32)[..., None] * inv
    cos_tab, sin_tab = jnp.cos(ang).reshape(rows, -1), jnp.sin(ang).reshape(rows, -1)

    mod = _modulation(c, w_mod.astype(BF16), b_mod)
    xf = x.reshape(rows, d)

    for i in range(depth):
        mod_l = mod[i].reshape(bsz, 6, d)
        if i % 2 == 0:
            w = w_in_even[i // 2]
            n_sb = n_dsa = d // (2 * HEAD_DIM)
            sbw, dsw, ixw = n_sb * HEAD_DIM, n_dsa * HEAD_DIM, N_IDX_HEADS * HEAD_DIM
            offs = [0]
            for width in (sbw, sbw, sbw, dsw, HEAD_DIM, HEAD_DIM, ixw, HEAD_DIM, N_IDX_HEADS):
                offs.append(offs[-1] + width)
            col = lambda j: w[:, offs[j]:offs[j + 1]]
            pad = jnp.zeros((d, LANES - N_IDX_HEADS), w.dtype)
            wp = jnp.concatenate([col(0), col(1), col(2), col(5), col(5), col(8), pad], axis=1)
            wr = jnp.concatenate([col(3), col(6), col(4), col(4), col(7), col(7)], axis=1)
            plain_outs = ((0, sbw, scale), (sbw, sbw, 1.0), (3 * sbw + LANES, LANES, N_IDX_HEADS ** -0.5))
            rope_outs = ((0, dsw, scale), (dsw, ixw, scale), (dsw + ixw, LANES, 1.0),
                         (dsw + ixw + LANES, LANES, 1.0))
            vt_outs = ((2 * sbw, sbw, LANES, 0), (3 * sbw, LANES, HEAD_DIM, ONES_ROWS))
            dts = (BF16, BF16, F32, BF16, BF16, BF16, BF16)
            q_sb, k_sb, wx, qd, qx, kd2, kx2, vt_sb, vt2 = _inproj(
                xf, mod_l, cos_tab, sin_tab, wp.astype(BF16), wr.astype(BF16), _rot_partner(wr).astype(BF16),
                plain_outs, rope_outs, vt_outs, dts, s, tm)
            r3 = lambda t: t.reshape(bsz, s, t.shape[1])
            o_sb = _sb_attention(r3(q_sb), r3(k_sb), vt_sb, blk)
            wt = r3(wx)[:, :, :8].swapaxes(1, 2)
            o_dsa = _dsa_attention(r3(qd), r3(qx), wt, r3(kd2), r3(kx2), vt2, blk, bk_dsa,
                                   min(DSA_TOPK_MAX, s // 4))
            w_out = w_out_even[i // 2].astype(BF16)
            o_list = [o_sb.reshape(rows, sbw), o_dsa.reshape(rows, dsw)]
            w_list = [w_out[:sbw], w_out[sbw:]]
        else:
            j = i // 2
            w = w_in_odd[j]
            dw = w.shape[1] // 3
            lambda_init = 0.8 - 0.6 * math.exp(-0.3 * i)
            wp = w[:, 2 * dw:]
            wr = w[:, :2 * dw]
            rope_outs = ((0, dw, scale), (dw, dw, 1.0))
            vt_outs = ((0, dw, LANES, ONES_ROWS),)
            q_df, k_df, vt = _inproj(
                xf, mod_l, cos_tab, sin_tab, wp.astype(BF16), wr.astype(BF16), _rot_partner(wr).astype(BF16),
                (), rope_outs, vt_outs, (BF16, BF16), s, tm)
            r3 = lambda t: t.reshape(bsz, s, t.shape[1])
            lam_params = jnp.stack([lam_q1[j], lam_k1[j], lam_q2[j], lam_k2[j]]).astype(F32)
            o_df = _diff_attention(r3(q_df), r3(k_df), vt, lam_params, subln_g[j].astype(F32), bq_diff, bk_diff,
                                   lambda_init)
            o_list = [o_df.reshape(rows, dw)]
            w_list = [w_out_odd[j].astype(BF16)]
        xf = _post_mixer(o_list, w_list, xf, mod_l, ln_mix_g[i], ln_mix_b[i], w_gate[i].astype(BF16),
                         w_up[i].astype(BF16), w_down[i].astype(BF16), ln_ffn_g[i], ln_ffn_b[i], alpha, s, tm)
    return xf.reshape(bsz, s, d)
```
